```python
import math
import jax, jax.numpy as jnp
from jax import lax
import numpy as np

D_MODEL = 4096
BATCH = 4
SEQ = 2048
DEPTH = 1

CHUNK = 64
Q_BLOCK = 128

MLA_HEADS = 16
MLA_NOPE_DIM = 128
MLA_ROPE_DIM = 64
MLA_V_DIM = 128
MLA_Q_LORA = 1024
MLA_KV_LORA = 512
ROPE_THETA = 10000.0
MLA_OUT_WIDTH = MLA_HEADS * MLA_V_DIM

HG_HEADS = 16
HG_K_DIM = 128
HG_V_DIM = 128
HG_BLOCK = 16
HG_WIDTH_K = HG_HEADS * HG_K_DIM
HG_WIDTH_V = HG_HEADS * HG_V_DIM

D_FF = -(-8 * D_MODEL // (3 * 256)) * 256

ALPHA = (2 * DEPTH) ** 0.25
BETA = (8 * DEPTH) ** -0.25
EPS = 1e-5

SPLITS = (MLA_Q_LORA, MLA_KV_LORA, MLA_ROPE_DIM,
          HG_WIDTH_K, HG_WIDTH_K, HG_WIDTH_V, HG_WIDTH_V,
          D_MODEL, D_MODEL)
IN_WIDTH = sum(SPLITS)

kernel_name = "mla_hgrn2_gated_hybrid_deepnorm"


def layer_norm(x, g, b):
    xf = x.astype(jnp.float32)
    mu = jnp.mean(xf, axis=-1, keepdims=True)
    var = jnp.mean(jnp.square(xf - mu), axis=-1, keepdims=True)
    return ((xf - mu) * lax.rsqrt(var + EPS) * g + b).astype(x.dtype)


def rms_norm(x, g):
    xf = x.astype(jnp.float32)
    ms = jnp.mean(jnp.square(xf), axis=-1, keepdims=True)
    return (xf * lax.rsqrt(ms + EPS) * g).astype(x.dtype)


def rope(x, positions):
    half = x.shape[-1] // 2
    inv_freq = ROPE_THETA ** (-jnp.arange(half, dtype=jnp.float32) / half)
    ang = positions[..., None].astype(jnp.float32) * inv_freq
    cos = jnp.cos(ang)[:, :, None, :]
    sin = jnp.sin(ang)[:, :, None, :]
    xf = x.astype(jnp.float32)
    x1, x2 = xf[..., :half], xf[..., half:]
    return jnp.concatenate([x1 * cos - x2 * sin, x1 * sin + x2 * cos], axis=-1).astype(x.dtype)


def split_columns(p):
    points = np.cumsum(np.array(SPLITS))[:-1].tolist()
    return jnp.split(p, points, axis=-1)


def mla_attention(q_nope, q_rope, k_nope, k_rope, v):
    T = q_nope.shape[1]
    scale = (MLA_NOPE_DIM + MLA_ROPE_DIM) ** -0.5
    outs = []
    for j in range(T // Q_BLOCK):
        s0, s1 = j * Q_BLOCK, (j + 1) * Q_BLOCK
        scores = (jnp.einsum('bqhd,bkhd->bhqk', q_nope[:, s0:s1], k_nope[:, :s1])
                  + jnp.einsum('bqhr,bkr->bhqk', q_rope[:, s0:s1], k_rope[:, :s1]))
        scores = scores.astype(jnp.float32) * scale
        q_chunk = (s0 + jnp.arange(Q_BLOCK)) // CHUNK
        k_chunk = jnp.arange(s1) // CHUNK
        scores = jnp.where(k_chunk[None, :] <= q_chunk[:, None], scores, -jnp.inf)
        probs = jax.nn.softmax(scores, axis=-1).astype(v.dtype)
        outs.append(jnp.einsum('bhqk,bkhd->bqhd', probs, v[:, :s1]))
    return jnp.concatenate(outs, axis=1)


def hgrn2_chunkwise(q, k, v, log_f):
    B, T, H, K = q.shape
    V = v.shape[-1]
    L = HG_BLOCK
    N = T // L

    def blocks(a):
        a = a.astype(jnp.float32)
        return a.reshape(B, N, L, H, a.shape[-1]).transpose(1, 0, 3, 2, 4)

    qb, kb, vb, gb = blocks(q), blocks(k), blocks(v), blocks(log_f)
    causal = jnp.tril(jnp.ones((L, L), dtype=bool))[:, :, None]

    def step(S, blk):
        qc, kc, vc, gc = blk
        b = jnp.cumsum(gc, axis=-2)
        diff = b[:, :, :, None, :] - b[:, :, None, :, :]
        decay = jnp.exp(jnp.where(causal, diff, -jnp.inf))
        scores = jnp.einsum('bhtk,bhsk,bhtsk->bhts', qc, kc, decay)
        o = (jnp.einsum('bhts,bhsv->bhtv', scores, vc)
             + jnp.einsum('bhtk,bhkv->bhtv', qc * jnp.exp(b), S))
        b_last = b[:, :, -1:, :]
        S_new = (jnp.exp(b_last[:, :, 0, :])[..., None] * S
                 + jnp.einsum('bhsk,bhsv->bhkv', kc * jnp.exp(b_last - b), vc))
        return S_new, o

    S0 = jnp.zeros((B, H, K, V), jnp.float32)
    _, o = lax.scan(step, S0, (qb, kb, vb, gb))
    return o.transpose(1, 0, 3, 2, 4).reshape(B, T, H, V)


def token_mixers(h, positions, w_in, q_norm_g, w_uq, kv_norm_g, w_ukv, lb,
                 hg_norm_g, w_branch_a, w_branch_b, w_out):
    B, T, _ = h.shape
    proj = h @ w_in
    c_q, c_kv, k_rope, hq, hf, hi, hgate, gate_a, gate_b = split_columns(proj)

    q = (rms_norm(c_q, q_norm_g) @ w_uq).reshape(B, T, MLA_HEADS, MLA_NOPE_DIM + MLA_ROPE_DIM)
    q_nope = q[..., :MLA_NOPE_DIM]
    q_rope = rope(q[..., MLA_NOPE_DIM:], positions)
    kv = (rms_norm(c_kv, kv_norm_g) @ w_ukv).reshape(B, T, MLA_HEADS, MLA_NOPE_DIM + MLA_V_DIM)
    k_nope, v = kv[..., :MLA_NOPE_DIM], kv[..., MLA_NOPE_DIM:]
    k_rope = rope(k_rope[:, :, None, :], positions)[:, :, 0, :]
    o_a = mla_attention(q_nope, q_rope, k_nope, k_rope, v).reshape(B, T, MLA_OUT_WIDTH)

    f = lb + (1.0 - lb) * jax.nn.sigmoid(hf.astype(jnp.float32))
    log_f = jnp.log(f).reshape(B, T, HG_HEADS, HG_K_DIM)
    k_in = (1.0 - f).reshape(B, T, HG_HEADS, HG_K_DIM)
    q_hg = jax.nn.silu(hq).reshape(B, T, HG_HEADS, HG_K_DIM)
    i_hg = hi.reshape(B, T, HG_HEADS, HG_V_DIM)
    o_hg = hgrn2_chunkwise(q_hg, k_in, i_hg, log_f)
    g_hg = jax.nn.silu(hgate.astype(jnp.float32)).reshape(B, T, HG_HEADS, HG_V_DIM)
    o_b = (rms_norm(o_hg, hg_norm_g) * g_hg).reshape(B, T, HG_WIDTH_V).astype(h.dtype)

    merged = (jax.nn.sigmoid(gate_a) * (o_a @ w_branch_a)
              + jax.nn.sigmoid(gate_b) * (o_b @ w_branch_b))
    return merged @ w_out


def swiglu(h, w_gate, w_up, w_down):
    return (jax.nn.silu(h @ w_gate) * (h @ w_up)) @ w_down


def setup_inputs(seed: int = 0) -> dict:
    key = jax.random.key(seed)
    ks = jax.random.split(key, 24)

    def normal(k, shape, scale):
        return jax.random.normal(k, shape, jnp.float32) * scale

    def gain(k, shape):
        return 1.0 + normal(k, shape, 0.02)

    x = normal(ks[0], (BATCH, SEQ, D_MODEL), 1.0)
    offsets = jax.random.randint(ks[1], (BATCH, 1), 0, 4096, dtype=jnp.int32)
    positions = offsets + jnp.arange(SEQ, dtype=jnp.int32)[None, :]
    return {
        "x": x,
        "positions": positions,
        "ln_in_g": gain(ks[2], (D_MODEL,)),
        "ln_in_b": normal(ks[3], (D_MODEL,), 0.02),
        "w_in": normal(ks[4], (DEPTH, D_MODEL, IN_WIDTH), D_MODEL ** -0.5),
        "q_norm_g": gain(ks[5], (DEPTH, MLA_Q_LORA)),
        "w_uq": normal(ks[6], (DEPTH, MLA_Q_LORA, MLA_HEADS * (MLA_NOPE_DIM + MLA_ROPE_DIM)), MLA_Q_LORA ** -0.5),
        "kv_norm_g": gain(ks[7], (DEPTH, MLA_KV_LORA)),
        "w_ukv": normal(ks[8], (DEPTH, MLA_KV_LORA, MLA_HEADS * (MLA_NOPE_DIM + MLA_V_DIM)), MLA_KV_LORA ** -0.5),
        "hg_lb": normal(ks[9], (DEPTH + 1, HG_WIDTH_K), 0.1),
        "hg_norm_g": gain(ks[10], (DEPTH, HG_V_DIM)),
        "w_branch_a": normal(ks[11], (DEPTH, MLA_OUT_WIDTH, D_MODEL), MLA_OUT_WIDTH ** -0.5),
        "w_branch_b": normal(ks[12], (DEPTH, HG_WIDTH_V, D_MODEL), HG_WIDTH_V ** -0.5),
        "w_out": normal(ks[13], (DEPTH, D_MODEL, D_MODEL), BETA * D_MODEL ** -0.5),
        "ln1_g": gain(ks[14], (DEPTH, D_MODEL)),
        "ln1_b": normal(ks[15], (DEPTH, D_MODEL), 0.02),
        "w_gate": normal(ks[16], (DEPTH, D_MODEL, D_FF), D_MODEL ** -0.5),
        "w_up": normal(ks[17], (DEPTH, D_MODEL, D_FF), D_MODEL ** -0.5),
        "w_down": normal(ks[18], (DEPTH, D_FF, D_MODEL), BETA * D_FF ** -0.5),
        "ln2_g": gain(ks[19], (DEPTH, D_MODEL)),
        "ln2_b": normal(ks[20], (DEPTH, D_MODEL), 0.02),
    }


def reference(x, positions, ln_in_g, ln_in_b, w_in, q_norm_g, w_uq, kv_norm_g, w_ukv,
              hg_lb, hg_norm_g, w_branch_a, w_branch_b, w_out, ln1_g, ln1_b,
              w_gate, w_up, w_down, ln2_g, ln2_b):
    h = layer_norm(x, ln_in_g, ln_in_b)
    lb_all = jnp.cumsum(jax.nn.softmax(hg_lb.astype(jnp.float32), axis=0), axis=0)
    for l in range(DEPTH):
        mix = token_mixers(h, positions, w_in[l], q_norm_g[l], w_uq[l], kv_norm_g[l], w_ukv[l],
                           lb_all[l], hg_norm_g[l], w_branch_a[l], w_branch_b[l], w_out[l])
        h = layer_norm(ALPHA * h + mix, ln1_g[l], ln1_b[l])
        h = layer_norm(ALPHA * h + swiglu(h, w_gate[l], w_up[l], w_down[l]), ln2_g[l], ln2_b[l])
    return h
```

```python
import functools
import math

import jax
import jax.numpy as jnp
from jax import lax
from jax.experimental import pallas as pl
from jax.experimental.pallas import tpu as pltpu

F32 = jnp.float32
BF16 = jnp.bfloat16

D_MODEL = 4096
CHUNK = 64
MLA_HEADS = 16
NOPE = 128
ROPE = 64
V_DIM = 128
Q_LORA = 1024
KV_LORA = 512
ROPE_THETA = 10000.0
HG_HEADS = 16
HG_DIM = 128
HG_WIDTH = HG_HEADS * HG_DIM
D_FF = 11008
D_FF_PAD = 11264
DEPTH = 1
ALPHA = (2 * DEPTH) ** 0.25
EPS = 1e-5

LANES = 128
VMEM_LIMIT = 56 * 1024 * 1024

HEAD_W = 2 * LANES
SUB = 16
HG_CHUNK = 128
ATT_TQ = 256


def _cparams(n_grid, vmem=VMEM_LIMIT):
    return pltpu.CompilerParams(dimension_semantics=("arbitrary",) * n_grid,
                                vmem_limit_bytes=vmem)


def _mm_kernel(*refs, pair_x, n_x, n_extra, n_out, nk, epilogue):
    n_pairs = len(pair_x)
    xs = refs[:n_x]
    ws = refs[n_x:n_x + n_pairs]
    extras = refs[n_x + n_pairs:n_x + n_pairs + n_extra]
    outs = refs[n_x + n_pairs + n_extra:n_x + n_pairs + n_extra + n_out]
    accs = refs[n_x + n_pairs + n_extra + n_out:]

    def finish(vals):
        res = epilogue(vals, [e[...] for e in extras])
        for o, r in zip(outs, res):
            o[...] = r.astype(o.dtype)

    parts = [jnp.dot(xs[xi][...], w[...], preferred_element_type=F32)
             for xi, w in zip(pair_x, ws)]
    if nk == 1:
        finish(parts)
        return

    k = pl.program_id(2)

    @pl.when(k == 0)
    def _():
        for a, p in zip(accs, parts):
            a[...] = p

    @pl.when(k > 0)
    def _():
        for a, p in zip(accs, parts):
            a[...] += p

    @pl.when(k == nk - 1)
    def _():
        finish([a[...] for a in accs])


def _matmul(xs, pairs, epilogue, outs, *, tm, tn, tk=None, extras=()):
    m = xs[0].shape[0]
    n = pairs[0][1].shape[1]
    kdims = [x.shape[1] for x in xs]
    if tk is None:
        nk = 1
    else:
        assert len(set(kdims)) == 1 and kdims[0] % tk == 0
        nk = kdims[0] // tk
    assert m % tm == 0 and n % tn == 0
    grid = (m // tm, n // tn, nk)

    in_specs = []
    for x in xs:
        kb = x.shape[1] if nk == 1 else tk
        in_specs.append(pl.BlockSpec((tm, kb), lambda i, j, k: (i, k)))
    for xi, w in pairs:
        kb = w.shape[0] if nk == 1 else tk
        in_specs.append(pl.BlockSpec((kb, tn), lambda i, j, k: (k, j)))
    for _, bs, im in extras:
        in_specs.append(pl.BlockSpec(bs, functools.partial(_drop_k, im)))
    out_specs = [pl.BlockSpec(bs, functools.partial(_drop_k, im)) for _, _, bs, im in outs]
    out_shape = [jax.ShapeDtypeStruct(s, d) for s, d, _, _ in outs]
    scratch = [pltpu.VMEM((tm, tn), F32) for _ in pairs] if nk > 1 else []

    kern = functools.partial(_mm_kernel, pair_x=tuple(p[0] for p in pairs), n_x=len(xs),
                             n_extra=len(extras), n_out=len(outs), nk=nk, epilogue=epilogue)
    res = pl.pallas_call(
        kern, grid=grid, in_specs=in_specs, out_specs=out_specs, out_shape=out_shape,
        scratch_shapes=scratch, compiler_params=_cparams(3),
    )(*xs, *[p[1] for p in pairs], *[e[0] for e in extras])
    return res


def _drop_k(im, i, j, k):
    return im(i, j)


def _tile_ij(i, j):
    return (i, j)


def _row_i(i, j):
    return (i, 0)


def _col_j(i, j):
    return (0, j)


def _sigmoid(x):
    return 1.0 / (1.0 + jnp.exp(-x))


def _silu(x):
    return x * _sigmoid(x)


def _ln_kernel(y_ref, g_ref, b_ref, *outs):
    y = y_ref[...]
    mu = jnp.mean(y, axis=-1, keepdims=True)
    d = y - mu
    var = jnp.mean(d * d, axis=-1, keepdims=True)
    r = d * lax.rsqrt(var + EPS) * g_ref[...] + b_ref[...]
    for o in outs:
        o[...] = r.astype(o.dtype)


def _layer_norm(y, g, b, out_dtypes, tm=256):
    m, d = y.shape
    return pl.pallas_call(
        _ln_kernel, grid=(m // tm,),
        in_specs=[pl.BlockSpec((tm, d), lambda i: (i, 0)),
                  pl.BlockSpec((1, d), lambda i: (0, 0)),
                  pl.BlockSpec((1, d), lambda i: (0, 0))],
        out_specs=[pl.BlockSpec((tm, d), lambda i: (i, 0)) for _ in out_dtypes],
        out_shape=[jax.ShapeDtypeStruct((m, d), dt) for dt in out_dtypes],
        compiler_params=_cparams(1),
    )(y, g.reshape(1, d), b.reshape(1, d))


def _rope_table_kernel(pos_ref, invf_ref, c_ref, sa_ref, sb_ref):
    ang = pos_ref[...].astype(F32) * invf_ref[...]
    lane = lax.broadcasted_iota(jnp.int32, ang.shape, 1)
    cos = jnp.cos(ang)
    sin = jnp.sin(ang)
    half = ROPE // 2
    c_ref[...] = cos
    sa_ref[...] = jnp.where(lane < half, -sin, 0.0)
    sb_ref[...] = jnp.where((lane >= half) & (lane < ROPE), sin, 0.0)


def _rope_tables(positions, tm=1024):
    n = positions.size
    half = ROPE // 2
    inv_freq = ROPE_THETA ** (-jnp.arange(half, dtype=F32) / half)
    invf = jnp.concatenate([inv_freq, inv_freq, jnp.zeros((LANES - ROPE,), F32)]).reshape(1, LANES)
    pos = positions.reshape(n, 1)
    return pl.pallas_call(
        _rope_table_kernel, grid=(n // tm,),
        in_specs=[pl.BlockSpec((tm, 1), lambda i: (i, 0)),
                  pl.BlockSpec((1, LANES), lambda i: (0, 0))],
        out_specs=[pl.BlockSpec((tm, LANES), lambda i: (i, 0))] * 3,
        out_shape=[jax.ShapeDtypeStruct((n, LANES), F32)] * 3,
        compiler_params=_cparams(1),
    )(pos, invf)


def _rope_lanes(x, c, sa, sb):
    half = ROPE // 2
    return (x * c + pltpu.roll(x, LANES - half, axis=1) * sa
            + pltpu.roll(x, half, axis=1) * sb)


def _rms(x, g):
    ms = jnp.mean(x * x, axis=-1, keepdims=True)
    return x * lax.rsqrt(ms + EPS) * g


def _latent_epilogue(accs, extras):
    acc = accs[0]
    gq, gkv, c, sa, sb = extras
    qn = _rms(acc[:, :Q_LORA], gq)
    kvn = _rms(acc[:, Q_LORA:Q_LORA + KV_LORA], gkv)
    kr = _rope_lanes(acc[:, Q_LORA + KV_LORA:], c, sa, sb)
    return [qn, kvn, kr]


def _uq_epilogue(accs, extras, *, scale):
    acc = accs[0]
    c, sa, sb = extras
    tn = acc.shape[1]
    cols = []
    for h in range(tn // HEAD_W):
        base = h * HEAD_W
        cols.append(acc[:, base:base + NOPE] * scale)
        cols.append(_rope_lanes(acc[:, base + NOPE:base + HEAD_W], c, sa, sb) * scale)
    return [jnp.concatenate(cols, axis=1)]


def _attn_kernel(q_ref, kn_ref, v_ref, kr_ref, o_ref, kcat_ref, *, tq):
    t = q_ref.shape[0]
    kcat_ref[:, :NOPE] = kn_ref[...]
    kcat_ref[:, NOPE:] = kr_ref[...]
    row_chunk = lax.broadcasted_iota(jnp.int32, (tq, tq), 0) // CHUNK
    col_chunk = lax.broadcasted_iota(jnp.int32, (tq, tq), 1) // CHUNK
    visible = col_chunk <= row_chunk
    nt = (((1,), (1,)), ((), ()))
    for jq in range(t // tq):
        s0, s1 = jq * tq, (jq + 1) * tq
        q = q_ref[s0:s1, :]
        sd = lax.dot_general(q, kcat_ref[s0:s1, :], nt, preferred_element_type=F32)
        sd = jnp.where(visible, sd, -jnp.inf)
        m = jnp.max(sd, axis=-1, keepdims=True)
        if jq > 0:
            sp = lax.dot_general(q, kcat_ref[0:s0, :], nt, preferred_element_type=F32)
            m = jnp.maximum(m, jnp.max(sp, axis=-1, keepdims=True))
        pd = jnp.exp(sd - m)
        l = jnp.sum(pd, axis=-1, keepdims=True)
        o = jnp.dot(pd.astype(BF16), v_ref[s0:s1, :], preferred_element_type=F32)
        if jq > 0:
            pp = jnp.exp(sp - m)
            l = l + jnp.sum(pp, axis=-1, keepdims=True)
            o = o + jnp.dot(pp.astype(BF16), v_ref[0:s0, :], preferred_element_type=F32)
        o_ref[s0:s1, :] = (o / l).astype(o_ref.dtype)


def _attention(qf, kv, kr, batch, seq, heads):
    n = batch * seq
    kern = functools.partial(_attn_kernel, tq=ATT_TQ)
    return pl.pallas_call(
        kern, grid=(batch, heads),
        in_specs=[pl.BlockSpec((seq, HEAD_W), lambda b, h: (b, h)),
                  pl.BlockSpec((seq, NOPE), lambda b, h: (b, h)),
                  pl.BlockSpec((seq, V_DIM), lambda b, h: (b, heads + h)),
                  pl.BlockSpec((seq, LANES), lambda b, h: (b, 0))],
        out_specs=pl.BlockSpec((seq, V_DIM), lambda b, h: (b, h)),
        out_shape=jax.ShapeDtypeStruct((n, heads * V_DIM), BF16),
        scratch_shapes=[pltpu.VMEM((seq, HEAD_W), BF16)],
        compiler_params=_cparams(2),
    )(qf, kv, kv, kr)


def _split3(x):
    a = x.astype(BF16)
    r = x - a.astype(F32)
    b = r.astype(BF16)
    c = (r - b.astype(F32)).astype(BF16)
    return a, b, c


def _bcast_rows(x, rows, reps):
    return jnp.concatenate(
        [jnp.broadcast_to(x[r:r + 1, :], (reps, x.shape[1])) for r in rows], axis=0)


def _hgrn_kernel(q_ref, f_ref, i_ref, gs_ref, gn_ref, o_ref, st_ref):
    t = q_ref.shape[0]
    c = HG_CHUNK
    d = HG_DIM
    n_sub = c // SUB
    row = lax.broadcasted_iota(jnp.int32, (c, d), 0)
    col = lax.broadcasted_iota(jnp.int32, (c, d), 1)
    tri = (col <= row).astype(BF16)
    ones = jnp.ones((d, d), BF16)
    row_in_sub = row % SUB
    diag_lane = col - (row // SUB) * SUB
    nt = (((1,), (1,)), ((), ()))

    st_ref[...] = jnp.zeros_like(st_ref)

    def chunk(ci, carry):
        r0 = pl.multiple_of(ci * c, c)
        q = q_ref[pl.ds(r0, c), :].astype(F32)
        f = f_ref[pl.ds(r0, c), :]
        v = i_ref[pl.ds(r0, c), :]
        g = jnp.log(f)
        k = 1.0 - f
        g1, g2, g3 = _split3(g)
        b = (jnp.dot(tri, g1, preferred_element_type=F32)
             + jnp.dot(tri, g2, preferred_element_type=F32)
             + jnp.dot(tri, g3, preferred_element_type=F32))
        b_last = b[c - 1:c, :]

        p = jnp.zeros((c, c), F32)
        for s in range(SUB):
            rows = [i * SUB + s for i in range(n_sub)]
            ks = _bcast_rows(k, rows, SUB)
            bs = _bcast_rows(b, rows, SUB)
            e = jnp.exp(jnp.where(row_in_sub >= s, b - bs, -jnp.inf))
            a = (q * ks * e).astype(BF16)
            rsum = jnp.dot(a, ones, preferred_element_type=F32)
            p = p + jnp.where(diag_lane == s, rsum, 0.0)

        hs = c // 2
        while hs >= SUB:
            blk = 2 * hs
            rows = [(i // blk) * blk + hs - 1 for i in range(0, c, hs)]
            bb = _bcast_rows(b, rows, hs)
            upper = (row % blk) >= hs
            e = jnp.exp(jnp.where(upper, b - bb, bb - b))
            qt = jnp.where(upper, q * e, 0.0).astype(BF16)
            kt = jnp.where(upper, 0.0, k * e).astype(BF16)
            sc = lax.dot_general(qt, kt, nt, preferred_element_type=F32)
            p = p + jnp.where((row // blk) == (col // blk), sc, 0.0)
            hs //= 2

        o = jnp.dot(p.astype(BF16), v, preferred_element_type=F32)

        st = st_ref[...]
        q0 = (q * jnp.exp(b)).astype(BF16)
        o = o + lax.dot_general(q0, st.astype(BF16), nt, preferred_element_type=F32)
        kl = (k * jnp.exp(b_last - b)).astype(BF16)
        vt = v.astype(F32).T.astype(BF16)
        st_ref[...] = jnp.exp(b_last) * st + jnp.dot(vt, kl, preferred_element_type=F32)

        ms = jnp.mean(o * o, axis=-1, keepdims=True)
        ob = o * lax.rsqrt(ms + EPS) * gn_ref[...] * gs_ref[pl.ds(r0, c), :].astype(F32)
        o_ref[pl.ds(r0, c), :] = ob.astype(o_ref.dtype)
        return carry

    lax.fori_loop(0, t // c, chunk, 0)


def _hgrn2(qs, f, iv, gs, gnorm, batch, seq, heads):
    n = batch * seq
    spec = pl.BlockSpec((seq, HG_DIM), lambda b, h: (b, h))
    return pl.pallas_call(
        _hgrn_kernel, grid=(batch, heads),
        in_specs=[spec, spec, spec, spec, pl.BlockSpec((1, HG_DIM), lambda b, h: (0, 0))],
        out_specs=spec,
        out_shape=jax.ShapeDtypeStruct((n, heads * HG_DIM), BF16),
        scratch_shapes=[pltpu.VMEM((HG_DIM, HG_DIM), F32)],
        compiler_params=_cparams(2),
    )(qs, f, iv, gs, gnorm.reshape(1, HG_DIM))


def _ep_silu(accs, extras):
    return [_silu(accs[0])]


def _ep_ident(accs, extras):
    return [accs[0]]


def _ep_forget(accs, extras):
    lbp = extras[0]
    m = jnp.max(lbp, axis=0, keepdims=True)
    e = jnp.exp(lbp - m)
    lb = e[0:1, :] / jnp.sum(e, axis=0, keepdims=True)
    return [lb + (1.0 - lb) * _sigmoid(accs[0])]


def _ep_merge(accs, extras):
    ga, a, gb, b = accs
    return [_sigmoid(ga) * a + _sigmoid(gb) * b]


def _ep_residual(accs, extras):
    return [ALPHA * extras[0] + accs[0]]


def _ep_swiglu(accs, extras):
    return [_silu(accs[0]) * accs[1]]


def _uq_weight(w_uq):
    r = w_uq.shape[0]
    w = w_uq.reshape(r, MLA_HEADS, NOPE + ROPE)
    w = jnp.concatenate([w, jnp.zeros((r, MLA_HEADS, HEAD_W - NOPE - ROPE), w.dtype)], axis=-1)
    return w.reshape(r, MLA_HEADS * HEAD_W).astype(BF16)


def _ukv_weight(w_ukv):
    r = w_ukv.shape[0]
    w = w_ukv.reshape(r, MLA_HEADS, NOPE + V_DIM)
    return jnp.concatenate([w[:, :, :NOPE].reshape(r, -1), w[:, :, NOPE:].reshape(r, -1)],
                           axis=1).astype(BF16)


def kernel(x, positions, ln_in_g, ln_in_b, w_in, q_norm_g, w_uq, kv_norm_g, w_ukv, hg_lb,
           hg_norm_g, w_branch_a, w_branch_b, w_out, ln1_g, ln1_b, w_gate, w_up, w_down,
           ln2_g, ln2_b):
    batch, seq, d = x.shape
    n = batch * seq
    l = 0
    wi = w_in[l]

    o_q, o_kv, o_kr = 0, Q_LORA, Q_LORA + KV_LORA
    o_hq = o_kr + ROPE
    o_hf, o_hi, o_hg = o_hq + HG_WIDTH, o_hq + 2 * HG_WIDTH, o_hq + 3 * HG_WIDTH
    o_ga = o_hq + 4 * HG_WIDTH
    o_gb = o_ga + D_MODEL

    w_lat = jnp.concatenate([wi[:, o_q:o_hq], jnp.zeros((d, LANES - ROPE), wi.dtype)],
                            axis=1).astype(BF16)
    w_hq = wi[:, o_hq:o_hf].astype(BF16)
    w_hf = wi[:, o_hf:o_hi].astype(BF16)
    w_hi = wi[:, o_hi:o_hg].astype(BF16)
    w_hg = wi[:, o_hg:o_ga].astype(BF16)
    w_ga = wi[:, o_ga:o_gb].astype(BF16)
    w_gb = wi[:, o_gb:].astype(BF16)

    h32, h16 = _layer_norm(x.reshape(n, d), ln_in_g, ln_in_b, (F32, BF16))
    rc, rsa, rsb = _rope_tables(positions)

    lat_w = Q_LORA + KV_LORA + LANES
    tm = 512
    rope_specs = [(rc, (tm, LANES), _row_i), (rsa, (tm, LANES), _row_i), (rsb, (tm, LANES), _row_i)]
    qn, kvn, kr = _matmul(
        [h16], [(0, w_lat)], _latent_epilogue,
        [((n, Q_LORA), BF16, (tm, Q_LORA), _row_i),
         ((n, KV_LORA), BF16, (tm, KV_LORA), _row_i),
         ((n, LANES), BF16, (tm, LANES), _row_i)],
        tm=tm, tn=lat_w,
        extras=[(q_norm_g[l].reshape(1, -1), (1, Q_LORA), lambda i, j: (0, 0)),
                (kv_norm_g[l].reshape(1, -1), (1, KV_LORA), lambda i, j: (0, 0))] + rope_specs)

    scale = (NOPE + ROPE) ** -0.5
    (qf,) = _matmul([qn], [(0, _uq_weight(w_uq[l]))], functools.partial(_uq_epilogue, scale=scale),
                    [((n, MLA_HEADS * HEAD_W), BF16, (tm, 1024), _tile_ij)],
                    tm=tm, tn=1024, extras=rope_specs)
    (kv,) = _matmul([kvn], [(0, _ukv_weight(w_ukv[l]))], _ep_ident,
                    [((n, MLA_HEADS * (NOPE + V_DIM)), BF16, (tm, 1024), _tile_ij)],
                    tm=tm, tn=1024)
    o_a = _attention(qf, kv, kr, batch, seq, MLA_HEADS)

    tm, tn = 1024, 512

    def proj(w, ep, dtype, extras=()):
        (r,) = _matmul([h16], [(0, w)], ep, [((n, w.shape[1]), dtype, (tm, tn), _tile_ij)],
                       tm=tm, tn=tn, extras=list(extras))
        return r

    qs = proj(w_hq, _ep_silu, BF16)
    fg = proj(w_hf, _ep_forget, F32, [(hg_lb, (hg_lb.shape[0], tn), _col_j)])
    iv = proj(w_hi, _ep_ident, BF16)
    gs = proj(w_hg, _ep_silu, BF16)
    o_b = _hgrn2(qs, fg, iv, gs, hg_norm_g[l], batch, seq, HG_HEADS)

    tm, tn = 512, 512
    (merged,) = _matmul(
        [h16, o_a, o_b],
        [(0, w_ga), (1, w_branch_a[l].astype(BF16)), (0, w_gb), (2, w_branch_b[l].astype(BF16))],
        _ep_merge, [((n, d), BF16, (tm, tn), _tile_ij)], tm=tm, tn=tn)

    tm, tn = 1024, 512
    (y1,) = _matmul([merged], [(0, w_out[l].astype(BF16))], _ep_residual,
                    [((n, d), F32, (tm, tn), _tile_ij)], tm=tm, tn=tn,
                    extras=[(h32, (tm, tn), _tile_ij)])
    h1_32, h1_16 = _layer_norm(y1, ln1_g[l], ln1_b[l], (F32, BF16))

    pad = D_FF_PAD - D_FF
    wg = jnp.pad(w_gate[l], ((0, 0), (0, pad))).astype(BF16)
    wu = jnp.pad(w_up[l], ((0, 0), (0, pad))).astype(BF16)
    wd = jnp.pad(w_down[l], ((0, pad), (0, 0))).astype(BF16)
    tm, tn = 1024, 512
    (act,) = _matmul([h1_16], [(0, wg), (0, wu)],
                     _ep_swiglu, [((n, D_FF_PAD), BF16, (tm, tn), _tile_ij)], tm=tm, tn=tn)
    tm, tn, tk = 1024, 1024, D_FF_PAD // 4
    (y2,) = _matmul([act], [(0, wd)], _ep_residual,
                    [((n, d), F32, (tm, tn), _tile_ij)], tm=tm, tn=tn, tk=tk,
                    extras=[(h1_32, (tm, tn), _tile_ij)])
    (out,) = _layer_norm(y2, ln2_g[l], ln2_b[l], (F32,))
    return out.reshape(batch, seq, d)
```

```python
import functools

import jax
import jax.numpy as jnp
from jax import lax
from jax.experimental import pallas as pl
from jax.experimental.pallas import tpu as pltpu

F32 = jnp.float32
BF16 = jnp.bfloat16

D_MODEL = 4096
CHUNK = 64
MLA_HEADS = 16
NOPE = 128
ROPE = 64
V_DIM = 128
Q_LORA = 1024
KV_LORA = 512
ROPE_THETA = 10000.0
HG_HEADS = 16
HG_DIM = 128
HG_WIDTH = HG_HEADS * HG_DIM
D_FF = 11008
DEPTH = 1
ALPHA = (2 * DEPTH) ** 0.25
EPS = 1e-5

LANES = 128
HALF_LANES = LANES // 2
VMEM_LIMIT = 60 * 1024 * 1024

HEAD_W = 2 * LANES
SUB = 16
HG_CHUNK = 128
ATT_TQ = 256


def _cparams(n_grid, vmem=VMEM_LIMIT):
    return pltpu.CompilerParams(dimension_semantics=("arbitrary",) * n_grid,
                                vmem_limit_bytes=vmem)


def _mm_kernel(*refs, pair_x, w_kinds, n_x, n_extra, n_out, epilogue):
    xs = refs[:n_x]
    pos = n_x
    w_refs = []
    for kind in w_kinds:
        cnt = 2 if kind == "shifted" else 1
        w_refs.append(refs[pos:pos + cnt])
        pos += cnt
    extras = refs[pos:pos + n_extra]
    outs = refs[pos + n_extra:pos + n_extra + n_out]
    scratch = list(refs[pos + n_extra + n_out:])

    w16 = []
    for kind, wr in zip(w_kinds, w_refs):
        w16.append(wr[0] if kind == "bf16" else scratch.pop(0))

    @pl.when(pl.program_id(1) == 0)
    def _():
        for kind, wr, dst in zip(w_kinds, w_refs, w16):
            if kind == "f32":
                dst[...] = wr[0][...].astype(BF16)
            elif kind == "shifted":
                tn = dst.shape[1]
                dst[:, :tn - HALF_LANES] = wr[0][:, HALF_LANES:].astype(BF16)
                dst[:, tn - HALF_LANES:] = wr[1][:, :HALF_LANES].astype(BF16)

    accs = [jnp.dot(xs[xi][...], w[...], preferred_element_type=F32)
            for xi, w in zip(pair_x, w16)]
    res = epilogue(accs, [e[...] for e in extras])
    for o, r in zip(outs, res):
        o[...] = r.astype(o.dtype)


def _matmul(name, xs, pairs, epilogue, outs, *, tm, tn, extras=()):
    m = xs[0].shape[0]
    n = pairs[0][3]
    assert m % tm == 0 and n % tn == 0
    grid = (n // tn, m // tm)

    in_specs = [pl.BlockSpec((tm, x.shape[1]), lambda j, i: (i, 0)) for x in xs]
    w_args, w_kinds, scratch = [], [], []
    for _, w, col0, ncols in pairs:
        assert ncols == n
        kdim = w.shape[0]
        if w.dtype == BF16:
            assert col0 % tn == 0
            w_kinds.append("bf16")
        elif col0 % tn == 0:
            w_kinds.append("f32")
        else:
            assert (col0 - HALF_LANES) % tn == 0
            w_kinds.append("shifted")
        if w_kinds[-1] == "shifted":
            a0 = (col0 - HALF_LANES) // tn
            b0 = (col0 - HALF_LANES + tn) // LANES
            in_specs.append(pl.BlockSpec((kdim, tn), lambda j, i, a0=a0: (0, a0 + j)))
            in_specs.append(pl.BlockSpec((kdim, LANES),
                                         lambda j, i, b0=b0: (0, b0 + j * (tn // LANES))))
            w_args += [w, w]
        else:
            c0 = col0 // tn
            in_specs.append(pl.BlockSpec((kdim, tn), lambda j, i, c0=c0: (0, c0 + j)))
            w_args.append(w)
        if w_kinds[-1] != "bf16":
            scratch.append(pltpu.VMEM((kdim, tn), BF16))
    for _, bs, im in extras:
        in_specs.append(pl.BlockSpec(bs, functools.partial(_swap_ji, im)))
    out_specs = [pl.BlockSpec(bs, functools.partial(_swap_ji, im)) for _, _, bs, im in outs]
    out_shape = [jax.ShapeDtypeStruct(s, d) for s, d, _, _ in outs]

    kern = functools.partial(_mm_kernel, pair_x=tuple(p[0] for p in pairs),
                             w_kinds=tuple(w_kinds), n_x=len(xs), n_extra=len(extras),
                             n_out=len(outs), epilogue=epilogue)
    return pl.pallas_call(
        kern, grid=grid, in_specs=in_specs, out_specs=out_specs, out_shape=out_shape,
        scratch_shapes=scratch, compiler_params=_cparams(2), name=name,
    )(*xs, *w_args, *[e[0] for e in extras])


def _swap_ji(im, j, i):
    return im(i, j)


def _tile_ij(i, j):
    return (i, j)


def _row_i(i, j):
    return (i, 0)


def _col_j(i, j):
    return (0, j)


def _sigmoid(x):
    return 1.0 / (1.0 + jnp.exp(-x))


def _silu(x):
    return x * _sigmoid(x)


def _ln_kernel(y_ref, g_ref, b_ref, *outs):
    y = y_ref[...]
    mu = jnp.mean(y, axis=-1, keepdims=True)
    d = y - mu
    var = jnp.mean(d * d, axis=-1, keepdims=True)
    r = d * lax.rsqrt(var + EPS) * g_ref[...] + b_ref[...]
    for o in outs:
        o[...] = r.astype(o.dtype)


def _layer_norm(name, y, g, b, out_dtypes, tm=256):
    m, d = y.shape
    return pl.pallas_call(
        _ln_kernel, grid=(m // tm,), name=name,
        in_specs=[pl.BlockSpec((tm, d), lambda i: (i, 0)),
                  pl.BlockSpec((1, d), lambda i: (0, 0)),
                  pl.BlockSpec((1, d), lambda i: (0, 0))],
        out_specs=[pl.BlockSpec((tm, d), lambda i: (i, 0)) for _ in out_dtypes],
        out_shape=[jax.ShapeDtypeStruct((m, d), dt) for dt in out_dtypes],
        compiler_params=_cparams(1),
    )(y, g.reshape(1, d), b.reshape(1, d))


def _rope_table_kernel(pos_ref, invf_ref, c_ref, sa_ref, sb_ref):
    ang = pos_ref[...].astype(F32) * invf_ref[...]
    lane = lax.broadcasted_iota(jnp.int32, ang.shape, 1)
    cos = jnp.cos(ang)
    sin = jnp.sin(ang)
    half = ROPE // 2
    c_ref[...] = cos
    sa_ref[...] = jnp.where(lane < half, -sin, 0.0)
    sb_ref[...] = jnp.where((lane >= half) & (lane < ROPE), sin, 0.0)


def _rope_tables(positions, tm=1024):
    n = positions.size
    half = ROPE // 2
    inv_freq = ROPE_THETA ** (-jnp.arange(half, dtype=F32) / half)
    invf = jnp.concatenate([inv_freq, inv_freq, jnp.zeros((LANES - ROPE,), F32)]).reshape(1, LANES)
    pos = positions.reshape(n, 1)
    return pl.pallas_call(
        _rope_table_kernel, grid=(n // tm,), name="rope_tables",
        in_specs=[pl.BlockSpec((tm, 1), lambda i: (i, 0)),
                  pl.BlockSpec((1, LANES), lambda i: (0, 0))],
        out_specs=[pl.BlockSpec((tm, LANES), lambda i: (i, 0))] * 3,
        out_shape=[jax.ShapeDtypeStruct((n, LANES), F32)] * 3,
        compiler_params=_cparams(1),
    )(pos, invf)


def _rope_lanes(x, c, sa, sb):
    half = ROPE // 2
    return (x * c + pltpu.roll(x, LANES - half, axis=1) * sa
            + pltpu.roll(x, half, axis=1) * sb)


def _rms(x, g):
    ms = jnp.mean(x * x, axis=-1, keepdims=True)
    return x * lax.rsqrt(ms + EPS) * g


def _latent_epilogue(accs, extras):
    acc = accs[0]
    gq, gkv, c, sa, sb = extras
    qn = _rms(acc[:, :Q_LORA], gq)
    kvn = _rms(acc[:, Q_LORA:Q_LORA + KV_LORA], gkv)
    kr = _rope_lanes(acc[:, Q_LORA + KV_LORA:], c, sa, sb)
    return [qn, kvn, kr]


def _uq_epilogue(accs, extras, *, scale):
    acc = accs[0]
    c, sa, sb = extras
    tn = acc.shape[1]
    cols = []
    for h in range(tn // HEAD_W):
        base = h * HEAD_W
        cols.append(acc[:, base:base + NOPE] * scale)
        cols.append(_rope_lanes(acc[:, base + NOPE:base + HEAD_W], c, sa, sb) * scale)
    return [jnp.concatenate(cols, axis=1)]


def _attn_kernel(q_ref, kn_ref, v_ref, kr_ref, o_ref, kcat_ref, *, tq):
    t = q_ref.shape[0]
    kcat_ref[:, :NOPE] = kn_ref[...]
    kcat_ref[:, NOPE:] = kr_ref[...]
    row_chunk = lax.broadcasted_iota(jnp.int32, (tq, tq), 0) // CHUNK
    col_chunk = lax.broadcasted_iota(jnp.int32, (tq, tq), 1) // CHUNK
    visible = col_chunk <= row_chunk
    nt = (((1,), (1,)), ((), ()))
    for jq in range(t // tq):
        s0, s1 = jq * tq, (jq + 1) * tq
        q = q_ref[s0:s1, :]
        sd = lax.dot_general(q, kcat_ref[s0:s1, :], nt, preferred_element_type=F32)
        sd = jnp.where(visible, sd, -jnp.inf)
        m = jnp.max(sd, axis=-1, keepdims=True)
        if jq > 0:
            sp = lax.dot_general(q, kcat_ref[0:s0, :], nt, preferred_element_type=F32)
            m = jnp.maximum(m, jnp.max(sp, axis=-1, keepdims=True))
        pd = jnp.exp(sd - m)
        l = jnp.sum(pd, axis=-1, keepdims=True)
        o = jnp.dot(pd.astype(BF16), v_ref[s0:s1, :], preferred_element_type=F32)
        if jq > 0:
            pp = jnp.exp(sp - m)
            l = l + jnp.sum(pp, axis=-1, keepdims=True)
            o = o + jnp.dot(pp.astype(BF16), v_ref[0:s0, :], preferred_element_type=F32)
        o_ref[s0:s1, :] = (o / l).astype(o_ref.dtype)


def _attention(qf, kv, kr, batch, seq, heads):
    n = batch * seq
    kern = functools.partial(_attn_kernel, tq=ATT_TQ)
    return pl.pallas_call(
        kern, grid=(batch, heads), name="mla_attention",
        in_specs=[pl.BlockSpec((seq, HEAD_W), lambda b, h: (b, h)),
                  pl.BlockSpec((seq, NOPE), lambda b, h: (b, h)),
                  pl.BlockSpec((seq, V_DIM), lambda b, h: (b, heads + h)),
                  pl.BlockSpec((seq, LANES), lambda b, h: (b, 0))],
        out_specs=pl.BlockSpec((seq, V_DIM), lambda b, h: (b, h)),
        out_shape=jax.ShapeDtypeStruct((n, heads * V_DIM), BF16),
        scratch_shapes=[pltpu.VMEM((seq, HEAD_W), BF16)],
        compiler_params=_cparams(2),
    )(qf, kv, kv, kr)


def _split3(x):
    a = x.astype(BF16)
    r = x - a.astype(F32)
    b = r.astype(BF16)
    c = (r - b.astype(F32)).astype(BF16)
    return a, b, c


def _bcast_rows(x, rows, reps):
    return jnp.concatenate(
        [jnp.broadcast_to(x[r:r + 1, :], (reps, x.shape[1])) for r in rows], axis=0)


def _hgrn_kernel(q_ref, f_ref, i_ref, gs_ref, gn_ref, o_ref, st_ref):
    t = q_ref.shape[0]
    c = HG_CHUNK
    d = HG_DIM
    n_sub = c // SUB
    row = lax.broadcasted_iota(jnp.int32, (c, d), 0)
    col = lax.broadcasted_iota(jnp.int32, (c, d), 1)
    tri = (col <= row).astype(BF16)
    ones = jnp.ones((d, d), BF16)
    row_in_sub = row % SUB
    diag_lane = col - (row // SUB) * SUB
    nt = (((1,), (1,)), ((), ()))

    st_ref[...] = jnp.zeros_like(st_ref)

    def chunk(ci, carry):
        r0 = pl.multiple_of(ci * c, c)
        q = q_ref[pl.ds(r0, c), :].astype(F32)
        f = f_ref[pl.ds(r0, c), :]
        v = i_ref[pl.ds(r0, c), :]
        g = jnp.log(f)
        k = 1.0 - f
        g1, g2, g3 = _split3(g)
        b = (jnp.dot(tri, g1, preferred_element_type=F32)
             + jnp.dot(tri, g2, preferred_element_type=F32)
             + jnp.dot(tri, g3, preferred_element_type=F32))
        b_last = b[c - 1:c, :]

        p = jnp.zeros((c, c), F32)
        for s in range(SUB):
            rows = [i * SUB + s for i in range(n_sub)]
            ks = _bcast_rows(k, rows, SUB)
            bs = _bcast_rows(b, rows, SUB)
            e = jnp.exp(jnp.where(row_in_sub >= s, b - bs, -jnp.inf))
            a = (q * ks * e).astype(BF16)
            rsum = jnp.dot(a, ones, preferred_element_type=F32)
            p = p + jnp.where(diag_lane == s, rsum, 0.0)

        hs = c // 2
        while hs >= SUB:
            blk = 2 * hs
            rows = [(i // blk) * blk + hs - 1 for i in range(0, c, hs)]
            bb = _bcast_rows(b, rows, hs)
            upper = (row % blk) >= hs
            e = jnp.exp(jnp.where(upper, b - bb, bb - b))
            qt = jnp.where(upper, q * e, 0.0).astype(BF16)
            kt = jnp.where(upper, 0.0, k * e).astype(BF16)
            sc = lax.dot_general(qt, kt, nt, preferred_element_type=F32)
            p = p + jnp.where((row // blk) == (col // blk), sc, 0.0)
            hs //= 2

        o = jnp.dot(p.astype(BF16), v, preferred_element_type=F32)

        st = st_ref[...]
        q0 = (q * jnp.exp(b)).astype(BF16)
        o = o + lax.dot_general(q0, st.astype(BF16), nt, preferred_element_type=F32)
        kl = (k * jnp.exp(b_last - b)).astype(BF16)
        vt = v.astype(F32).T.astype(BF16)
        st_ref[...] = jnp.exp(b_last) * st + jnp.dot(vt, kl, preferred_element_type=F32)

        ms = jnp.mean(o * o, axis=-1, keepdims=True)
        ob = o * lax.rsqrt(ms + EPS) * gn_ref[...] * gs_ref[pl.ds(r0, c), :].astype(F32)
        o_ref[pl.ds(r0, c), :] = ob.astype(o_ref.dtype)
        return carry

    lax.fori_loop(0, t // c, chunk, 0)


def _hgrn2(qs, f, iv, gs, gnorm, batch, seq, heads):
    n = batch * seq
    spec = pl.BlockSpec((seq, HG_DIM), lambda b, h: (b, h))
    return pl.pallas_call(
        _hgrn_kernel, grid=(batch, heads), name="hgrn2",
        in_specs=[spec, spec, spec, spec, pl.BlockSpec((1, HG_DIM), lambda b, h: (0, 0))],
        out_specs=spec,
        out_shape=jax.ShapeDtypeStruct((n, heads * HG_DIM), BF16),
        scratch_shapes=[pltpu.VMEM((HG_DIM, HG_DIM), F32)],
        compiler_params=_cparams(2),
    )(qs, f, iv, gs, gnorm.reshape(1, HG_DIM))


def _ep_silu(accs, extras):
    return [_silu(accs[0])]


def _ep_ident(accs, extras):
    return [accs[0]]


def _ep_forget(accs, extras):
    lbp = extras[0]
    m = jnp.max(lbp, axis=0, keepdims=True)
    e = jnp.exp(lbp - m)
    lb = e[0:1, :] / jnp.sum(e, axis=0, keepdims=True)
    return [lb + (1.0 - lb) * _sigmoid(accs[0])]


def _ep_merge(accs, extras):
    ga, a, gb, b = accs
    return [_sigmoid(ga) * a + _sigmoid(gb) * b]


def _ep_residual(accs, extras):
    return [ALPHA * extras[0] + accs[0]]


def _ep_swiglu(accs, extras):
    return [_silu(accs[0]) * accs[1]]


def _uq_weight(w_uq):
    r = w_uq.shape[0]
    w = w_uq.reshape(r, MLA_HEADS, NOPE + ROPE)
    w = jnp.concatenate([w, jnp.zeros((r, MLA_HEADS, HEAD_W - NOPE - ROPE), w.dtype)], axis=-1)
    return w.reshape(r, MLA_HEADS * HEAD_W).astype(BF16)


def _ukv_weight(w_ukv):
    r = w_ukv.shape[0]
    w = w_ukv.reshape(r, MLA_HEADS, NOPE + V_DIM)
    return jnp.concatenate([w[:, :, :NOPE].reshape(r, -1), w[:, :, NOPE:].reshape(r, -1)],
                           axis=1).astype(BF16)


def kernel(x, positions, ln_in_g, ln_in_b, w_in, q_norm_g, w_uq, kv_norm_g, w_ukv, hg_lb,
           hg_norm_g, w_branch_a, w_branch_b, w_out, ln1_g, ln1_b, w_gate, w_up, w_down,
           ln2_g, ln2_b):
    batch, seq, d = x.shape
    n = batch * seq
    assert w_in.shape[0] == DEPTH == 1
    l = 0
    wi = w_in.reshape(d, w_in.shape[-1])

    o_kr = Q_LORA + KV_LORA
    o_hq = o_kr + ROPE
    o_hf, o_hi, o_hg = o_hq + HG_WIDTH, o_hq + 2 * HG_WIDTH, o_hq + 3 * HG_WIDTH
    o_ga = o_hq + 4 * HG_WIDTH
    o_gb = o_ga + D_MODEL

    h32, h16 = _layer_norm("ln_in", x.reshape(n, d), ln_in_g, ln_in_b, (F32, BF16))
    rc, rsa, rsb = _rope_tables(positions)

    lat_w = o_kr + LANES
    w_lat = jnp.concatenate([wi[:, :o_hq], jnp.zeros((d, LANES - ROPE), wi.dtype)],
                            axis=1).astype(BF16)
    tm = 512
    rope_specs = [(rc, (tm, LANES), _row_i), (rsa, (tm, LANES), _row_i), (rsb, (tm, LANES), _row_i)]
    qn, kvn, kr = _matmul(
        "latent_proj", [h16], [(0, w_lat, 0, lat_w)], _latent_epilogue,
        [((n, Q_LORA), BF16, (tm, Q_LORA), _row_i),
         ((n, KV_LORA), BF16, (tm, KV_LORA), _row_i),
         ((n, LANES), BF16, (tm, LANES), _row_i)],
        tm=tm, tn=lat_w,
        extras=[(q_norm_g[l].reshape(1, -1), (1, Q_LORA), lambda i, j: (0, 0)),
                (kv_norm_g[l].reshape(1, -1), (1, KV_LORA), lambda i, j: (0, 0))] + rope_specs)

    scale = (NOPE + ROPE) ** -0.5
    q_w = MLA_HEADS * HEAD_W
    (qf,) = _matmul("q_up_proj", [qn], [(0, _uq_weight(w_uq[l]), 0, q_w)],
                    functools.partial(_uq_epilogue, scale=scale),
                    [((n, q_w), BF16, (tm, 1024), _tile_ij)],
                    tm=tm, tn=1024, extras=rope_specs)
    kv_w = MLA_HEADS * (NOPE + V_DIM)
    (kv,) = _matmul("kv_up_proj", [kvn], [(0, _ukv_weight(w_ukv[l]), 0, kv_w)], _ep_ident,
                    [((n, kv_w), BF16, (tm, 1024), _tile_ij)], tm=tm, tn=1024)
    o_a = _attention(qf, kv, kr, batch, seq, MLA_HEADS)

    tm, tn = 1024, 512

    def proj(name, col0, ep, dtype, extras=()):
        (r,) = _matmul(name, [h16], [(0, wi, col0, HG_WIDTH)], ep,
                       [((n, HG_WIDTH), dtype, (tm, tn), _tile_ij)],
                       tm=tm, tn=tn, extras=list(extras))
        return r

    qs = proj("hg_q_proj", o_hq, _ep_silu, BF16)
    fg = proj("hg_f_proj", o_hf, _ep_forget, F32, [(hg_lb, (hg_lb.shape[0], tn), _col_j)])
    iv = proj("hg_i_proj", o_hi, _ep_ident, BF16)
    gs = proj("hg_g_proj", o_hg, _ep_silu, BF16)
    o_b = _hgrn2(qs, fg, iv, gs, hg_norm_g[l], batch, seq, HG_HEADS)

    tm, tn = 512, 256
    (merged,) = _matmul(
        "gated_merge", [h16, o_a, o_b],
        [(0, wi, o_ga, d), (1, w_branch_a[l], 0, d), (0, wi, o_gb, d), (2, w_branch_b[l], 0, d)],
        _ep_merge, [((n, d), BF16, (tm, tn), _tile_ij)], tm=tm, tn=tn)

    tm, tn = 1024, 512
    (y1,) = _matmul("out_proj", [merged], [(0, w_out[l], 0, d)], _ep_residual,
                    [((n, d), F32, (tm, tn), _tile_ij)], tm=tm, tn=tn,
                    extras=[(h32, (tm, tn), _tile_ij)])
    h1_32, h1_16 = _layer_norm("ln1", y1, ln1_g[l], ln1_b[l], (F32, BF16))

    tm, tn = 1024, 256
    (act,) = _matmul("swiglu_up", [h1_16], [(0, w_gate[l], 0, D_FF), (0, w_up[l], 0, D_FF)],
                     _ep_swiglu, [((n, D_FF), BF16, (tm, tn), _tile_ij)], tm=tm, tn=tn)
    tm, tn = 512, 256
    (y2,) = _matmul("swiglu_down", [act], [(0, w_down[l], 0, d)], _ep_residual,
                    [((n, d), F32, (tm, tn), _tile_ij)], tm=tm, tn=tn,
                    extras=[(h1_32, (tm, tn), _tile_ij)])
    (out,) = _layer_norm("ln2", y2, ln2_g[l], ln2_b[l], (F32,))
    return out.reshape(batch, seq, d)
```

```python
import functools

import jax
import jax.numpy as jnp
from jax import lax
from jax.experimental import pallas as pl
from jax.experimental.pallas import tpu as pltpu

F32 = jnp.float32
BF16 = jnp.bfloat16

D_MODEL = 4096
CHUNK = 64
MLA_HEADS = 16
NOPE = 128
ROPE = 64
V_DIM = 128
Q_LORA = 1024
KV_LORA = 512
ROPE_THETA = 10000.0
HG_HEADS = 16
HG_DIM = 128
HG_WIDTH = HG_HEADS * HG_DIM
D_FF = 11008
DEPTH = 1
ALPHA = (2 * DEPTH) ** 0.25
EPS = 1e-5
LOG2_E = 1.4426950408889634

LANES = 128
VMEM_LIMIT = 60 * 1024 * 1024

HEAD_W = 2 * LANES
SUB = 16
HG_CHUNK = 128
ATT_TQ = 256


def _cparams(n_grid, vmem=VMEM_LIMIT):
    return pltpu.CompilerParams(dimension_semantics=("arbitrary",) * n_grid,
                                vmem_limit_bytes=vmem)


def _mm_kernel(*refs, pair_x, w_kinds, n_x, n_extra, n_out, epilogue):
    n_pairs = len(pair_x)
    xs = refs[:n_x]
    w_refs = refs[n_x:n_x + n_pairs]
    extras = refs[n_x + n_pairs:n_x + n_pairs + n_extra]
    outs = refs[n_x + n_pairs + n_extra:n_x + n_pairs + n_extra + n_out]
    scratch = list(refs[n_x + n_pairs + n_extra + n_out:])

    w16 = [wr if wr.dtype == BF16 else scratch.pop(0) for wr in w_refs]

    @pl.when(pl.program_id(1) == 0)
    def _():
        for wr, dst in zip(w_refs, w16):
            if wr.dtype != BF16:
                dst[...] = wr[...].astype(BF16)

    accs = []
    for xi, w, kind in zip(pair_x, w16, w_kinds):
        if kind == "kn":
            accs.append(jnp.dot(xs[xi][...], w[...], preferred_element_type=F32))
        else:
            accs.append(lax.dot_general(xs[xi][...], w[...], (((1,), (1,)), ((), ())),
                                        preferred_element_type=F32))
    res = epilogue(accs, [e[...] for e in extras])
    for o, r in zip(outs, res):
        o[...] = r.astype(o.dtype)


def _matmul(name, xs, pairs, epilogue, outs, *, tm, tn, extras=()):
    m = xs[0].shape[0]
    n = pairs[0][4]
    assert m % tm == 0 and n % tn == 0
    grid = (n // tn, m // tm)

    in_specs = [pl.BlockSpec((tm, x.shape[1]), lambda j, i: (i, 0)) for x in xs]
    scratch = []
    for _, w, kind, col0, ncols in pairs:
        assert ncols == n
        if kind == "kn":
            assert col0 % tn == 0
            shape = (w.shape[0], tn)
            in_specs.append(pl.BlockSpec(shape, lambda j, i, c0=col0 // tn: (0, c0 + j)))
        else:
            shape = (tn, w.shape[1])
            if col0 % tn == 0:
                in_specs.append(pl.BlockSpec(shape, lambda j, i, c0=col0 // tn: (c0 + j, 0)))
            else:
                in_specs.append(pl.BlockSpec((pl.Element(tn), pl.Element(w.shape[1])),
                                             lambda j, i, c0=col0: (_row_start(c0, j, tn), 0)))
        if w.dtype != BF16:
            scratch.append(pltpu.VMEM(shape, BF16))
    for _, bs, im in extras:
        in_specs.append(pl.BlockSpec(bs, functools.partial(_swap_ji, im)))
    out_specs = [pl.BlockSpec(bs, functools.partial(_swap_ji, im)) for _, _, bs, im in outs]
    out_shape = [jax.ShapeDtypeStruct(s, d) for s, d, _, _ in outs]

    kern = functools.partial(_mm_kernel, pair_x=tuple(p[0] for p in pairs),
                             w_kinds=tuple(p[2] for p in pairs), n_x=len(xs),
                             n_extra=len(extras), n_out=len(outs), epilogue=epilogue)
    return pl.pallas_call(
        kern, grid=grid, in_specs=in_specs, out_specs=out_specs, out_shape=out_shape,
        scratch_shapes=scratch, compiler_params=_cparams(2), name=name,
    )(*xs, *[p[1] for p in pairs], *[e[0] for e in extras])


SUBLANES = 8


def _row_start(row0, j, tn):
    assert row0 % SUBLANES == 0 and tn % SUBLANES == 0
    return pl.multiple_of(row0 + j * tn, SUBLANES)


def _swap_ji(im, j, i):
    return im(i, j)


def _tile_ij(i, j):
    return (i, j)


def _row_i(i, j):
    return (i, 0)


def _col_j(i, j):
    return (0, j)


def _sigmoid(x):
    return 1.0 / (1.0 + jnp.exp(-x))


def _silu(x):
    return x * _sigmoid(x)


def _ln_kernel(y_ref, g_ref, b_ref, *outs):
    y = y_ref[...]
    mu = jnp.mean(y, axis=-1, keepdims=True)
    d = y - mu
    var = jnp.mean(d * d, axis=-1, keepdims=True)
    r = d * lax.rsqrt(var + EPS) * g_ref[...] + b_ref[...]
    for o in outs:
        o[...] = r.astype(o.dtype)


def _layer_norm(name, y, g, b, out_dtypes, tm=256):
    m, d = y.shape
    return pl.pallas_call(
        _ln_kernel, grid=(m // tm,), name=name,
        in_specs=[pl.BlockSpec((tm, d), lambda i: (i, 0)),
                  pl.BlockSpec((1, d), lambda i: (0, 0)),
                  pl.BlockSpec((1, d), lambda i: (0, 0))],
        out_specs=[pl.BlockSpec((tm, d), lambda i: (i, 0)) for _ in out_dtypes],
        out_shape=[jax.ShapeDtypeStruct((m, d), dt) for dt in out_dtypes],
        compiler_params=_cparams(1),
    )(y, g.reshape(1, d), b.reshape(1, d))


def _rope_table_kernel(pos_ref, invf_ref, c_ref, sa_ref, sb_ref):
    ang = pos_ref[...].astype(F32) * invf_ref[...]
    lane = lax.broadcasted_iota(jnp.int32, ang.shape, 1)
    cos = jnp.cos(ang)
    sin = jnp.sin(ang)
    half = ROPE // 2
    c_ref[...] = cos
    sa_ref[...] = jnp.where(lane < half, -sin, 0.0)
    sb_ref[...] = jnp.where((lane >= half) & (lane < ROPE), sin, 0.0)


def _rope_tables(positions, tm=1024):
    n = positions.size
    half = ROPE // 2
    inv_freq = ROPE_THETA ** (-jnp.arange(half, dtype=F32) / half)
    invf = jnp.concatenate([inv_freq, inv_freq, jnp.zeros((LANES - ROPE,), F32)]).reshape(1, LANES)
    pos = positions.reshape(n, 1)
    return pl.pallas_call(
        _rope_table_kernel, grid=(n // tm,), name="rope_tables",
        in_specs=[pl.BlockSpec((tm, 1), lambda i: (i, 0)),
                  pl.BlockSpec((1, LANES), lambda i: (0, 0))],
        out_specs=[pl.BlockSpec((tm, LANES), lambda i: (i, 0))] * 3,
        out_shape=[jax.ShapeDtypeStruct((n, LANES), F32)] * 3,
        compiler_params=_cparams(1),
    )(pos, invf)


def _rope_lanes(x, c, sa, sb):
    half = ROPE // 2
    return (x * c + pltpu.roll(x, LANES - half, axis=1) * sa
            + pltpu.roll(x, half, axis=1) * sb)


def _rms(x, g):
    ms = jnp.mean(x * x, axis=-1, keepdims=True)
    return x * lax.rsqrt(ms + EPS) * g


def _latent_epilogue(accs, extras):
    acc = accs[0]
    gq, gkv, c, sa, sb = extras
    qn = _rms(acc[:, :Q_LORA], gq)
    kvn = _rms(acc[:, Q_LORA:Q_LORA + KV_LORA], gkv)
    kr = _rope_lanes(acc[:, Q_LORA + KV_LORA:], c, sa, sb)
    return [qn, kvn, kr]


def _uq_epilogue(accs, extras, *, scale):
    acc = accs[0]
    c, sa, sb = extras
    tn = acc.shape[1]
    cols = []
    for h in range(tn // HEAD_W):
        base = h * HEAD_W
        cols.append(acc[:, base:base + NOPE] * scale)
        cols.append(_rope_lanes(acc[:, base + NOPE:base + HEAD_W], c, sa, sb) * scale)
    return [jnp.concatenate(cols, axis=1)]


def _attn_kernel(q_ref, kn_ref, v_ref, kr_ref, o_ref, kcat_ref, *, tq):
    t = q_ref.shape[0]
    kcat_ref[:, :NOPE] = kn_ref[...]
    kcat_ref[:, NOPE:] = kr_ref[...]
    row_chunk = lax.broadcasted_iota(jnp.int32, (tq, tq), 0) // CHUNK
    col_chunk = lax.broadcasted_iota(jnp.int32, (tq, tq), 1) // CHUNK
    visible = col_chunk <= row_chunk
    nt = (((1,), (1,)), ((), ()))
    for jq in range(t // tq):
        s0, s1 = jq * tq, (jq + 1) * tq
        q = q_ref[s0:s1, :]
        sd = lax.dot_general(q, kcat_ref[s0:s1, :], nt, preferred_element_type=F32)
        sd = jnp.where(visible, sd, -jnp.inf)
        m = jnp.max(sd, axis=-1, keepdims=True)
        if jq > 0:
            sp = lax.dot_general(q, kcat_ref[0:s0, :], nt, preferred_element_type=F32)
            m = jnp.maximum(m, jnp.max(sp, axis=-1, keepdims=True))
        pd = jnp.exp(sd - m)
        l = jnp.sum(pd, axis=-1, keepdims=True)
        o = jnp.dot(pd.astype(BF16), v_ref[s0:s1, :], preferred_element_type=F32)
        if jq > 0:
            pp = jnp.exp(sp - m)
            l = l + jnp.sum(pp, axis=-1, keepdims=True)
            o = o + jnp.dot(pp.astype(BF16), v_ref[0:s0, :], preferred_element_type=F32)
        o_ref[s0:s1, :] = (o / l).astype(o_ref.dtype)


def _attention(qf, kv, kr, batch, seq, heads):
    n = batch * seq
    kern = functools.partial(_attn_kernel, tq=ATT_TQ)
    return pl.pallas_call(
        kern, grid=(batch, heads), name="mla_attention",
        in_specs=[pl.BlockSpec((seq, HEAD_W), lambda b, h: (b, h)),
                  pl.BlockSpec((seq, NOPE), lambda b, h: (b, h)),
                  pl.BlockSpec((seq, V_DIM), lambda b, h: (b, heads + h)),
                  pl.BlockSpec((seq, LANES), lambda b, h: (b, 0))],
        out_specs=pl.BlockSpec((seq, V_DIM), lambda b, h: (b, h)),
        out_shape=jax.ShapeDtypeStruct((n, heads * V_DIM), BF16),
        scratch_shapes=[pltpu.VMEM((seq, HEAD_W), BF16)],
        compiler_params=_cparams(2),
    )(qf, kv, kv, kr)


def _split3(x):
    a = x.astype(BF16)
    r = x - a.astype(F32)
    b = r.astype(BF16)
    c = (r - b.astype(F32)).astype(BF16)
    return a, b, c


def _hgrn_head_chunk(hh, r0, q_ref, f_ref, i_ref, gs_ref, gn_ref, sel_ref, o_ref, st_ref,
                     a_ref, masks):
    c, d, n_sub = HG_CHUNK, HG_DIM, HG_CHUNK // SUB
    tri, m_diag, m_l16, m_l32 = masks
    lanes = slice(hh * d, (hh + 1) * d)
    rows = pl.ds(r0, c)
    nt = (((1,), (1,)), ((), ()))

    q = q_ref[rows, lanes].astype(F32)
    f = f_ref[rows, lanes]
    v = i_ref[rows, lanes]
    g = jnp.log(f) * LOG2_E
    k = 1.0 - f
    cs = jnp.dot(tri, jnp.concatenate(_split3(g), axis=1), preferred_element_type=F32)
    b = cs[:, :d] + cs[:, d:2 * d] + cs[:, 2 * d:]
    b_last = b[c - 1:c, :]

    q3 = q.reshape(n_sub, SUB, d)
    b3 = b.reshape(n_sub, SUB, d)
    c3 = (b - jnp.log(k) * LOG2_E).reshape(n_sub, SUB, d)
    half = SUB // 2
    for s in range(SUB):
        lo = 0 if s < half else half
        a = q3[:, lo:, :] * jnp.exp2(jnp.minimum(b3[:, lo:, :] - c3[:, s:s + 1, :], 0.0))
        if lo:
            a = jnp.concatenate([jnp.zeros((n_sub, lo, d), F32), a], axis=1)
        a_ref[hh, :, s * d:(s + 1) * d] = a.reshape(c, d).astype(BF16)
    p_diag = jnp.dot(a_ref[hh], sel_ref[...], preferred_element_type=F32)

    def level(hs):
        blk = 2 * hs
        qt, kt = [], []
        for i in range(0, c, hs):
            edge = (i // blk) * blk + hs - 1
            e = jnp.exp2(-jnp.abs(b[i:i + hs, :] - b[edge:edge + 1, :]))
            zero = jnp.zeros((hs, d), F32)
            upper = (i % blk) >= hs
            qt.append(q[i:i + hs, :] * e if upper else zero)
            kt.append(zero if upper else k[i:i + hs, :] * e)
        qt = jnp.concatenate(qt, axis=0).astype(BF16)
        kt = jnp.concatenate(kt, axis=0).astype(BF16)
        return lax.dot_general(qt, kt, nt, preferred_element_type=F32)

    p = jnp.where(m_diag, p_diag,
                  jnp.where(m_l16, level(SUB), jnp.where(m_l32, level(2 * SUB), level(4 * SUB))))
    o = jnp.dot(p.astype(BF16), v, preferred_element_type=F32)

    st = st_ref[hh]
    q0 = (q * jnp.exp2(b)).astype(BF16)
    o = o + lax.dot_general(q0, st.astype(BF16), nt, preferred_element_type=F32)
    kl = (k * jnp.exp2(b_last - b)).astype(BF16)
    vt = v.astype(F32).T.astype(BF16)
    st_ref[hh] = jnp.exp2(b_last) * st + jnp.dot(vt, kl, preferred_element_type=F32)

    ms = jnp.mean(o * o, axis=-1, keepdims=True)
    ob = o * lax.rsqrt(ms + EPS) * gn_ref[...] * gs_ref[rows, lanes].astype(F32)
    o_ref[rows, lanes] = ob.astype(o_ref.dtype)


def _hgrn_kernel(q_ref, f_ref, i_ref, gs_ref, gn_ref, sel_ref, o_ref, st_ref, a_ref, *, hpb):
    t = q_ref.shape[0]
    c, d = HG_CHUNK, HG_DIM
    row = lax.broadcasted_iota(jnp.int32, (c, d), 0)
    col = lax.broadcasted_iota(jnp.int32, (c, d), 1)
    tri = (col <= row).astype(BF16)
    m_diag = ((row // SUB) == (col // SUB)) & ((col % SUB) <= (row % SUB))
    m_l16 = (row // (2 * SUB)) == (col // (2 * SUB))
    m_l32 = (row // (4 * SUB)) == (col // (4 * SUB))
    masks = (tri, m_diag, m_l16, m_l32)

    st_ref[...] = jnp.zeros_like(st_ref)

    def chunk(ci, carry):
        r0 = pl.multiple_of(ci * c, c)
        for hh in range(hpb):
            _hgrn_head_chunk(hh, r0, q_ref, f_ref, i_ref, gs_ref, gn_ref, sel_ref, o_ref,
                             st_ref, a_ref, masks)
        return carry

    lax.fori_loop(0, t // c, chunk, 0)


def _hgrn2(qs, f, iv, gs, gnorm, batch, seq, heads, hpb=2):
    n = batch * seq
    d = HG_DIM
    sel = (jnp.arange(SUB * d)[:, None] // d == jnp.arange(d)[None, :] % SUB).astype(BF16)
    spec = pl.BlockSpec((seq, hpb * d), lambda b, h: (b, h))
    return pl.pallas_call(
        functools.partial(_hgrn_kernel, hpb=hpb), grid=(batch, heads // hpb), name="hgrn2",
        in_specs=[spec, spec, spec, spec, pl.BlockSpec((1, d), lambda b, h: (0, 0)),
                  pl.BlockSpec((SUB * d, d), lambda b, h: (0, 0))],
        out_specs=spec,
        out_shape=jax.ShapeDtypeStruct((n, heads * d), BF16),
        scratch_shapes=[pltpu.VMEM((hpb, d, d), F32),
                        pltpu.VMEM((hpb, HG_CHUNK, SUB * d), BF16)],
        compiler_params=_cparams(2),
    )(qs, f, iv, gs, gnorm.reshape(1, d), sel)


def _ep_silu(accs, extras):
    return [_silu(accs[0])]


def _ep_ident(accs, extras):
    return [accs[0]]


def _ep_forget(accs, extras):
    lbp = extras[0]
    m = jnp.max(lbp, axis=0, keepdims=True)
    e = jnp.exp(lbp - m)
    lb = e[0:1, :] / jnp.sum(e, axis=0, keepdims=True)
    return [lb + (1.0 - lb) * _sigmoid(accs[0])]


def _ep_merge(accs, extras):
    ga, a, gb, b = accs
    return [_sigmoid(ga) * a + _sigmoid(gb) * b]


def _ep_residual(accs, extras):
    return [ALPHA * extras[0] + accs[0]]


def _ep_swiglu(accs, extras):
    return [_silu(accs[0]) * accs[1]]


def _uq_weight(w_uq):
    r = w_uq.shape[0]
    w = w_uq.reshape(r, MLA_HEADS, NOPE + ROPE)
    w = jnp.concatenate([w, jnp.zeros((r, MLA_HEADS, HEAD_W - NOPE - ROPE), w.dtype)], axis=-1)
    return w.reshape(r, MLA_HEADS * HEAD_W).astype(BF16)


def _ukv_weight(w_ukv):
    r = w_ukv.shape[0]
    w = w_ukv.reshape(r, MLA_HEADS, NOPE + V_DIM)
    return jnp.concatenate([w[:, :, :NOPE].reshape(r, -1), w[:, :, NOPE:].reshape(r, -1)],
                           axis=1).astype(BF16)


def kernel(x, positions, ln_in_g, ln_in_b, w_in, q_norm_g, w_uq, kv_norm_g, w_ukv, hg_lb,
           hg_norm_g, w_branch_a, w_branch_b, w_out, ln1_g, ln1_b, w_gate, w_up, w_down,
           ln2_g, ln2_b):
    batch, seq, d = x.shape
    n = batch * seq
    assert w_in.shape[0] == DEPTH == 1
    l = 0
    wt = w_in.reshape(d, w_in.shape[-1]).T

    o_kr = Q_LORA + KV_LORA
    o_hq = o_kr + ROPE
    o_hf, o_hi, o_hg = o_hq + HG_WIDTH, o_hq + 2 * HG_WIDTH, o_hq + 3 * HG_WIDTH
    o_ga = o_hq + 4 * HG_WIDTH
    o_gb = o_ga + D_MODEL

    h32, h16 = _layer_norm("ln_in", x.reshape(n, d), ln_in_g, ln_in_b, (F32, BF16))
    rc, rsa, rsb = _rope_tables(positions)

    lat_w = o_kr + LANES
    w_lat = jnp.concatenate([wt[:o_hq], jnp.zeros((LANES - ROPE, d), wt.dtype)],
                            axis=0).astype(BF16)
    tm = 512
    rope_specs = [(rc, (tm, LANES), _row_i), (rsa, (tm, LANES), _row_i), (rsb, (tm, LANES), _row_i)]
    qn, kvn, kr = _matmul(
        "latent_proj", [h16], [(0, w_lat, "nk", 0, lat_w)], _latent_epilogue,
        [((n, Q_LORA), BF16, (tm, Q_LORA), _row_i),
         ((n, KV_LORA), BF16, (tm, KV_LORA), _row_i),
         ((n, LANES), BF16, (tm, LANES), _row_i)],
        tm=tm, tn=lat_w,
        extras=[(q_norm_g[l].reshape(1, -1), (1, Q_LORA), lambda i, j: (0, 0)),
                (kv_norm_g[l].reshape(1, -1), (1, KV_LORA), lambda i, j: (0, 0))] + rope_specs)

    scale = (NOPE + ROPE) ** -0.5
    q_w = MLA_HEADS * HEAD_W
    (qf,) = _matmul("q_up_proj", [qn], [(0, _uq_weight(w_uq[l]), "kn", 0, q_w)],
                    functools.partial(_uq_epilogue, scale=scale),
                    [((n, q_w), BF16, (tm, 1024), _tile_ij)],
                    tm=tm, tn=1024, extras=rope_specs)
    kv_w = MLA_HEADS * (NOPE + V_DIM)
    (kv,) = _matmul("kv_up_proj", [kvn], [(0, _ukv_weight(w_ukv[l]), "kn", 0, kv_w)], _ep_ident,
                    [((n, kv_w), BF16, (tm, 1024), _tile_ij)], tm=tm, tn=1024)
    o_a = _attention(qf, kv, kr, batch, seq, MLA_HEADS)

    tm, tn = 1024, 512

    def proj(name, col0, ep, dtype, extras=()):
        (r,) = _matmul(name, [h16], [(0, wt, "nk", col0, HG_WIDTH)], ep,
                       [((n, HG_WIDTH), dtype, (tm, tn), _tile_ij)],
                       tm=tm, tn=tn, extras=list(extras))
        return r

    qs = proj("hg_q_proj", o_hq, _ep_silu, BF16)
    fg = proj("hg_f_proj", o_hf, _ep_forget, F32, [(hg_lb, (hg_lb.shape[0], tn), _col_j)])
    iv = proj("hg_i_proj", o_hi, _ep_ident, BF16)
    gs = proj("hg_g_proj", o_hg, _ep_silu, BF16)
    o_b = _hgrn2(qs, fg, iv, gs, hg_norm_g[l], batch, seq, HG_HEADS)

    tm, tn = 512, 256
    (merged,) = _matmul(
        "gated_merge", [h16, o_a, o_b],
        [(0, wt, "nk", o_ga, d), (1, w_branch_a[l], "kn", 0, d),
         (0, wt, "nk", o_gb, d), (2, w_branch_b[l], "kn", 0, d)],
        _ep_merge, [((n, d), BF16, (tm, tn), _tile_ij)], tm=tm, tn=tn)

    tm, tn = 1024, 512
    (y1,) = _matmul("out_proj", [merged], [(0, w_out[l], "kn", 0, d)], _ep_residual,
                    [((n, d), F32, (tm, tn), _tile_ij)], tm=tm, tn=tn,
                    extras=[(h32, (tm, tn), _tile_ij)])
    h1_32, h1_16 = _layer_norm("ln1", y1, ln1_g[l], ln1_b[l], (F32, BF16))

    tm, tn = 1024, 256
    (act,) = _matmul("swiglu_up", [h1_16], [(0, w_gate[l], "kn", 0, D_FF), (0, w_up[l], "kn", 0, D_FF)],
                     _ep_swiglu, [((n, D_FF), BF16, (tm, tn), _tile_ij)], tm=tm, tn=tn)
    tm, tn = 512, 256
    (y2,) = _matmul("swiglu_down", [act], [(0, w_down[l], "kn", 0, d)], _ep_residual,
                    [((n, d), F32, (tm, tn), _tile_ij)], tm=tm, tn=tn,
                    extras=[(h1_32, (tm, tn), _tile_ij)])
    (out,) = _layer_norm("ln2", y2, ln2_g[l], ln2_b[l], (F32,))
    return out.reshape(batch, seq, d)
```

```python
import functools

import jax
import jax.numpy as jnp
from jax import lax
from jax.experimental import pallas as pl
from jax.experimental.pallas import tpu as pltpu

F32 = jnp.float32
BF16 = jnp.bfloat16

D_MODEL = 4096
CHUNK = 64
MLA_HEADS = 16
NOPE = 128
ROPE = 64
V_DIM = 128
Q_LORA = 1024
KV_LORA = 512
ROPE_THETA = 10000.0
HG_HEADS = 16
HG_DIM = 128
HG_WIDTH = HG_HEADS * HG_DIM
D_FF = 11008
DEPTH = 1
ALPHA = (2 * DEPTH) ** 0.25
EPS = 1e-5
LOG2_E = 1.4426950408889634

LANES = 128
SUBLANES = 8
VMEM_LIMIT = 60 * 1024 * 1024

HEAD_W = 2 * LANES
SUB = 16
HG_CHUNK = 128
ATT_TQ = 256


def _cparams(n_grid, vmem=VMEM_LIMIT):
    return pltpu.CompilerParams(dimension_semantics=("arbitrary",) * n_grid,
                                vmem_limit_bytes=vmem)


def _weight_chunk_copy(w_hbm, stage, sem, spec, jj, ci):
    kind, (col0, tn, cr) = spec
    if kind == "kn":
        src = w_hbm.at[pl.ds(pl.multiple_of(ci * cr, SUBLANES), cr),
                       pl.ds(pl.multiple_of(col0 + jj * tn, LANES), tn)]
    else:
        src = w_hbm.at[pl.ds(pl.multiple_of(col0 + jj * tn + ci * cr, SUBLANES), cr), :]
    return pltpu.make_async_copy(src, stage, sem)


def _mm_kernel(*refs, pair_x, w_specs, n_x, n_extra, n_out, epilogue):
    n_pairs = len(pair_x)
    xs = refs[:n_x]
    w_refs = refs[n_x:n_x + n_pairs]
    extras = refs[n_x + n_pairs:n_x + n_pairs + n_extra]
    outs = refs[n_x + n_pairs + n_extra:n_x + n_pairs + n_extra + n_out]
    scratch = list(refs[n_x + n_pairs + n_extra + n_out:])
    j, i = pl.program_id(0), pl.program_id(1)
    nj, ni = pl.num_programs(0), pl.num_programs(1)
    slot = j % 2

    streamed = []
    tiles = []
    for wr, spec in zip(w_refs, w_specs):
        if spec[1] is None:
            tiles.append(wr)
        else:
            tile16, stage, sem = scratch[:3]
            del scratch[:3]
            streamed.append((wr, spec, tile16, stage, sem))
            tiles.append(tile16.at[slot])

    def chunk_rows(spec, ci):
        cr = spec[1][2]
        return pl.ds(pl.multiple_of(ci * cr, SUBLANES), cr)

    @pl.when((j == 0) & (i == 0))
    def _():
        for wr, spec, tile16, stage, sem in streamed:
            def load(ci, carry):
                cp = _weight_chunk_copy(wr, stage, sem, spec, 0, ci)
                cp.start()
                cp.wait()
                tile16[0, chunk_rows(spec, ci), :] = stage[...].astype(BF16)
                return carry
            lax.fori_loop(0, ni, load, 0)

    @pl.when(j + 1 < nj)
    def _():
        for wr, spec, tile16, stage, sem in streamed:
            _weight_chunk_copy(wr, stage, sem, spec, j + 1, i).start()

    accs = []
    for xi, w, spec in zip(pair_x, tiles, w_specs):
        if spec[0] == "kn":
            accs.append(jnp.dot(xs[xi][...], w[...], preferred_element_type=F32))
        else:
            accs.append(lax.dot_general(xs[xi][...], w[...], (((1,), (1,)), ((), ())),
                                        preferred_element_type=F32))
    res = epilogue(accs, [e[...] for e in extras])
    for o, r in zip(outs, res):
        o[...] = r.astype(o.dtype)

    @pl.when(j + 1 < nj)
    def _():
        for wr, spec, tile16, stage, sem in streamed:
            _weight_chunk_copy(wr, stage, sem, spec, j + 1, i).wait()
            tile16[1 - slot, chunk_rows(spec, i), :] = stage[...].astype(BF16)


def _matmul(name, xs, pairs, epilogue, outs, *, tm, tn, extras=()):
    m = xs[0].shape[0]
    n = pairs[0][4]
    assert m % tm == 0 and n % tn == 0
    nj, ni = n // tn, m // tm
    grid = (nj, ni)

    in_specs = [pl.BlockSpec((tm, x.shape[1]), lambda j, i: (i, 0)) for x in xs]
    scratch, w_specs = [], []
    for _, w, kind, col0, ncols in pairs:
        assert ncols == n
        if w.dtype == BF16:
            assert col0 % tn == 0
            if kind == "kn":
                in_specs.append(pl.BlockSpec((w.shape[0], tn),
                                             lambda j, i, c0=col0 // tn: (0, c0 + j)))
            else:
                in_specs.append(pl.BlockSpec((tn, w.shape[1]),
                                             lambda j, i, c0=col0 // tn: (c0 + j, 0)))
            w_specs.append((kind, None))
            continue
        rows = w.shape[0] if kind == "kn" else tn
        lanes = tn if kind == "kn" else w.shape[1]
        assert rows % (ni * SUBLANES) == 0 and col0 % SUBLANES == 0
        if kind == "kn":
            assert col0 % LANES == 0 and tn % LANES == 0
        cr = rows // ni
        in_specs.append(pl.BlockSpec(memory_space=pl.ANY))
        scratch += [pltpu.VMEM((2, rows, lanes), BF16), pltpu.VMEM((cr, lanes), F32),
                    pltpu.SemaphoreType.DMA(())]
        w_specs.append((kind, (col0, tn, cr)))
    for _, bs, im in extras:
        in_specs.append(pl.BlockSpec(bs, functools.partial(_swap_ji, im)))
    out_specs = [pl.BlockSpec(bs, functools.partial(_swap_ji, im)) for _, _, bs, im in outs]
    out_shape = [jax.ShapeDtypeStruct(s, d) for s, d, _, _ in outs]

    kern = functools.partial(_mm_kernel, pair_x=tuple(p[0] for p in pairs),
                             w_specs=tuple(w_specs), n_x=len(xs), n_extra=len(extras),
                             n_out=len(outs), epilogue=epilogue)
    return pl.pallas_call(
        kern, grid=grid, in_specs=in_specs, out_specs=out_specs, out_shape=out_shape,
        scratch_shapes=scratch, compiler_params=_cparams(2), name=name,
    )(*xs, *[p[1] for p in pairs], *[e[0] for e in extras])


def _swap_ji(im, j, i):
    return im(i, j)


def _tile_ij(i, j):
    return (i, j)


def _row_i(i, j):
    return (i, 0)


def _col_j(i, j):
    return (0, j)


def _sigmoid(x):
    return 1.0 / (1.0 + jnp.exp(-x))


def _silu(x):
    return x * _sigmoid(x)


def _ln_kernel(y_ref, g_ref, b_ref, *outs):
    y = y_ref[...]
    mu = jnp.mean(y, axis=-1, keepdims=True)
    d = y - mu
    var = jnp.mean(d * d, axis=-1, keepdims=True)
    r = d * lax.rsqrt(var + EPS) * g_ref[...] + b_ref[...]
    for o in outs:
        o[...] = r.astype(o.dtype)


def _layer_norm(name, y, g, b, out_dtypes, tm=256):
    m, d = y.shape
    return pl.pallas_call(
        _ln_kernel, grid=(m // tm,), name=name,
        in_specs=[pl.BlockSpec((tm, d), lambda i: (i, 0)),
                  pl.BlockSpec((1, d), lambda i: (0, 0)),
                  pl.BlockSpec((1, d), lambda i: (0, 0))],
        out_specs=[pl.BlockSpec((tm, d), lambda i: (i, 0)) for _ in out_dtypes],
        out_shape=[jax.ShapeDtypeStruct((m, d), dt) for dt in out_dtypes],
        compiler_params=_cparams(1),
    )(y, g.reshape(1, d), b.reshape(1, d))


def _rope_table_kernel(pos_ref, invf_ref, c_ref, sa_ref, sb_ref):
    ang = pos_ref[...].astype(F32) * invf_ref[...]
    lane = lax.broadcasted_iota(jnp.int32, ang.shape, 1)
    cos = jnp.cos(ang)
    sin = jnp.sin(ang)
    half = ROPE // 2
    c_ref[...] = cos
    sa_ref[...] = jnp.where(lane < half, -sin, 0.0)
    sb_ref[...] = jnp.where((lane >= half) & (lane < ROPE), sin, 0.0)


def _rope_tables(positions, tm=1024):
    n = positions.size
    half = ROPE // 2
    inv_freq = ROPE_THETA ** (-jnp.arange(half, dtype=F32) / half)
    invf = jnp.concatenate([inv_freq, inv_freq, jnp.zeros((LANES - ROPE,), F32)]).reshape(1, LANES)
    pos = positions.reshape(n, 1)
    return pl.pallas_call(
        _rope_table_kernel, grid=(n // tm,), name="rope_tables",
        in_specs=[pl.BlockSpec((tm, 1), lambda i: (i, 0)),
                  pl.BlockSpec((1, LANES), lambda i: (0, 0))],
        out_specs=[pl.BlockSpec((tm, LANES), lambda i: (i, 0))] * 3,
        out_shape=[jax.ShapeDtypeStruct((n, LANES), F32)] * 3,
        compiler_params=_cparams(1),
    )(pos, invf)


def _rope_lanes(x, c, sa, sb):
    half = ROPE // 2
    return (x * c + pltpu.roll(x, LANES - half, axis=1) * sa
            + pltpu.roll(x, half, axis=1) * sb)


def _rms(x, g):
    ms = jnp.mean(x * x, axis=-1, keepdims=True)
    return x * lax.rsqrt(ms + EPS) * g


def _latent_epilogue(accs, extras):
    acc = accs[0]
    gq, gkv, c, sa, sb = extras
    qn = _rms(acc[:, :Q_LORA], gq)
    kvn = _rms(acc[:, Q_LORA:Q_LORA + KV_LORA], gkv)
    kr = _rope_lanes(acc[:, Q_LORA + KV_LORA:], c, sa, sb)
    return [qn, kvn, kr]


def _uq_epilogue(accs, extras, *, scale):
    acc = accs[0]
    c, sa, sb = extras
    tn = acc.shape[1]
    cols = []
    for h in range(tn // HEAD_W):
        base = h * HEAD_W
        cols.append(acc[:, base:base + NOPE] * scale)
        cols.append(_rope_lanes(acc[:, base + NOPE:base + HEAD_W], c, sa, sb) * scale)
    return [jnp.concatenate(cols, axis=1)]


def _attn_kernel(q_ref, kn_ref, v_ref, kr_ref, o_ref, kcat_ref, *, tq):
    t = q_ref.shape[0]
    kcat_ref[:, :NOPE] = kn_ref[...]
    kcat_ref[:, NOPE:] = kr_ref[...]
    row_chunk = lax.broadcasted_iota(jnp.int32, (tq, tq), 0) // CHUNK
    col_chunk = lax.broadcasted_iota(jnp.int32, (tq, tq), 1) // CHUNK
    visible = col_chunk <= row_chunk
    nt = (((1,), (1,)), ((), ()))
    for jq in range(t // tq):
        s0, s1 = jq * tq, (jq + 1) * tq
        q = q_ref[s0:s1, :]
        sd = lax.dot_general(q, kcat_ref[s0:s1, :], nt, preferred_element_type=F32)
        sd = jnp.where(visible, sd, -jnp.inf)
        m = jnp.max(sd, axis=-1, keepdims=True)
        if jq > 0:
            sp = lax.dot_general(q, kcat_ref[0:s0, :], nt, preferred_element_type=F32)
            m = jnp.maximum(m, jnp.max(sp, axis=-1, keepdims=True))
        pd = jnp.exp(sd - m)
        l = jnp.sum(pd, axis=-1, keepdims=True)
        o = jnp.dot(pd.astype(BF16), v_ref[s0:s1, :], preferred_element_type=F32)
        if jq > 0:
            pp = jnp.exp(sp - m)
            l = l + jnp.sum(pp, axis=-1, keepdims=True)
            o = o + jnp.dot(pp.astype(BF16), v_ref[0:s0, :], preferred_element_type=F32)
        o_ref[s0:s1, :] = (o / l).astype(o_ref.dtype)


def _attention(qf, kv, kr, batch, seq, heads):
    n = batch * seq
    kern = functools.partial(_attn_kernel, tq=ATT_TQ)
    return pl.pallas_call(
        kern, grid=(batch, heads), name="mla_attention",
        in_specs=[pl.BlockSpec((seq, HEAD_W), lambda b, h: (b, h)),
                  pl.BlockSpec((seq, NOPE), lambda b, h: (b, h)),
                  pl.BlockSpec((seq, V_DIM), lambda b, h: (b, heads + h)),
                  pl.BlockSpec((seq, LANES), lambda b, h: (b, 0))],
        out_specs=pl.BlockSpec((seq, V_DIM), lambda b, h: (b, h)),
        out_shape=jax.ShapeDtypeStruct((n, heads * V_DIM), BF16),
        scratch_shapes=[pltpu.VMEM((seq, HEAD_W), BF16)],
        compiler_params=_cparams(2),
    )(qf, kv, kv, kr)


def _split3(x):
    a = x.astype(BF16)
    r = x - a.astype(F32)
    b = r.astype(BF16)
    c = (r - b.astype(F32)).astype(BF16)
    return a, b, c


def _hgrn_head_chunk(hh, r0, q_ref, f_ref, i_ref, gs_ref, gn_ref, sel_ref, o_ref, st_ref,
                     a_ref, masks):
    c, d, n_sub = HG_CHUNK, HG_DIM, HG_CHUNK // SUB
    tri, m_diag, m_l16, m_l32 = masks
    lanes = slice(hh * d, (hh + 1) * d)
    rows = pl.ds(r0, c)
    nt = (((1,), (1,)), ((), ()))

    q = q_ref[rows, lanes].astype(F32)
    f = f_ref[rows, lanes]
    v = i_ref[rows, lanes]
    g = jnp.log(f) * LOG2_E
    k = 1.0 - f
    cs = jnp.dot(tri, jnp.concatenate(_split3(g), axis=1), preferred_element_type=F32)
    b = cs[:, :d] + cs[:, d:2 * d] + cs[:, 2 * d:]
    b_last = b[c - 1:c, :]

    q3 = q.reshape(n_sub, SUB, d)
    b3 = b.reshape(n_sub, SUB, d)
    c3 = (b - jnp.log(k) * LOG2_E).reshape(n_sub, SUB, d)
    half = SUB // 2
    for s in range(SUB):
        lo = 0 if s < half else half
        a = q3[:, lo:, :] * jnp.exp2(jnp.minimum(b3[:, lo:, :] - c3[:, s:s + 1, :], 0.0))
        if lo:
            a = jnp.concatenate([jnp.zeros((n_sub, lo, d), F32), a], axis=1)
        a_ref[hh, :, s * d:(s + 1) * d] = a.reshape(c, d).astype(BF16)
    p_diag = jnp.dot(a_ref[hh], sel_ref[...], preferred_element_type=F32)

    def level(hs):
        blk = 2 * hs
        qt, kt = [], []
        for i in range(0, c, hs):
            edge = (i // blk) * blk + hs - 1
            e = jnp.exp2(-jnp.abs(b[i:i + hs, :] - b[edge:edge + 1, :]))
            zero = jnp.zeros((hs, d), F32)
            upper = (i % blk) >= hs
            qt.append(q[i:i + hs, :] * e if upper else zero)
            kt.append(zero if upper else k[i:i + hs, :] * e)
        qt = jnp.concatenate(qt, axis=0).astype(BF16)
        kt = jnp.concatenate(kt, axis=0).astype(BF16)
        return lax.dot_general(qt, kt, nt, preferred_element_type=F32)

    p = jnp.where(m_diag, p_diag,
                  jnp.where(m_l16, level(SUB), jnp.where(m_l32, level(2 * SUB), level(4 * SUB))))
    o = jnp.dot(p.astype(BF16), v, preferred_element_type=F32)

    st = st_ref[hh]
    q0 = (q * jnp.exp2(b)).astype(BF16)
    o = o + lax.dot_general(q0, st.astype(BF16), nt, preferred_element_type=F32)
    kl = (k * jnp.exp2(b_last - b)).astype(BF16)
    vt = v.astype(F32).T.astype(BF16)
    st_ref[hh] = jnp.exp2(b_last) * st + jnp.dot(vt, kl, preferred_element_type=F32)

    ms = jnp.mean(o * o, axis=-1, keepdims=True)
    ob = o * lax.rsqrt(ms + EPS) * gn_ref[...] * gs_ref[rows, lanes].astype(F32)
    o_ref[rows, lanes] = ob.astype(o_ref.dtype)


def _hgrn_kernel(q_ref, f_ref, i_ref, gs_ref, gn_ref, sel_ref, o_ref, st_ref, a_ref, *, hpb):
    t = q_ref.shape[0]
    c, d = HG_CHUNK, HG_DIM
    row = lax.broadcasted_iota(jnp.int32, (c, d), 0)
    col = lax.broadcasted_iota(jnp.int32, (c, d), 1)
    tri = (col <= row).astype(BF16)
    m_diag = ((row // SUB) == (col // SUB)) & ((col % SUB) <= (row % SUB))
    m_l16 = (row // (2 * SUB)) == (col // (2 * SUB))
    m_l32 = (row // (4 * SUB)) == (col // (4 * SUB))
    masks = (tri, m_diag, m_l16, m_l32)

    st_ref[...] = jnp.zeros_like(st_ref)

    def chunk(ci, carry):
        r0 = pl.multiple_of(ci * c, c)
        for hh in range(hpb):
            _hgrn_head_chunk(hh, r0, q_ref, f_ref, i_ref, gs_ref, gn_ref, sel_ref, o_ref,
                             st_ref, a_ref, masks)
        return carry

    lax.fori_loop(0, t // c, chunk, 0)


def _hgrn2(qs, f, iv, gs, gnorm, batch, seq, heads, hpb=2):
    n = batch * seq
    d = HG_DIM
    sel = (jnp.arange(SUB * d)[:, None] // d == jnp.arange(d)[None, :] % SUB).astype(BF16)
    spec = pl.BlockSpec((seq, hpb * d), lambda b, h: (b, h))
    return pl.pallas_call(
        functools.partial(_hgrn_kernel, hpb=hpb), grid=(batch, heads // hpb), name="hgrn2",
        in_specs=[spec, spec, spec, spec, pl.BlockSpec((1, d), lambda b, h: (0, 0)),
                  pl.BlockSpec((SUB * d, d), lambda b, h: (0, 0))],
        out_specs=spec,
        out_shape=jax.ShapeDtypeStruct((n, heads * d), BF16),
        scratch_shapes=[pltpu.VMEM((hpb, d, d), F32),
                        pltpu.VMEM((hpb, HG_CHUNK, SUB * d), BF16)],
        compiler_params=_cparams(2),
    )(qs, f, iv, gs, gnorm.reshape(1, d), sel)


def _ep_silu(accs, extras):
    return [_silu(accs[0])]


def _ep_ident(accs, extras):
    return [accs[0]]


def _ep_forget(accs, extras):
    lbp = extras[0]
    m = jnp.max(lbp, axis=0, keepdims=True)
    e = jnp.exp(lbp - m)
    lb = e[0:1, :] / jnp.sum(e, axis=0, keepdims=True)
    return [lb + (1.0 - lb) * _sigmoid(accs[0])]


def _ep_merge(accs, extras):
    ga, a, gb, b = accs
    return [_sigmoid(ga) * a + _sigmoid(gb) * b]


def _ep_residual(accs, extras):
    return [ALPHA * extras[0] + accs[0]]


def _ep_swiglu(accs, extras):
    return [_silu(accs[0]) * accs[1]]


def _uq_weight(w_uq):
    r = w_uq.shape[0]
    w = w_uq.reshape(r, MLA_HEADS, NOPE + ROPE)
    w = jnp.concatenate([w, jnp.zeros((r, MLA_HEADS, HEAD_W - NOPE - ROPE), w.dtype)], axis=-1)
    return w.reshape(r, MLA_HEADS * HEAD_W).astype(BF16)


def _ukv_weight(w_ukv):
    r = w_ukv.shape[0]
    w = w_ukv.reshape(r, MLA_HEADS, NOPE + V_DIM)
    return jnp.concatenate([w[:, :, :NOPE].reshape(r, -1), w[:, :, NOPE:].reshape(r, -1)],
                           axis=1).astype(BF16)


def kernel(x, positions, ln_in_g, ln_in_b, w_in, q_norm_g, w_uq, kv_norm_g, w_ukv, hg_lb,
           hg_norm_g, w_branch_a, w_branch_b, w_out, ln1_g, ln1_b, w_gate, w_up, w_down,
           ln2_g, ln2_b):
    batch, seq, d = x.shape
    n = batch * seq
    assert w_in.shape[0] == DEPTH == 1
    l = 0
    wt = w_in.reshape(d, w_in.shape[-1]).T

    o_kr = Q_LORA + KV_LORA
    o_hq = o_kr + ROPE
    o_hf, o_hi, o_hg = o_hq + HG_WIDTH, o_hq + 2 * HG_WIDTH, o_hq + 3 * HG_WIDTH
    o_ga = o_hq + 4 * HG_WIDTH
    o_gb = o_ga + D_MODEL

    h32, h16 = _layer_norm("ln_in", x.reshape(n, d), ln_in_g, ln_in_b, (F32, BF16))
    rc, rsa, rsb = _rope_tables(positions)

    lat_w = o_kr + LANES
    w_lat = jnp.concatenate([wt[:o_hq], jnp.zeros((LANES - ROPE, d), wt.dtype)],
                            axis=0).astype(BF16)
    tm = 512
    rope_specs = [(rc, (tm, LANES), _row_i), (rsa, (tm, LANES), _row_i), (rsb, (tm, LANES), _row_i)]
    qn, kvn, kr = _matmul(
        "latent_proj", [h16], [(0, w_lat, "nk", 0, lat_w)], _latent_epilogue,
        [((n, Q_LORA), BF16, (tm, Q_LORA), _row_i),
         ((n, KV_LORA), BF16, (tm, KV_LORA), _row_i),
         ((n, LANES), BF16, (tm, LANES), _row_i)],
        tm=tm, tn=lat_w,
        extras=[(q_norm_g[l].reshape(1, -1), (1, Q_LORA), lambda i, j: (0, 0)),
                (kv_norm_g[l].reshape(1, -1), (1, KV_LORA), lambda i, j: (0, 0))] + rope_specs)

    scale = (NOPE + ROPE) ** -0.5
    q_w = MLA_HEADS * HEAD_W
    (qf,) = _matmul("q_up_proj", [qn], [(0, _uq_weight(w_uq[l]), "kn", 0, q_w)],
                    functools.partial(_uq_epilogue, scale=scale),
                    [((n, q_w), BF16, (tm, 1024), _tile_ij)],
                    tm=tm, tn=1024, extras=rope_specs)
    kv_w = MLA_HEADS * (NOPE + V_DIM)
    (kv,) = _matmul("kv_up_proj", [kvn], [(0, _ukv_weight(w_ukv[l]), "kn", 0, kv_w)], _ep_ident,
                    [((n, kv_w), BF16, (tm, 1024), _tile_ij)], tm=tm, tn=1024)
    o_a = _attention(qf, kv, kr, batch, seq, MLA_HEADS)

    tm, tn = 1024, 1024

    def proj(name, col0, ep, dtype, extras=()):
        (r,) = _matmul(name, [h16], [(0, wt, "nk", col0, HG_WIDTH)], ep,
                       [((n, HG_WIDTH), dtype, (tm, tn), _tile_ij)],
                       tm=tm, tn=tn, extras=list(extras))
        return r

    qs = proj("hg_q_proj", o_hq, _ep_silu, BF16)
    fg = proj("hg_f_proj", o_hf, _ep_forget, F32, [(hg_lb, (hg_lb.shape[0], tn), _col_j)])
    iv = proj("hg_i_proj", o_hi, _ep_ident, BF16)
    gs = proj("hg_g_proj", o_hg, _ep_silu, BF16)
    o_b = _hgrn2(qs, fg, iv, gs, hg_norm_g[l], batch, seq, HG_HEADS)

    tm, tn = 512, 512
    (merged,) = _matmul(
        "gated_merge", [h16, o_a, o_b],
        [(0, wt, "nk", o_ga, d), (1, w_branch_a[l], "kn", 0, d),
         (0, wt, "nk", o_gb, d), (2, w_branch_b[l], "kn", 0, d)],
        _ep_merge, [((n, d), BF16, (tm, tn), _tile_ij)], tm=tm, tn=tn)

    tm, tn = 1024, 1024
    (y1,) = _matmul("out_proj", [merged], [(0, w_out[l], "kn", 0, d)], _ep_residual,
                    [((n, d), F32, (tm, tn), _tile_ij)], tm=tm, tn=tn,
                    extras=[(h32, (tm, tn), _tile_ij)])
    h1_32, h1_16 = _layer_norm("ln1", y1, ln1_g[l], ln1_b[l], (F32, BF16))

    tm, tn = 2048, 256
    (act,) = _matmul("swiglu_up", [h1_16], [(0, w_gate[l], "kn", 0, D_FF), (0, w_up[l], "kn", 0, D_FF)],
                     _ep_swiglu, [((n, D_FF), BF16, (tm, tn), _tile_ij)], tm=tm, tn=tn)
    tm, tn = 512, 512
    (y2,) = _matmul("swiglu_down", [act], [(0, w_down[l], "kn", 0, d)], _ep_residual,
                    [((n, d), F32, (tm, tn), _tile_ij)], tm=tm, tn=tn,
                    extras=[(h1_32, (tm, tn), _tile_ij)])
    (out,) = _layer_norm("ln2", y2, ln2_g[l], ln2_b[l], (F32,))
    return out.reshape(batch, seq, d)
```

```python
import functools

import jax
import jax.numpy as jnp
from jax import lax
from jax.experimental import pallas as pl
from jax.experimental.pallas import tpu as pltpu

F32 = jnp.float32
BF16 = jnp.bfloat16

D_MODEL = 4096
CHUNK = 64
MLA_HEADS = 16
NOPE = 128
ROPE = 64
V_DIM = 128
Q_LORA = 1024
KV_LORA = 512
ROPE_THETA = 10000.0
HG_HEADS = 16
HG_DIM = 128
HG_WIDTH = HG_HEADS * HG_DIM
D_FF = 11008
DEPTH = 1
ALPHA = (2 * DEPTH) ** 0.25
EPS = 1e-5
LOG2_E = 1.4426950408889634

LANES = 128
SUBLANES = 8
VMEM_LIMIT = 60 * 1024 * 1024

HEAD_W = 2 * LANES
SUB = 16
HG_CHUNK = 128
ATT_TQ = 256


def _cparams(n_grid, vmem=VMEM_LIMIT):
    return pltpu.CompilerParams(dimension_semantics=("arbitrary",) * n_grid,
                                vmem_limit_bytes=vmem)


def _weight_chunk_copy(w_hbm, stage, sem, spec, jj, ci):
    kind, (col0, tn, cr) = spec
    if kind == "kn":
        src = w_hbm.at[pl.ds(pl.multiple_of(ci * cr, SUBLANES), cr),
                       pl.ds(pl.multiple_of(col0 + jj * tn, LANES), tn)]
    else:
        src = w_hbm.at[pl.ds(pl.multiple_of(col0 + jj * tn + ci * cr, SUBLANES), cr), :]
    return pltpu.make_async_copy(src, stage, sem)


def _mm_kernel(*refs, pair_x, w_specs, n_x, n_extra, n_out, epilogue):
    n_pairs = len(pair_x)
    xs = refs[:n_x]
    w_refs = refs[n_x:n_x + n_pairs]
    extras = refs[n_x + n_pairs:n_x + n_pairs + n_extra]
    outs = refs[n_x + n_pairs + n_extra:n_x + n_pairs + n_extra + n_out]
    scratch = list(refs[n_x + n_pairs + n_extra + n_out:])
    j, i = pl.program_id(0), pl.program_id(1)
    nj, ni = pl.num_programs(0), pl.num_programs(1)
    slot = j % 2

    streamed = []
    tiles = []
    for wr, spec in zip(w_refs, w_specs):
        if spec[1] is None:
            tiles.append(wr)
        else:
            tile16, stage, sem = scratch[:3]
            del scratch[:3]
            streamed.append((wr, spec, tile16, stage, sem))
            tiles.append(tile16.at[slot])

    def chunk_rows(spec, ci):
        cr = spec[1][2]
        return pl.ds(pl.multiple_of(ci * cr, SUBLANES), cr)

    @pl.when((j == 0) & (i == 0))
    def _():
        for wr, spec, tile16, stage, sem in streamed:
            def load(ci, carry):
                cp = _weight_chunk_copy(wr, stage, sem, spec, 0, ci)
                cp.start()
                cp.wait()
                tile16[0, chunk_rows(spec, ci), :] = stage[...].astype(BF16)
                return carry
            lax.fori_loop(0, ni, load, 0)

            @pl.when(nj > 1)
            def _():
                _weight_chunk_copy(wr, stage, sem, spec, 1, 0).start()

    accs = []
    for xi, w, spec in zip(pair_x, tiles, w_specs):
        if spec[0] == "kn":
            accs.append(jnp.dot(xs[xi][...], w[...], preferred_element_type=F32))
        else:
            accs.append(lax.dot_general(xs[xi][...], w[...], (((1,), (1,)), ((), ())),
                                        preferred_element_type=F32))
    res = epilogue(accs, [e[...] for e in extras])
    for o, r in zip(outs, res):
        o[...] = r.astype(o.dtype)

    for wr, spec, tile16, stage, sem in streamed:
        @pl.when(j + 1 < nj)
        def _():
            _weight_chunk_copy(wr, stage, sem, spec, j + 1, i).wait()
            tile16[1 - slot, chunk_rows(spec, i), :] = stage[...].astype(BF16)

        @pl.when((j + 1 < nj) & (i + 1 < ni))
        def _():
            _weight_chunk_copy(wr, stage, sem, spec, j + 1, i + 1).start()

        @pl.when((i + 1 == ni) & (j + 2 < nj))
        def _():
            _weight_chunk_copy(wr, stage, sem, spec, j + 2, 0).start()


def _matmul(name, xs, pairs, epilogue, outs, *, tm, tn, extras=()):
    m = xs[0].shape[0]
    n = pairs[0][4]
    assert m % tm == 0 and n % tn == 0
    nj, ni = n // tn, m // tm
    grid = (nj, ni)

    in_specs = [pl.BlockSpec((tm, x.shape[1]), lambda j, i: (i, 0)) for x in xs]
    scratch, w_specs = [], []
    for _, w, kind, col0, ncols in pairs:
        assert ncols == n
        if w.dtype == BF16:
            assert col0 % tn == 0
            if kind == "kn":
                in_specs.append(pl.BlockSpec((w.shape[0], tn),
                                             lambda j, i, c0=col0 // tn: (0, c0 + j)))
            else:
                in_specs.append(pl.BlockSpec((tn, w.shape[1]),
                                             lambda j, i, c0=col0 // tn: (c0 + j, 0)))
            w_specs.append((kind, None))
            continue
        rows = w.shape[0] if kind == "kn" else tn
        lanes = tn if kind == "kn" else w.shape[1]
        assert rows % (ni * SUBLANES) == 0 and col0 % SUBLANES == 0
        if kind == "kn":
            assert col0 % LANES == 0 and tn % LANES == 0
        cr = rows // ni
        in_specs.append(pl.BlockSpec(memory_space=pl.ANY))
        scratch += [pltpu.VMEM((2, rows, lanes), BF16), pltpu.VMEM((cr, lanes), F32),
                    pltpu.SemaphoreType.DMA(())]
        w_specs.append((kind, (col0, tn, cr)))
    for _, bs, im in extras:
        in_specs.append(pl.BlockSpec(bs, functools.partial(_swap_ji, im)))
    out_specs = [pl.BlockSpec(bs, functools.partial(_swap_ji, im)) for _, _, bs, im in outs]
    out_shape = [jax.ShapeDtypeStruct(s, d) for s, d, _, _ in outs]

    kern = functools.partial(_mm_kernel, pair_x=tuple(p[0] for p in pairs),
                             w_specs=tuple(w_specs), n_x=len(xs), n_extra=len(extras),
                             n_out=len(outs), epilogue=epilogue)
    return pl.pallas_call(
        kern, grid=grid, in_specs=in_specs, out_specs=out_specs, out_shape=out_shape,
        scratch_shapes=scratch, compiler_params=_cparams(2), name=name,
    )(*xs, *[p[1] for p in pairs], *[e[0] for e in extras])


def _swap_ji(im, j, i):
    return im(i, j)


def _tile_ij(i, j):
    return (i, j)


def _row_i(i, j):
    return (i, 0)


def _col_j(i, j):
    return (0, j)


def _sigmoid(x):
    return 1.0 / (1.0 + jnp.exp(-x))


def _silu(x):
    return x * _sigmoid(x)


def _ln_kernel(y_ref, g_ref, b_ref, *outs):
    y = y_ref[...]
    mu = jnp.mean(y, axis=-1, keepdims=True)
    d = y - mu
    var = jnp.mean(d * d, axis=-1, keepdims=True)
    r = d * lax.rsqrt(var + EPS) * g_ref[...] + b_ref[...]
    for o in outs:
        o[...] = r.astype(o.dtype)


def _layer_norm(name, y, g, b, out_dtypes, tm=256):
    m, d = y.shape
    return pl.pallas_call(
        _ln_kernel, grid=(m // tm,), name=name,
        in_specs=[pl.BlockSpec((tm, d), lambda i: (i, 0)),
                  pl.BlockSpec((1, d), lambda i: (0, 0)),
                  pl.BlockSpec((1, d), lambda i: (0, 0))],
        out_specs=[pl.BlockSpec((tm, d), lambda i: (i, 0)) for _ in out_dtypes],
        out_shape=[jax.ShapeDtypeStruct((m, d), dt) for dt in out_dtypes],
        compiler_params=_cparams(1),
    )(y, g.reshape(1, d), b.reshape(1, d))


def _rope_table_kernel(pos_ref, invf_ref, c_ref, sa_ref, sb_ref):
    ang = pos_ref[...].astype(F32) * invf_ref[...]
    lane = lax.broadcasted_iota(jnp.int32, ang.shape, 1)
    cos = jnp.cos(ang)
    sin = jnp.sin(ang)
    half = ROPE // 2
    c_ref[...] = cos
    sa_ref[...] = jnp.where(lane < half, -sin, 0.0)
    sb_ref[...] = jnp.where((lane >= half) & (lane < ROPE), sin, 0.0)


def _rope_tables(positions, tm=1024):
    n = positions.size
    half = ROPE // 2
    inv_freq = ROPE_THETA ** (-jnp.arange(half, dtype=F32) / half)
    invf = jnp.concatenate([inv_freq, inv_freq, jnp.zeros((LANES - ROPE,), F32)]).reshape(1, LANES)
    pos = positions.reshape(n, 1)
    return pl.pallas_call(
        _rope_table_kernel, grid=(n // tm,), name="rope_tables",
        in_specs=[pl.BlockSpec((tm, 1), lambda i: (i, 0)),
                  pl.BlockSpec((1, LANES), lambda i: (0, 0))],
        out_specs=[pl.BlockSpec((tm, LANES), lambda i: (i, 0))] * 3,
        out_shape=[jax.ShapeDtypeStruct((n, LANES), F32)] * 3,
        compiler_params=_cparams(1),
    )(pos, invf)


def _rope_lanes(x, c, sa, sb):
    half = ROPE // 2
    return (x * c + pltpu.roll(x, LANES - half, axis=1) * sa
            + pltpu.roll(x, half, axis=1) * sb)


def _rms(x, g):
    ms = jnp.mean(x * x, axis=-1, keepdims=True)
    return x * lax.rsqrt(ms + EPS) * g


def _latent_epilogue(accs, extras):
    acc = accs[0]
    gq, gkv, c, sa, sb = extras
    qn = _rms(acc[:, :Q_LORA], gq)
    kvn = _rms(acc[:, Q_LORA:Q_LORA + KV_LORA], gkv)
    kr = _rope_lanes(acc[:, Q_LORA + KV_LORA:], c, sa, sb)
    return [qn, kvn, kr]


def _uq_epilogue(accs, extras, *, scale):
    acc = accs[0]
    c, sa, sb = extras
    tn = acc.shape[1]
    cols = []
    for h in range(tn // HEAD_W):
        base = h * HEAD_W
        cols.append(acc[:, base:base + NOPE] * scale)
        cols.append(_rope_lanes(acc[:, base + NOPE:base + HEAD_W], c, sa, sb) * scale)
    return [jnp.concatenate(cols, axis=1)]


def _attn_kernel(q_ref, kn_ref, v_ref, kr_ref, o_ref, kcat_ref, *, tq):
    t = q_ref.shape[0]
    kcat_ref[:, :NOPE] = kn_ref[...]
    kcat_ref[:, NOPE:] = kr_ref[...]
    row_chunk = lax.broadcasted_iota(jnp.int32, (tq, tq), 0) // CHUNK
    col_chunk = lax.broadcasted_iota(jnp.int32, (tq, tq), 1) // CHUNK
    visible = col_chunk <= row_chunk
    nt = (((1,), (1,)), ((), ()))
    for jq in range(t // tq):
        s0, s1 = jq * tq, (jq + 1) * tq
        q = q_ref[s0:s1, :]
        sd = lax.dot_general(q, kcat_ref[s0:s1, :], nt, preferred_element_type=F32)
        sd = jnp.where(visible, sd, -jnp.inf)
        m = jnp.max(sd, axis=-1, keepdims=True)
        if jq > 0:
            sp = lax.dot_general(q, kcat_ref[0:s0, :], nt, preferred_element_type=F32)
            m = jnp.maximum(m, jnp.max(sp, axis=-1, keepdims=True))
        pd = jnp.exp(sd - m)
        l = jnp.sum(pd, axis=-1, keepdims=True)
        o = jnp.dot(pd.astype(BF16), v_ref[s0:s1, :], preferred_element_type=F32)
        if jq > 0:
            pp = jnp.exp(sp - m)
            l = l + jnp.sum(pp, axis=-1, keepdims=True)
            o = o + jnp.dot(pp.astype(BF16), v_ref[0:s0, :], preferred_element_type=F32)
        o_ref[s0:s1, :] = (o / l).astype(o_ref.dtype)


def _attention(qf, kv, kr, batch, seq, heads):
    n = batch * seq
    kern = functools.partial(_attn_kernel, tq=ATT_TQ)
    return pl.pallas_call(
        kern, grid=(batch, heads), name="mla_attention",
        in_specs=[pl.BlockSpec((seq, HEAD_W), lambda b, h: (b, h)),
                  pl.BlockSpec((seq, NOPE), lambda b, h: (b, h)),
                  pl.BlockSpec((seq, V_DIM), lambda b, h: (b, heads + h)),
                  pl.BlockSpec((seq, LANES), lambda b, h: (b, 0))],
        out_specs=pl.BlockSpec((seq, V_DIM), lambda b, h: (b, h)),
        out_shape=jax.ShapeDtypeStruct((n, heads * V_DIM), BF16),
        scratch_shapes=[pltpu.VMEM((seq, HEAD_W), BF16)],
        compiler_params=_cparams(2),
    )(qf, kv, kv, kr)


def _split3(x):
    a = x.astype(BF16)
    r = x - a.astype(F32)
    b = r.astype(BF16)
    c = (r - b.astype(F32)).astype(BF16)
    return a, b, c


def _hgrn_head_chunk(hh, r0, q_ref, f_ref, i_ref, gs_ref, gn_ref, sel_ref, o_ref, st_ref,
                     a_ref, masks):
    c, d, n_sub = HG_CHUNK, HG_DIM, HG_CHUNK // SUB
    tri, m_diag, m_l16, m_l32 = masks
    lanes = slice(hh * d, (hh + 1) * d)
    rows = pl.ds(r0, c)
    nt = (((1,), (1,)), ((), ()))

    q = q_ref[rows, lanes].astype(F32)
    f = f_ref[rows, lanes]
    v = i_ref[rows, lanes]
    g = jnp.log(f) * LOG2_E
    k = 1.0 - f
    cs = jnp.dot(tri, jnp.concatenate(_split3(g), axis=1), preferred_element_type=F32)
    b = cs[:, :d] + cs[:, d:2 * d] + cs[:, 2 * d:]
    b_last = b[c - 1:c, :]

    q3 = q.reshape(n_sub, SUB, d)
    b3 = b.reshape(n_sub, SUB, d)
    c3 = (b - jnp.log(k) * LOG2_E).reshape(n_sub, SUB, d)
    half = SUB // 2
    for s in range(SUB):
        lo = 0 if s < half else half
        a = q3[:, lo:, :] * jnp.exp2(jnp.minimum(b3[:, lo:, :] - c3[:, s:s + 1, :], 0.0))
        if lo:
            a = jnp.concatenate([jnp.zeros((n_sub, lo, d), F32), a], axis=1)
        a_ref[hh, :, s * d:(s + 1) * d] = a.reshape(c, d).astype(BF16)
    p_diag = jnp.dot(a_ref[hh], sel_ref[...], preferred_element_type=F32)

    def level(hs):
        blk = 2 * hs
        qt, kt = [], []
        for i in range(0, c, hs):
            edge = (i // blk) * blk + hs - 1
            e = jnp.exp2(-jnp.abs(b[i:i + hs, :] - b[edge:edge + 1, :]))
            zero = jnp.zeros((hs, d), F32)
            upper = (i % blk) >= hs
            qt.append(q[i:i + hs, :] * e if upper else zero)
            kt.append(zero if upper else k[i:i + hs, :] * e)
        qt = jnp.concatenate(qt, axis=0).astype(BF16)
        kt = jnp.concatenate(kt, axis=0).astype(BF16)
        return lax.dot_general(qt, kt, nt, preferred_element_type=F32)

    p = jnp.where(m_diag, p_diag,
                  jnp.where(m_l16, level(SUB), jnp.where(m_l32, level(2 * SUB), level(4 * SUB))))
    o = jnp.dot(p.astype(BF16), v, preferred_element_type=F32)

    st = st_ref[hh]
    q0 = (q * jnp.exp2(b)).astype(BF16)
    o = o + lax.dot_general(q0, st.astype(BF16), nt, preferred_element_type=F32)
    kl = (k * jnp.exp2(b_last - b)).astype(BF16)
    vt = v.astype(F32).T.astype(BF16)
    st_ref[hh] = jnp.exp2(b_last) * st + jnp.dot(vt, kl, preferred_element_type=F32)

    ms = jnp.mean(o * o, axis=-1, keepdims=True)
    ob = o * lax.rsqrt(ms + EPS) * gn_ref[...] * gs_ref[rows, lanes].astype(F32)
    o_ref[rows, lanes] = ob.astype(o_ref.dtype)


def _hgrn_kernel(q_ref, f_ref, i_ref, gs_ref, gn_ref, sel_ref, o_ref, st_ref, a_ref, *, hpb):
    t = q_ref.shape[0]
    c, d = HG_CHUNK, HG_DIM
    row = lax.broadcasted_iota(jnp.int32, (c, d), 0)
    col = lax.broadcasted_iota(jnp.int32, (c, d), 1)
    tri = (col <= row).astype(BF16)
    m_diag = ((row // SUB) == (col // SUB)) & ((col % SUB) <= (row % SUB))
    m_l16 = (row // (2 * SUB)) == (col // (2 * SUB))
    m_l32 = (row // (4 * SUB)) == (col // (4 * SUB))
    masks = (tri, m_diag, m_l16, m_l32)

    st_ref[...] = jnp.zeros_like(st_ref)

    def chunk(ci, carry):
        r0 = pl.multiple_of(ci * c, c)
        for hh in range(hpb):
            _hgrn_head_chunk(hh, r0, q_ref, f_ref, i_ref, gs_ref, gn_ref, sel_ref, o_ref,
                             st_ref, a_ref, masks)
        return carry

    lax.fori_loop(0, t // c, chunk, 0)


def _hgrn2(qs, f, iv, gs, gnorm, batch, seq, heads, hpb=2):
    n = batch * seq
    d = HG_DIM
    sel = (jnp.arange(SUB * d)[:, None] // d == jnp.arange(d)[None, :] % SUB).astype(BF16)
    spec = pl.BlockSpec((seq, hpb * d), lambda b, h: (b, h))
    return pl.pallas_call(
        functools.partial(_hgrn_kernel, hpb=hpb), grid=(batch, heads // hpb), name="hgrn2",
        in_specs=[spec, spec, spec, spec, pl.BlockSpec((1, d), lambda b, h: (0, 0)),
                  pl.BlockSpec((SUB * d, d), lambda b, h: (0, 0))],
        out_specs=spec,
        out_shape=jax.ShapeDtypeStruct((n, heads * d), BF16),
        scratch_shapes=[pltpu.VMEM((hpb, d, d), F32),
                        pltpu.VMEM((hpb, HG_CHUNK, SUB * d), BF16)],
        compiler_params=_cparams(2),
    )(qs, f, iv, gs, gnorm.reshape(1, d), sel)


def _ep_silu(accs, extras):
    return [_silu(accs[0])]


def _ep_ident(accs, extras):
    return [accs[0]]


def _ep_forget(accs, extras):
    lbp = extras[0]
    m = jnp.max(lbp, axis=0, keepdims=True)
    e = jnp.exp(lbp - m)
    lb = e[0:1, :] / jnp.sum(e, axis=0, keepdims=True)
    return [lb + (1.0 - lb) * _sigmoid(accs[0])]


def _ep_merge(accs, extras):
    ga, a, gb, b = accs
    return [_sigmoid(ga) * a + _sigmoid(gb) * b]


def _ep_residual(accs, extras):
    return [ALPHA * extras[0] + accs[0]]


def _ep_swiglu(accs, extras):
    return [_silu(accs[0]) * accs[1]]


def _uq_weight(w_uq):
    r = w_uq.shape[0]
    w = w_uq.reshape(r, MLA_HEADS, NOPE + ROPE)
    w = jnp.concatenate([w, jnp.zeros((r, MLA_HEADS, HEAD_W - NOPE - ROPE), w.dtype)], axis=-1)
    return w.reshape(r, MLA_HEADS * HEAD_W).astype(BF16)


def _ukv_weight(w_ukv):
    r = w_ukv.shape[0]
    w = w_ukv.reshape(r, MLA_HEADS, NOPE + V_DIM)
    return jnp.concatenate([w[:, :, :NOPE].reshape(r, -1), w[:, :, NOPE:].reshape(r, -1)],
                           axis=1).astype(BF16)


def kernel(x, positions, ln_in_g, ln_in_b, w_in, q_norm_g, w_uq, kv_norm_g, w_ukv, hg_lb,
           hg_norm_g, w_branch_a, w_branch_b, w_out, ln1_g, ln1_b, w_gate, w_up, w_down,
           ln2_g, ln2_b):
    batch, seq, d = x.shape
    n = batch * seq
    assert w_in.shape[0] == DEPTH == 1
    l = 0
    wt = w_in.reshape(d, w_in.shape[-1]).T

    o_kr = Q_LORA + KV_LORA
    o_hq = o_kr + ROPE
    o_hf, o_hi, o_hg = o_hq + HG_WIDTH, o_hq + 2 * HG_WIDTH, o_hq + 3 * HG_WIDTH
    o_ga = o_hq + 4 * HG_WIDTH
    o_gb = o_ga + D_MODEL

    h32, h16 = _layer_norm("ln_in", x.reshape(n, d), ln_in_g, ln_in_b, (F32, BF16))
    rc, rsa, rsb = _rope_tables(positions)

    lat_w = o_kr + LANES
    w_lat = jnp.concatenate([wt[:o_hq], jnp.zeros((LANES - ROPE, d), wt.dtype)],
                            axis=0).astype(BF16)
    tm = 512
    rope_specs = [(rc, (tm, LANES), _row_i), (rsa, (tm, LANES), _row_i), (rsb, (tm, LANES), _row_i)]
    qn, kvn, kr = _matmul(
        "latent_proj", [h16], [(0, w_lat, "nk", 0, lat_w)], _latent_epilogue,
        [((n, Q_LORA), BF16, (tm, Q_LORA), _row_i),
         ((n, KV_LORA), BF16, (tm, KV_LORA), _row_i),
         ((n, LANES), BF16, (tm, LANES), _row_i)],
        tm=tm, tn=lat_w,
        extras=[(q_norm_g[l].reshape(1, -1), (1, Q_LORA), lambda i, j: (0, 0)),
                (kv_norm_g[l].reshape(1, -1), (1, KV_LORA), lambda i, j: (0, 0))] + rope_specs)

    scale = (NOPE + ROPE) ** -0.5
    q_w = MLA_HEADS * HEAD_W
    (qf,) = _matmul("q_up_proj", [qn], [(0, _uq_weight(w_uq[l]), "kn", 0, q_w)],
                    functools.partial(_uq_epilogue, scale=scale),
                    [((n, q_w), BF16, (tm, 1024), _tile_ij)],
                    tm=tm, tn=1024, extras=rope_specs)
    kv_w = MLA_HEADS * (NOPE + V_DIM)
    (kv,) = _matmul("kv_up_proj", [kvn], [(0, _ukv_weight(w_ukv[l]), "kn", 0, kv_w)], _ep_ident,
                    [((n, kv_w), BF16, (tm, 1024), _tile_ij)], tm=tm, tn=1024)
    o_a = _attention(qf, kv, kr, batch, seq, MLA_HEADS)

    tm, tn = 1024, 1024

    def proj(name, col0, ep, dtype, extras=()):
        (r,) = _matmul(name, [h16], [(0, wt, "nk", col0, HG_WIDTH)], ep,
                       [((n, HG_WIDTH), dtype, (tm, tn), _tile_ij)],
                       tm=tm, tn=tn, extras=list(extras))
        return r

    qs = proj("hg_q_proj", o_hq, _ep_silu, BF16)
    fg = proj("hg_f_proj", o_hf, _ep_forget, F32, [(hg_lb, (hg_lb.shape[0], tn), _col_j)])
    iv = proj("hg_i_proj", o_hi, _ep_ident, BF16)
    gs = proj("hg_g_proj", o_hg, _ep_silu, BF16)
    o_b = _hgrn2(qs, fg, iv, gs, hg_norm_g[l], batch, seq, HG_HEADS)

    tm, tn = 512, 512
    (merged,) = _matmul(
        "gated_merge", [h16, o_a, o_b],
        [(0, wt, "nk", o_ga, d), (1, w_branch_a[l], "kn", 0, d),
         (0, wt, "nk", o_gb, d), (2, w_branch_b[l], "kn", 0, d)],
        _ep_merge, [((n, d), BF16, (tm, tn), _tile_ij)], tm=tm, tn=tn)

    tm, tn = 1024, 1024
    (y1,) = _matmul("out_proj", [merged], [(0, w_out[l], "kn", 0, d)], _ep_residual,
                    [((n, d), F32, (tm, tn), _tile_ij)], tm=tm, tn=tn,
                    extras=[(h32, (tm, tn), _tile_ij)])
    h1_32, h1_16 = _layer_norm("ln1", y1, ln1_g[l], ln1_b[l], (F32, BF16))

    tm, tn = 2048, 256
    (act,) = _matmul("swiglu_up", [h1_16], [(0, w_gate[l], "kn", 0, D_FF), (0, w_up[l], "kn", 0, D_FF)],
                     _ep_swiglu, [((n, D_FF), BF16, (tm, tn), _tile_ij)], tm=tm, tn=tn)
    tm, tn = 512, 512
    (y2,) = _matmul("swiglu_down", [act], [(0, w_down[l], "kn", 0, d)], _ep_residual,
                    [((n, d), F32, (tm, tn), _tile_ij)], tm=tm, tn=tn,
                    extras=[(h1_32, (tm, tn), _tile_ij)])
    (out,) = _layer_norm("ln2", y2, ln2_g[l], ln2_b[l], (F32,))
    return out.reshape(batch, seq, d)
```

```python
import functools

import jax
import jax.numpy as jnp
from jax import lax
from jax.experimental import pallas as pl
from jax.experimental.pallas import tpu as pltpu

F32 = jnp.float32
BF16 = jnp.bfloat16

D_MODEL = 4096
CHUNK = 64
MLA_HEADS = 16
NOPE = 128
ROPE = 64
V_DIM = 128
Q_LORA = 1024
KV_LORA = 512
ROPE_THETA = 10000.0
HG_HEADS = 16
HG_DIM = 128
HG_WIDTH = HG_HEADS * HG_DIM
D_FF = 11008
DEPTH = 1
ALPHA = (2 * DEPTH) ** 0.25
EPS = 1e-5
LOG2_E = 1.4426950408889634

LANES = 128
SUBLANES = 8
VMEM_LIMIT = 60 * 1024 * 1024

HEAD_W = 2 * LANES
SUB = 16
HG_CHUNK = 128
ATT_TQ = 256


def _cparams(n_grid, vmem=VMEM_LIMIT):
    return pltpu.CompilerParams(dimension_semantics=("arbitrary",) * n_grid,
                                vmem_limit_bytes=vmem)


def _weight_chunk_copy(w_hbm, stage, sem, spec, jj, ci):
    kind, (col0, tn, cr) = spec
    if kind == "kn":
        src = w_hbm.at[pl.ds(pl.multiple_of(ci * cr, SUBLANES), cr),
                       pl.ds(pl.multiple_of(col0 + jj * tn, LANES), tn)]
    else:
        src = w_hbm.at[pl.ds(pl.multiple_of(col0 + jj * tn + ci * cr, SUBLANES), cr), :]
    return pltpu.make_async_copy(src, stage, sem)


def _mm_kernel(*refs, pair_x, w_specs, n_x, n_extra, n_out, epilogue):
    n_pairs = len(pair_x)
    xs = refs[:n_x]
    w_refs = refs[n_x:n_x + n_pairs]
    extras = refs[n_x + n_pairs:n_x + n_pairs + n_extra]
    outs = refs[n_x + n_pairs + n_extra:n_x + n_pairs + n_extra + n_out]
    scratch = list(refs[n_x + n_pairs + n_extra + n_out:])
    j, i = pl.program_id(0), pl.program_id(1)
    nj, ni = pl.num_programs(0), pl.num_programs(1)
    slot = j % 2

    streamed = []
    tiles = []
    for wr, spec in zip(w_refs, w_specs):
        if spec[1] is None:
            tiles.append(wr)
        else:
            tile16, stage, sem = scratch[:3]
            del scratch[:3]
            streamed.append((wr, spec, tile16, stage, sem))
            tiles.append(tile16.at[slot])

    def chunk_rows(spec, ci):
        cr = spec[1][2]
        return pl.ds(pl.multiple_of(ci * cr, SUBLANES), cr)

    @pl.when((j == 0) & (i == 0))
    def _():
        for wr, spec, tile16, stage, sem in streamed:
            def load(ci, carry):
                cp = _weight_chunk_copy(wr, stage, sem, spec, 0, ci)
                cp.start()
                cp.wait()
                tile16[0, chunk_rows(spec, ci), :] = stage[...].astype(BF16)
                return carry
            lax.fori_loop(0, ni, load, 0)

            @pl.when(nj > 1)
            def _():
                _weight_chunk_copy(wr, stage, sem, spec, 1, 0).start()

    accs = []
    for xi, w, spec in zip(pair_x, tiles, w_specs):
        if spec[0] == "kn":
            accs.append(jnp.dot(xs[xi][...], w[...], preferred_element_type=F32))
        else:
            accs.append(lax.dot_general(xs[xi][...], w[...], (((1,), (1,)), ((), ())),
                                        preferred_element_type=F32))
    res = epilogue(accs, [e[...] for e in extras])
    for o, r in zip(outs, res):
        o[...] = r.astype(o.dtype)

    for wr, spec, tile16, stage, sem in streamed:
        @pl.when(j + 1 < nj)
        def _():
            _weight_chunk_copy(wr, stage, sem, spec, j + 1, i).wait()
            tile16[1 - slot, chunk_rows(spec, i), :] = stage[...].astype(BF16)

        @pl.when((j + 1 < nj) & (i + 1 < ni))
        def _():
            _weight_chunk_copy(wr, stage, sem, spec, j + 1, i + 1).start()

        @pl.when((i + 1 == ni) & (j + 2 < nj))
        def _():
            _weight_chunk_copy(wr, stage, sem, spec, j + 2, 0).start()


def _matmul(name, xs, pairs, epilogue, outs, *, tm, tn, extras=()):
    m = xs[0].shape[0]
    n = pairs[0][4]
    assert m % tm == 0 and n % tn == 0
    nj, ni = n // tn, m // tm
    grid = (nj, ni)

    in_specs = [pl.BlockSpec((tm, x.shape[1]), lambda j, i: (i, 0)) for x in xs]
    scratch, w_specs = [], []
    for _, w, kind, col0, ncols in pairs:
        assert ncols == n
        if w.dtype == BF16:
            assert col0 % tn == 0
            if kind == "kn":
                in_specs.append(pl.BlockSpec((w.shape[0], tn),
                                             lambda j, i, c0=col0 // tn: (0, c0 + j)))
            else:
                in_specs.append(pl.BlockSpec((tn, w.shape[1]),
                                             lambda j, i, c0=col0 // tn: (c0 + j, 0)))
            w_specs.append((kind, None))
            continue
        rows = w.shape[0] if kind == "kn" else tn
        lanes = tn if kind == "kn" else w.shape[1]
        assert rows % (ni * SUBLANES) == 0 and col0 % SUBLANES == 0
        if kind == "kn":
            assert col0 % LANES == 0 and tn % LANES == 0
        cr = rows // ni
        in_specs.append(pl.BlockSpec(memory_space=pl.ANY))
        scratch += [pltpu.VMEM((2, rows, lanes), BF16), pltpu.VMEM((cr, lanes), F32),
                    pltpu.SemaphoreType.DMA(())]
        w_specs.append((kind, (col0, tn, cr)))
    for _, bs, im in extras:
        in_specs.append(pl.BlockSpec(bs, functools.partial(_swap_ji, im)))
    out_specs = [pl.BlockSpec(bs, functools.partial(_swap_ji, im)) for _, _, bs, im in outs]
    out_shape = [jax.ShapeDtypeStruct(s, d) for s, d, _, _ in outs]

    kern = functools.partial(_mm_kernel, pair_x=tuple(p[0] for p in pairs),
                             w_specs=tuple(w_specs), n_x=len(xs), n_extra=len(extras),
                             n_out=len(outs), epilogue=epilogue)
    return pl.pallas_call(
        kern, grid=grid, in_specs=in_specs, out_specs=out_specs, out_shape=out_shape,
        scratch_shapes=scratch, compiler_params=_cparams(2), name=name,
    )(*xs, *[p[1] for p in pairs], *[e[0] for e in extras])


def _swap_ji(im, j, i):
    return im(i, j)


def _tile_ij(i, j):
    return (i, j)


def _row_i(i, j):
    return (i, 0)


def _col_j(i, j):
    return (0, j)


def _sigmoid(x):
    return 1.0 / (1.0 + jnp.exp(-x))


def _silu(x):
    return x * _sigmoid(x)


def _ln_kernel(y_ref, g_ref, b_ref, *outs):
    y = y_ref[...]
    mu = jnp.mean(y, axis=-1, keepdims=True)
    d = y - mu
    var = jnp.mean(d * d, axis=-1, keepdims=True)
    r = d * lax.rsqrt(var + EPS) * g_ref[...] + b_ref[...]
    for o in outs:
        o[...] = r.astype(o.dtype)


def _layer_norm(name, y, g, b, out_dtypes, tm=256):
    m, d = y.shape
    return pl.pallas_call(
        _ln_kernel, grid=(m // tm,), name=name,
        in_specs=[pl.BlockSpec((tm, d), lambda i: (i, 0)),
                  pl.BlockSpec((1, d), lambda i: (0, 0)),
                  pl.BlockSpec((1, d), lambda i: (0, 0))],
        out_specs=[pl.BlockSpec((tm, d), lambda i: (i, 0)) for _ in out_dtypes],
        out_shape=[jax.ShapeDtypeStruct((m, d), dt) for dt in out_dtypes],
        compiler_params=_cparams(1),
    )(y, g.reshape(1, d), b.reshape(1, d))


def _rope_table_kernel(pos_ref, invf_ref, c_ref, sa_ref, sb_ref):
    ang = pos_ref[...].astype(F32) * invf_ref[...]
    lane = lax.broadcasted_iota(jnp.int32, ang.shape, 1)
    cos = jnp.cos(ang)
    sin = jnp.sin(ang)
    half = ROPE // 2
    c_ref[...] = cos
    sa_ref[...] = jnp.where(lane < half, -sin, 0.0)
    sb_ref[...] = jnp.where((lane >= half) & (lane < ROPE), sin, 0.0)


def _rope_tables(positions, tm=1024):
    n = positions.size
    half = ROPE // 2
    inv_freq = ROPE_THETA ** (-jnp.arange(half, dtype=F32) / half)
    invf = jnp.concatenate([inv_freq, inv_freq, jnp.zeros((LANES - ROPE,), F32)]).reshape(1, LANES)
    pos = positions.reshape(n, 1)
    return pl.pallas_call(
        _rope_table_kernel, grid=(n // tm,), name="rope_tables",
        in_specs=[pl.BlockSpec((tm, 1), lambda i: (i, 0)),
                  pl.BlockSpec((1, LANES), lambda i: (0, 0))],
        out_specs=[pl.BlockSpec((tm, LANES), lambda i: (i, 0))] * 3,
        out_shape=[jax.ShapeDtypeStruct((n, LANES), F32)] * 3,
        compiler_params=_cparams(1),
    )(pos, invf)


def _rope_lanes(x, c, sa, sb):
    half = ROPE // 2
    return (x * c + pltpu.roll(x, LANES - half, axis=1) * sa
            + pltpu.roll(x, half, axis=1) * sb)


def _rms(x, g):
    ms = jnp.mean(x * x, axis=-1, keepdims=True)
    return x * lax.rsqrt(ms + EPS) * g


def _latent_epilogue(accs, extras):
    acc = accs[0]
    gq, gkv, c, sa, sb = extras
    qn = _rms(acc[:, :Q_LORA], gq)
    kvn = _rms(acc[:, Q_LORA:Q_LORA + KV_LORA], gkv)
    kr = _rope_lanes(acc[:, Q_LORA + KV_LORA:], c, sa, sb)
    return [qn, kvn, kr]


def _uq_epilogue(accs, extras, *, scale):
    acc = accs[0]
    c, sa, sb = extras
    tn = acc.shape[1]
    cols = []
    for h in range(tn // HEAD_W):
        base = h * HEAD_W
        cols.append(acc[:, base:base + NOPE] * scale)
        cols.append(_rope_lanes(acc[:, base + NOPE:base + HEAD_W], c, sa, sb) * scale)
    return [jnp.concatenate(cols, axis=1)]


def _attn_kernel(q_ref, kn_ref, v_ref, kr_ref, o_ref, kcat_ref, *, tq):
    t = q_ref.shape[0]
    kcat_ref[:, :NOPE] = kn_ref[...]
    kcat_ref[:, NOPE:] = kr_ref[...]
    row_chunk = lax.broadcasted_iota(jnp.int32, (tq, tq), 0) // CHUNK
    col_chunk = lax.broadcasted_iota(jnp.int32, (tq, tq), 1) // CHUNK
    visible = col_chunk <= row_chunk
    nt = (((1,), (1,)), ((), ()))
    for jq in range(t // tq):
        s0, s1 = jq * tq, (jq + 1) * tq
        q = q_ref[s0:s1, :]
        sd = lax.dot_general(q, kcat_ref[s0:s1, :], nt, preferred_element_type=F32)
        sd = jnp.where(visible, sd, -jnp.inf)
        m = jnp.max(sd, axis=-1, keepdims=True)
        if jq > 0:
            sp = lax.dot_general(q, kcat_ref[0:s0, :], nt, preferred_element_type=F32)
            m = jnp.maximum(m, jnp.max(sp, axis=-1, keepdims=True))
        pd = jnp.exp(sd - m)
        l = jnp.sum(pd, axis=-1, keepdims=True)
        o = jnp.dot(pd.astype(BF16), v_ref[s0:s1, :], preferred_element_type=F32)
        if jq > 0:
            pp = jnp.exp(sp - m)
            l = l + jnp.sum(pp, axis=-1, keepdims=True)
            o = o + jnp.dot(pp.astype(BF16), v_ref[0:s0, :], preferred_element_type=F32)
        o_ref[s0:s1, :] = (o / l).astype(o_ref.dtype)


def _attention(qf, kv, kr, batch, seq, heads):
    n = batch * seq
    kern = functools.partial(_attn_kernel, tq=ATT_TQ)
    return pl.pallas_call(
        kern, grid=(batch, heads), name="mla_attention",
        in_specs=[pl.BlockSpec((seq, HEAD_W), lambda b, h: (b, h)),
                  pl.BlockSpec((seq, NOPE), lambda b, h: (b, h)),
                  pl.BlockSpec((seq, V_DIM), lambda b, h: (b, heads + h)),
                  pl.BlockSpec((seq, LANES), lambda b, h: (b, 0))],
        out_specs=pl.BlockSpec((seq, V_DIM), lambda b, h: (b, h)),
        out_shape=jax.ShapeDtypeStruct((n, heads * V_DIM), BF16),
        scratch_shapes=[pltpu.VMEM((seq, HEAD_W), BF16)],
        compiler_params=_cparams(2),
    )(qf, kv, kv, kr)


def _split3(x):
    a = x.astype(BF16)
    r = x - a.astype(F32)
    b = r.astype(BF16)
    c = (r - b.astype(F32)).astype(BF16)
    return a, b, c


def _chunk_rows(n):
    r0 = n * HG_CHUNK
    return pl.ds(r0 if isinstance(n, int) else pl.multiple_of(r0, HG_CHUNK), HG_CHUNK)


def _hgrn_decay_stage(n, hh, refs, masks):
    d = HG_DIM
    lanes = slice(hh * d, (hh + 1) * d)
    f = refs["f"][_chunk_rows(n), lanes]
    g = jnp.log(f) * LOG2_E
    cs = jnp.dot(masks[0], jnp.concatenate(_split3(g), axis=1), preferred_element_type=F32)
    b = cs[:, :d] + cs[:, d:2 * d] + cs[:, 2 * d:]
    refs["b"][n % 2, hh] = b
    refs["c"][n % 2, hh] = b - jnp.log(1.0 - f) * LOG2_E


def _hgrn_score_stage(n, hh, refs, masks):
    c, d, n_sub = HG_CHUNK, HG_DIM, HG_CHUNK // SUB
    _, m_diag, m_l16, m_l32 = masks
    lanes = slice(hh * d, (hh + 1) * d)
    rows = _chunk_rows(n)
    slot = n % 2
    nt = (((1,), (1,)), ((), ()))

    q = refs["q"][rows, lanes].astype(F32)
    k = 1.0 - refs["f"][rows, lanes]
    b = refs["b"][slot, hh]
    b_last = b[c - 1:c, :]

    q3 = q.reshape(n_sub, SUB, d)
    b3 = b.reshape(n_sub, SUB, d)
    c3 = refs["c"][slot, hh].reshape(n_sub, SUB, d)
    half = SUB // 2
    a_ref = refs["a"]
    for s in range(SUB):
        lo = 0 if s < half else half
        a = q3[:, lo:, :] * jnp.exp2(jnp.minimum(b3[:, lo:, :] - c3[:, s:s + 1, :], 0.0))
        if lo:
            a = jnp.concatenate([jnp.zeros((n_sub, lo, d), F32), a], axis=1)
        a_ref[hh, :, s * d:(s + 1) * d] = a.reshape(c, d).astype(BF16)
    p_diag = jnp.dot(a_ref[hh], refs["sel"][...], preferred_element_type=F32)

    def level(hs):
        blk = 2 * hs
        qt, kt = [], []
        for i in range(0, c, hs):
            edge = (i // blk) * blk + hs - 1
            e = jnp.exp2(-jnp.abs(b[i:i + hs, :] - b[edge:edge + 1, :]))
            zero = jnp.zeros((hs, d), F32)
            upper = (i % blk) >= hs
            qt.append(q[i:i + hs, :] * e if upper else zero)
            kt.append(zero if upper else k[i:i + hs, :] * e)
        qt = jnp.concatenate(qt, axis=0).astype(BF16)
        kt = jnp.concatenate(kt, axis=0).astype(BF16)
        return lax.dot_general(qt, kt, nt, preferred_element_type=F32)

    p = jnp.where(m_diag, p_diag,
                  jnp.where(m_l16, level(SUB), jnp.where(m_l32, level(2 * SUB), level(4 * SUB))))
    refs["p"][slot, hh] = p.astype(BF16)
    refs["q0"][slot, hh] = (q * jnp.exp2(b)).astype(BF16)
    refs["kl"][slot, hh] = (k * jnp.exp2(b_last - b)).astype(BF16)
    refs["dl"][slot, hh] = jnp.broadcast_to(jnp.exp2(b_last), (SUBLANES, d))


def _hgrn_output_stage(n, hh, refs):
    d = HG_DIM
    lanes = slice(hh * d, (hh + 1) * d)
    rows = _chunk_rows(n)
    slot = n % 2
    nt = (((1,), (1,)), ((), ()))
    v = refs["i"][rows, lanes]
    st = refs["st"][hh]
    o = (jnp.dot(refs["p"][slot, hh], v, preferred_element_type=F32)
         + lax.dot_general(refs["q0"][slot, hh], st.astype(BF16), nt, preferred_element_type=F32))
    vt = v.astype(F32).T.astype(BF16)
    refs["st"][hh] = (refs["dl"][slot, hh][0:1, :] * st
                      + jnp.dot(vt, refs["kl"][slot, hh], preferred_element_type=F32))
    ms = jnp.mean(o * o, axis=-1, keepdims=True)
    ob = o * lax.rsqrt(ms + EPS) * refs["gn"][...] * refs["gs"][rows, lanes].astype(F32)
    refs["o"][rows, lanes] = ob.astype(refs["o"].dtype)


def _hgrn_kernel(q_ref, f_ref, i_ref, gs_ref, gn_ref, sel_ref, o_ref, st_ref, a_ref, b_ref,
                 c_ref, p_ref, q0_ref, kl_ref, dl_ref, *, hpb):
    nc = q_ref.shape[0] // HG_CHUNK
    assert nc >= 3
    c, d = HG_CHUNK, HG_DIM
    row = lax.broadcasted_iota(jnp.int32, (c, d), 0)
    col = lax.broadcasted_iota(jnp.int32, (c, d), 1)
    tri = (col <= row).astype(BF16)
    m_diag = ((row // SUB) == (col // SUB)) & ((col % SUB) <= (row % SUB))
    m_l16 = (row // (2 * SUB)) == (col // (2 * SUB))
    m_l32 = (row // (4 * SUB)) == (col // (4 * SUB))
    masks = (tri, m_diag, m_l16, m_l32)
    refs = dict(q=q_ref, f=f_ref, i=i_ref, gs=gs_ref, gn=gn_ref, sel=sel_ref, o=o_ref,
                st=st_ref, a=a_ref, b=b_ref, c=c_ref, p=p_ref, q0=q0_ref, kl=kl_ref, dl=dl_ref)

    st_ref[...] = jnp.zeros_like(st_ref)

    def run(n_out, n_score, n_decay):
        for hh in range(hpb):
            if n_out is not None:
                _hgrn_output_stage(n_out, hh, refs)
            if n_score is not None:
                _hgrn_score_stage(n_score, hh, refs, masks)
            if n_decay is not None:
                _hgrn_decay_stage(n_decay, hh, refs, masks)

    run(None, None, 0)
    run(None, 0, 1)

    def steady(n, carry):
        run(n - 2, n - 1, n)
        return carry

    lax.fori_loop(2, nc, steady, 0)
    run(nc - 2, nc - 1, None)
    run(nc - 1, None, None)


def _hgrn2(qs, f, iv, gs, gnorm, batch, seq, heads, hpb=2):
    n = batch * seq
    d = HG_DIM
    sel = (jnp.arange(SUB * d)[:, None] // d == jnp.arange(d)[None, :] % SUB).astype(BF16)
    spec = pl.BlockSpec((seq, hpb * d), lambda b, h: (b, h))
    return pl.pallas_call(
        functools.partial(_hgrn_kernel, hpb=hpb), grid=(batch, heads // hpb), name="hgrn2",
        in_specs=[spec, spec, spec, spec, pl.BlockSpec((1, d), lambda b, h: (0, 0)),
                  pl.BlockSpec((SUB * d, d), lambda b, h: (0, 0))],
        out_specs=spec,
        out_shape=jax.ShapeDtypeStruct((n, heads * d), BF16),
        scratch_shapes=[pltpu.VMEM((hpb, d, d), F32),
                        pltpu.VMEM((hpb, HG_CHUNK, SUB * d), BF16),
                        pltpu.VMEM((2, hpb, HG_CHUNK, d), F32),
                        pltpu.VMEM((2, hpb, HG_CHUNK, d), F32),
                        pltpu.VMEM((2, hpb, HG_CHUNK, d), BF16),
                        pltpu.VMEM((2, hpb, HG_CHUNK, d), BF16),
                        pltpu.VMEM((2, hpb, HG_CHUNK, d), BF16),
                        pltpu.VMEM((2, hpb, SUBLANES, d), F32)],
        compiler_params=_cparams(2),
    )(qs, f, iv, gs, gnorm.reshape(1, d), sel)


def _ep_silu(accs, extras):
    return [_silu(accs[0])]


def _ep_ident(accs, extras):
    return [accs[0]]


def _ep_forget(accs, extras):
    lbp = extras[0]
    m = jnp.max(lbp, axis=0, keepdims=True)
    e = jnp.exp(lbp - m)
    lb = e[0:1, :] / jnp.sum(e, axis=0, keepdims=True)
    return [lb + (1.0 - lb) * _sigmoid(accs[0])]


def _ep_merge(accs, extras):
    ga, a, gb, b = accs
    return [_sigmoid(ga) * a + _sigmoid(gb) * b]


def _ep_residual(accs, extras):
    return [ALPHA * extras[0] + accs[0]]


def _ep_swiglu(accs, extras):
    return [_silu(accs[0]) * accs[1]]


def _uq_weight(w_uq):
    r = w_uq.shape[0]
    w = w_uq.reshape(r, MLA_HEADS, NOPE + ROPE)
    w = jnp.concatenate([w, jnp.zeros((r, MLA_HEADS, HEAD_W - NOPE - ROPE), w.dtype)], axis=-1)
    return w.reshape(r, MLA_HEADS * HEAD_W).astype(BF16)


def _ukv_weight(w_ukv):
    r = w_ukv.shape[0]
    w = w_ukv.reshape(r, MLA_HEADS, NOPE + V_DIM)
    return jnp.concatenate([w[:, :, :NOPE].reshape(r, -1), w[:, :, NOPE:].reshape(r, -1)],
                           axis=1).astype(BF16)


def kernel(x, positions, ln_in_g, ln_in_b, w_in, q_norm_g, w_uq, kv_norm_g, w_ukv, hg_lb,
           hg_norm_g, w_branch_a, w_branch_b, w_out, ln1_g, ln1_b, w_gate, w_up, w_down,
           ln2_g, ln2_b):
    batch, seq, d = x.shape
    n = batch * seq
    assert w_in.shape[0] == DEPTH == 1
    l = 0
    wt = w_in.reshape(d, w_in.shape[-1]).T

    o_kr = Q_LORA + KV_LORA
    o_hq = o_kr + ROPE
    o_hf, o_hi, o_hg = o_hq + HG_WIDTH, o_hq + 2 * HG_WIDTH, o_hq + 3 * HG_WIDTH
    o_ga = o_hq + 4 * HG_WIDTH
    o_gb = o_ga + D_MODEL

    h32, h16 = _layer_norm("ln_in", x.reshape(n, d), ln_in_g, ln_in_b, (F32, BF16))
    rc, rsa, rsb = _rope_tables(positions)

    lat_w = o_kr + LANES
    w_lat = jnp.concatenate([wt[:o_hq], jnp.zeros((LANES - ROPE, d), wt.dtype)],
                            axis=0).astype(BF16)
    tm = 512
    rope_specs = [(rc, (tm, LANES), _row_i), (rsa, (tm, LANES), _row_i), (rsb, (tm, LANES), _row_i)]
    qn, kvn, kr = _matmul(
        "latent_proj", [h16], [(0, w_lat, "nk", 0, lat_w)], _latent_epilogue,
        [((n, Q_LORA), BF16, (tm, Q_LORA), _row_i),
         ((n, KV_LORA), BF16, (tm, KV_LORA), _row_i),
         ((n, LANES), BF16, (tm, LANES), _row_i)],
        tm=tm, tn=lat_w,
        extras=[(q_norm_g[l].reshape(1, -1), (1, Q_LORA), lambda i, j: (0, 0)),
                (kv_norm_g[l].reshape(1, -1), (1, KV_LORA), lambda i, j: (0, 0))] + rope_specs)

    scale = (NOPE + ROPE) ** -0.5
    q_w = MLA_HEADS * HEAD_W
    (qf,) = _matmul("q_up_proj", [qn], [(0, _uq_weight(w_uq[l]), "kn", 0, q_w)],
                    functools.partial(_uq_epilogue, scale=scale),
                    [((n, q_w), BF16, (tm, 1024), _tile_ij)],
                    tm=tm, tn=1024, extras=rope_specs)
    kv_w = MLA_HEADS * (NOPE + V_DIM)
    (kv,) = _matmul("kv_up_proj", [kvn], [(0, _ukv_weight(w_ukv[l]), "kn", 0, kv_w)], _ep_ident,
                    [((n, kv_w), BF16, (tm, 1024), _tile_ij)], tm=tm, tn=1024)
    o_a = _attention(qf, kv, kr, batch, seq, MLA_HEADS)

    tm, tn = 1024, 1024

    def proj(name, col0, ep, dtype, extras=()):
        (r,) = _matmul(name, [h16], [(0, wt, "nk", col0, HG_WIDTH)], ep,
                       [((n, HG_WIDTH), dtype, (tm, tn), _tile_ij)],
                       tm=tm, tn=tn, extras=list(extras))
        return r

    qs = proj("hg_q_proj", o_hq, _ep_silu, BF16)
    fg = proj("hg_f_proj", o_hf, _ep_forget, F32, [(hg_lb, (hg_lb.shape[0], tn), _col_j)])
    iv = proj("hg_i_proj", o_hi, _ep_ident, BF16)
    gs = proj("hg_g_proj", o_hg, _ep_silu, BF16)
    o_b = _hgrn2(qs, fg, iv, gs, hg_norm_g[l], batch, seq, HG_HEADS)

    tm, tn = 512, 512
    (merged,) = _matmul(
        "gated_merge", [h16, o_a, o_b],
        [(0, wt, "nk", o_ga, d), (1, w_branch_a[l], "kn", 0, d),
         (0, wt, "nk", o_gb, d), (2, w_branch_b[l], "kn", 0, d)],
        _ep_merge, [((n, d), BF16, (tm, tn), _tile_ij)], tm=tm, tn=tn)

    tm, tn = 1024, 1024
    (y1,) = _matmul("out_proj", [merged], [(0, w_out[l], "kn", 0, d)], _ep_residual,
                    [((n, d), F32, (tm, tn), _tile_ij)], tm=tm, tn=tn,
                    extras=[(h32, (tm, tn), _tile_ij)])
    h1_32, h1_16 = _layer_norm("ln1", y1, ln1_g[l], ln1_b[l], (F32, BF16))

    tm, tn = 2048, 256
    (act,) = _matmul("swiglu_up", [h1_16], [(0, w_gate[l], "kn", 0, D_FF), (0, w_up[l], "kn", 0, D_FF)],
                     _ep_swiglu, [((n, D_FF), BF16, (tm, tn), _tile_ij)], tm=tm, tn=tn)
    tm, tn = 512, 512
    (y2,) = _matmul("swiglu_down", [act], [(0, w_down[l], "kn", 0, d)], _ep_residual,
                    [((n, d), F32, (tm, tn), _tile_ij)], tm=tm, tn=tn,
                    extras=[(h1_32, (tm, tn), _tile_ij)])
    (out,) = _layer_norm("ln2", y2, ln2_g[l], ln2_b[l], (F32,))
    return out.reshape(batch, seq, d)
```

```python
import functools

import jax
import jax.numpy as jnp
from jax import lax
from jax.experimental import pallas as pl
from jax.experimental.pallas import tpu as pltpu

F32 = jnp.float32
BF16 = jnp.bfloat16

D_MODEL = 4096
CHUNK = 64
MLA_HEADS = 16
NOPE = 128
ROPE = 64
V_DIM = 128
Q_LORA = 1024
KV_LORA = 512
ROPE_THETA = 10000.0
HG_HEADS = 16
HG_DIM = 128
HG_WIDTH = HG_HEADS * HG_DIM
D_FF = 11008
DEPTH = 1
ALPHA = (2 * DEPTH) ** 0.25
EPS = 1e-5
LOG2_E = 1.4426950408889634

LANES = 128
SUBLANES = 8
VMEM_LIMIT = 60 * 1024 * 1024

HEAD_W = 2 * LANES
SUB = 16
HG_CHUNK = 128
ATT_TQ = 256


def _cparams(n_grid, vmem=VMEM_LIMIT):
    return pltpu.CompilerParams(dimension_semantics=("arbitrary",) * n_grid,
                                vmem_limit_bytes=vmem)


def _weight_chunk_copy(w_hbm, stage, sem, spec, jj, ci):
    kind, (col0, tn, cr) = spec
    if kind == "kn":
        src = w_hbm.at[pl.ds(pl.multiple_of(ci * cr, SUBLANES), cr),
                       pl.ds(pl.multiple_of(col0 + jj * tn, LANES), tn)]
    else:
        src = w_hbm.at[pl.ds(pl.multiple_of(col0 + jj * tn + ci * cr, SUBLANES), cr), :]
    return pltpu.make_async_copy(src, stage, sem)


def _mm_kernel(*refs, pair_x, w_specs, n_x, n_extra, n_out, epilogue):
    n_pairs = len(pair_x)
    xs = refs[:n_x]
    w_refs = refs[n_x:n_x + n_pairs]
    extras = refs[n_x + n_pairs:n_x + n_pairs + n_extra]
    outs = refs[n_x + n_pairs + n_extra:n_x + n_pairs + n_extra + n_out]
    scratch = list(refs[n_x + n_pairs + n_extra + n_out:])
    j, i = pl.program_id(0), pl.program_id(1)
    nj, ni = pl.num_programs(0), pl.num_programs(1)
    slot = j % 2

    streamed = []
    tiles = []
    for wr, spec in zip(w_refs, w_specs):
        if spec[1] is None:
            tiles.append(wr)
        else:
            tile16, stage, sem = scratch[:3]
            del scratch[:3]
            streamed.append((wr, spec, tile16, stage, sem))
            tiles.append(tile16.at[slot])

    def chunk_rows(spec, ci):
        cr = spec[1][2]
        return pl.ds(pl.multiple_of(ci * cr, SUBLANES), cr)

    @pl.when((j == 0) & (i == 0))
    def _():
        for wr, spec, tile16, stage, sem in streamed:
            def load(ci, carry):
                cp = _weight_chunk_copy(wr, stage, sem, spec, 0, ci)
                cp.start()
                cp.wait()
                tile16[0, chunk_rows(spec, ci), :] = stage[...].astype(BF16)
                return carry
            lax.fori_loop(0, ni, load, 0)

            @pl.when(nj > 1)
            def _():
                _weight_chunk_copy(wr, stage, sem, spec, 1, 0).start()

    accs = []
    for xi, w, spec in zip(pair_x, tiles, w_specs):
        if spec[0] == "kn":
            accs.append(jnp.dot(xs[xi][...], w[...], preferred_element_type=F32))
        else:
            accs.append(lax.dot_general(xs[xi][...], w[...], (((1,), (1,)), ((), ())),
                                        preferred_element_type=F32))
    res = epilogue(accs, [e[...] for e in extras])
    for o, r in zip(outs, res):
        o[...] = r.astype(o.dtype)

    for wr, spec, tile16, stage, sem in streamed:
        @pl.when(j + 1 < nj)
        def _():
            _weight_chunk_copy(wr, stage, sem, spec, j + 1, i).wait()
            tile16[1 - slot, chunk_rows(spec, i), :] = stage[...].astype(BF16)

        @pl.when((j + 1 < nj) & (i + 1 < ni))
        def _():
            _weight_chunk_copy(wr, stage, sem, spec, j + 1, i + 1).start()

        @pl.when((i + 1 == ni) & (j + 2 < nj))
        def _():
            _weight_chunk_copy(wr, stage, sem, spec, j + 2, 0).start()


def _matmul(name, xs, pairs, epilogue, outs, *, tm, tn, extras=()):
    m = xs[0].shape[0]
    n = pairs[0][4]
    assert m % tm == 0 and n % tn == 0
    nj, ni = n // tn, m // tm
    grid = (nj, ni)

    in_specs = [pl.BlockSpec((tm, x.shape[1]), lambda j, i: (i, 0)) for x in xs]
    scratch, w_specs = [], []
    for _, w, kind, col0, ncols in pairs:
        assert ncols == n
        if w.dtype == BF16:
            assert col0 % tn == 0
            if kind == "kn":
                in_specs.append(pl.BlockSpec((w.shape[0], tn),
                                             lambda j, i, c0=col0 // tn: (0, c0 + j)))
            else:
                in_specs.append(pl.BlockSpec((tn, w.shape[1]),
                                             lambda j, i, c0=col0 // tn: (c0 + j, 0)))
            w_specs.append((kind, None))
            continue
        rows = w.shape[0] if kind == "kn" else tn
        lanes = tn if kind == "kn" else w.shape[1]
        assert rows % (ni * SUBLANES) == 0 and col0 % SUBLANES == 0
        if kind == "kn":
            assert col0 % LANES == 0 and tn % LANES == 0
        cr = rows // ni
        in_specs.append(pl.BlockSpec(memory_space=pl.ANY))
        scratch += [pltpu.VMEM((2, rows, lanes), BF16), pltpu.VMEM((cr, lanes), F32),
                    pltpu.SemaphoreType.DMA(())]
        w_specs.append((kind, (col0, tn, cr)))
    for _, bs, im in extras:
        in_specs.append(pl.BlockSpec(bs, functools.partial(_swap_ji, im)))
    out_specs = [pl.BlockSpec(bs, functools.partial(_swap_ji, im)) for _, _, bs, im in outs]
    out_shape = [jax.ShapeDtypeStruct(s, d) for s, d, _, _ in outs]

    kern = functools.partial(_mm_kernel, pair_x=tuple(p[0] for p in pairs),
                             w_specs=tuple(w_specs), n_x=len(xs), n_extra=len(extras),
                             n_out=len(outs), epilogue=epilogue)
    return pl.pallas_call(
        kern, grid=grid, in_specs=in_specs, out_specs=out_specs, out_shape=out_shape,
        scratch_shapes=scratch, compiler_params=_cparams(2), name=name,
    )(*xs, *[p[1] for p in pairs], *[e[0] for e in extras])


def _swap_ji(im, j, i):
    return im(i, j)


def _tile_ij(i, j):
    return (i, j)


def _row_i(i, j):
    return (i, 0)


def _col_j(i, j):
    return (0, j)


def _sigmoid(x):
    return 1.0 / (1.0 + jnp.exp(-x))


def _silu(x):
    return x * _sigmoid(x)


def _ln_apply(y, mu, rstd, g, b):
    return (y - mu) * rstd * g + b


def _ln_kernel(y_ref, g_ref, b_ref, o_ref, *stat_refs):
    y = y_ref[...]
    mu = jnp.mean(y, axis=-1, keepdims=True)
    d = y - mu
    rstd = lax.rsqrt(jnp.mean(d * d, axis=-1, keepdims=True) + EPS)
    o_ref[...] = _ln_apply(y, mu, rstd, g_ref[...], b_ref[...]).astype(o_ref.dtype)
    if stat_refs:
        mu_ref, rstd_ref = stat_refs
        mu_ref[...] = jnp.broadcast_to(mu, mu_ref.shape)
        rstd_ref[...] = jnp.broadcast_to(rstd, rstd_ref.shape)


def _layer_norm(name, y, g, b, out_dtype, with_stats, tm=256):
    m, d = y.shape
    out_specs = [pl.BlockSpec((tm, d), lambda i: (i, 0))]
    out_shape = [jax.ShapeDtypeStruct((m, d), out_dtype)]
    if with_stats:
        out_specs += [pl.BlockSpec((tm, LANES), lambda i: (i, 0))] * 2
        out_shape += [jax.ShapeDtypeStruct((m, LANES), F32)] * 2
    return pl.pallas_call(
        _ln_kernel, grid=(m // tm,), name=name,
        in_specs=[pl.BlockSpec((tm, d), lambda i: (i, 0)),
                  pl.BlockSpec((1, d), lambda i: (0, 0)),
                  pl.BlockSpec((1, d), lambda i: (0, 0))],
        out_specs=out_specs, out_shape=out_shape,
        compiler_params=_cparams(1),
    )(y, g.reshape(1, d), b.reshape(1, d))


def _rope_table_kernel(pos_ref, invf_ref, c_ref, sa_ref, sb_ref):
    ang = pos_ref[...].astype(F32) * invf_ref[...]
    lane = lax.broadcasted_iota(jnp.int32, ang.shape, 1)
    cos = jnp.cos(ang)
    sin = jnp.sin(ang)
    half = ROPE // 2
    c_ref[...] = cos
    sa_ref[...] = jnp.where(lane < half, -sin, 0.0)
    sb_ref[...] = jnp.where((lane >= half) & (lane < ROPE), sin, 0.0)


def _rope_tables(positions, tm=1024):
    n = positions.size
    half = ROPE // 2
    inv_freq = ROPE_THETA ** (-jnp.arange(half, dtype=F32) / half)
    invf = jnp.concatenate([inv_freq, inv_freq, jnp.zeros((LANES - ROPE,), F32)]).reshape(1, LANES)
    pos = positions.reshape(n, 1)
    return pl.pallas_call(
        _rope_table_kernel, grid=(n // tm,), name="rope_tables",
        in_specs=[pl.BlockSpec((tm, 1), lambda i: (i, 0)),
                  pl.BlockSpec((1, LANES), lambda i: (0, 0))],
        out_specs=[pl.BlockSpec((tm, LANES), lambda i: (i, 0))] * 3,
        out_shape=[jax.ShapeDtypeStruct((n, LANES), F32)] * 3,
        compiler_params=_cparams(1),
    )(pos, invf)


def _rope_lanes(x, c, sa, sb):
    half = ROPE // 2
    return (x * c + pltpu.roll(x, LANES - half, axis=1) * sa
            + pltpu.roll(x, half, axis=1) * sb)


def _rms(x, g):
    ms = jnp.mean(x * x, axis=-1, keepdims=True)
    return x * lax.rsqrt(ms + EPS) * g


def _latent_epilogue(accs, extras):
    acc = accs[0]
    gq, gkv, c, sa, sb = extras
    qn = _rms(acc[:, :Q_LORA], gq)
    kvn = _rms(acc[:, Q_LORA:Q_LORA + KV_LORA], gkv)
    kr = _rope_lanes(acc[:, Q_LORA + KV_LORA:], c, sa, sb)
    return [qn, kvn, kr]


def _uq_epilogue(accs, extras, *, scale):
    acc = accs[0]
    c, sa, sb = extras
    tn = acc.shape[1]
    cols = []
    for h in range(tn // HEAD_W):
        base = h * HEAD_W
        cols.append(acc[:, base:base + NOPE] * scale)
        cols.append(_rope_lanes(acc[:, base + NOPE:base + HEAD_W], c, sa, sb) * scale)
    return [jnp.concatenate(cols, axis=1)]


def _attn_kernel(q_ref, kn_ref, v_ref, kr_ref, o_ref, kcat_ref, *, tq):
    t = q_ref.shape[0]
    kcat_ref[:, :NOPE] = kn_ref[...]
    kcat_ref[:, NOPE:] = kr_ref[...]
    row_chunk = lax.broadcasted_iota(jnp.int32, (tq, tq), 0) // CHUNK
    col_chunk = lax.broadcasted_iota(jnp.int32, (tq, tq), 1) // CHUNK
    visible = col_chunk <= row_chunk
    nt = (((1,), (1,)), ((), ()))
    for jq in range(t // tq):
        s0, s1 = jq * tq, (jq + 1) * tq
        q = q_ref[s0:s1, :]
        sd = lax.dot_general(q, kcat_ref[s0:s1, :], nt, preferred_element_type=F32)
        sd = jnp.where(visible, sd, -jnp.inf)
        m = jnp.max(sd, axis=-1, keepdims=True)
        if jq > 0:
            sp = lax.dot_general(q, kcat_ref[0:s0, :], nt, preferred_element_type=F32)
            m = jnp.maximum(m, jnp.max(sp, axis=-1, keepdims=True))
        pd = jnp.exp(sd - m)
        l = jnp.sum(pd, axis=-1, keepdims=True)
        o = jnp.dot(pd.astype(BF16), v_ref[s0:s1, :], preferred_element_type=F32)
        if jq > 0:
            pp = jnp.exp(sp - m)
            l = l + jnp.sum(pp, axis=-1, keepdims=True)
            o = o + jnp.dot(pp.astype(BF16), v_ref[0:s0, :], preferred_element_type=F32)
        o_ref[s0:s1, :] = (o / l).astype(o_ref.dtype)


def _attention(qf, kv, kr, batch, seq, heads):
    n = batch * seq
    kern = functools.partial(_attn_kernel, tq=ATT_TQ)
    return pl.pallas_call(
        kern, grid=(batch, heads), name="mla_attention",
        in_specs=[pl.BlockSpec((seq, HEAD_W), lambda b, h: (b, h)),
                  pl.BlockSpec((seq, NOPE), lambda b, h: (b, h)),
                  pl.BlockSpec((seq, V_DIM), lambda b, h: (b, heads + h)),
                  pl.BlockSpec((seq, LANES), lambda b, h: (b, 0))],
        out_specs=pl.BlockSpec((seq, V_DIM), lambda b, h: (b, h)),
        out_shape=jax.ShapeDtypeStruct((n, heads * V_DIM), BF16),
        scratch_shapes=[pltpu.VMEM((seq, HEAD_W), BF16)],
        compiler_params=_cparams(2),
    )(qf, kv, kv, kr)


def _split3(x):
    a = x.astype(BF16)
    r = x - a.astype(F32)
    b = r.astype(BF16)
    c = (r - b.astype(F32)).astype(BF16)
    return a, b, c


def _chunk_rows(n):
    r0 = n * HG_CHUNK
    return pl.ds(r0 if isinstance(n, int) else pl.multiple_of(r0, HG_CHUNK), HG_CHUNK)


def _hgrn_decay_stage(n, hh, refs, masks):
    d = HG_DIM
    lanes = slice(hh * d, (hh + 1) * d)
    f = refs["f"][_chunk_rows(n), lanes]
    g = jnp.log(f) * LOG2_E
    cs = jnp.dot(masks[0], jnp.concatenate(_split3(g), axis=1), preferred_element_type=F32)
    b = cs[:, :d] + cs[:, d:2 * d] + cs[:, 2 * d:]
    refs["b"][n % 2, hh] = b
    refs["c"][n % 2, hh] = b - jnp.log(1.0 - f) * LOG2_E


def _hgrn_score_stage(n, hh, refs, masks):
    c, d, n_sub = HG_CHUNK, HG_DIM, HG_CHUNK // SUB
    _, m_diag, m_l16, m_l32 = masks
    lanes = slice(hh * d, (hh + 1) * d)
    rows = _chunk_rows(n)
    slot = n % 2
    nt = (((1,), (1,)), ((), ()))

    q = refs["q"][rows, lanes].astype(F32)
    k = 1.0 - refs["f"][rows, lanes]
    b = refs["b"][slot, hh]
    b_last = b[c - 1:c, :]

    q3 = q.reshape(n_sub, SUB, d)
    b3 = b.reshape(n_sub, SUB, d)
    c3 = refs["c"][slot, hh].reshape(n_sub, SUB, d)
    half = SUB // 2
    a_ref = refs["a"]
    for s in range(SUB):
        lo = 0 if s < half else half
        a = q3[:, lo:, :] * jnp.exp2(jnp.minimum(b3[:, lo:, :] - c3[:, s:s + 1, :], 0.0))
        if lo:
            a = jnp.concatenate([jnp.zeros((n_sub, lo, d), F32), a], axis=1)
        a_ref[hh, :, s * d:(s + 1) * d] = a.reshape(c, d).astype(BF16)
    p_diag = jnp.dot(a_ref[hh], refs["sel"][...], preferred_element_type=F32)

    def level(hs):
        blk = 2 * hs
        qt, kt = [], []
        for i in range(0, c, hs):
            edge = (i // blk) * blk + hs - 1
            e = jnp.exp2(-jnp.abs(b[i:i + hs, :] - b[edge:edge + 1, :]))
            zero = jnp.zeros((hs, d), F32)
            upper = (i % blk) >= hs
            qt.append(q[i:i + hs, :] * e if upper else zero)
            kt.append(zero if upper else k[i:i + hs, :] * e)
        qt = jnp.concatenate(qt, axis=0).astype(BF16)
        kt = jnp.concatenate(kt, axis=0).astype(BF16)
        return lax.dot_general(qt, kt, nt, preferred_element_type=F32)

    p = jnp.where(m_diag, p_diag,
                  jnp.where(m_l16, level(SUB), jnp.where(m_l32, level(2 * SUB), level(4 * SUB))))
    refs["p"][slot, hh] = p.astype(BF16)
    refs["q0"][slot, hh] = (q * jnp.exp2(b)).astype(BF16)
    refs["kl"][slot, hh] = (k * jnp.exp2(b_last - b)).astype(BF16)
    refs["dl"][slot, hh] = jnp.broadcast_to(jnp.exp2(b_last), (SUBLANES, d))


def _hgrn_output_stage(n, hh, refs):
    d = HG_DIM
    lanes = slice(hh * d, (hh + 1) * d)
    rows = _chunk_rows(n)
    slot = n % 2
    nt = (((1,), (1,)), ((), ()))
    v = refs["i"][rows, lanes]
    st = refs["st"][hh]
    o = (jnp.dot(refs["p"][slot, hh], v, preferred_element_type=F32)
         + lax.dot_general(refs["q0"][slot, hh], st.astype(BF16), nt, preferred_element_type=F32))
    vt = v.astype(F32).T.astype(BF16)
    refs["st"][hh] = (refs["dl"][slot, hh][0:1, :] * st
                      + jnp.dot(vt, refs["kl"][slot, hh], preferred_element_type=F32))
    ms = jnp.mean(o * o, axis=-1, keepdims=True)
    ob = o * lax.rsqrt(ms + EPS) * refs["gn"][...] * refs["gs"][rows, lanes].astype(F32)
    refs["o"][rows, lanes] = ob.astype(refs["o"].dtype)


def _hgrn_kernel(q_ref, f_ref, i_ref, gs_ref, gn_ref, sel_ref, o_ref, st_ref, a_ref, b_ref,
                 c_ref, p_ref, q0_ref, kl_ref, dl_ref, *, hpb):
    nc = q_ref.shape[0] // HG_CHUNK
    assert nc >= 3
    c, d = HG_CHUNK, HG_DIM
    row = lax.broadcasted_iota(jnp.int32, (c, d), 0)
    col = lax.broadcasted_iota(jnp.int32, (c, d), 1)
    tri = (col <= row).astype(BF16)
    m_diag = ((row // SUB) == (col // SUB)) & ((col % SUB) <= (row % SUB))
    m_l16 = (row // (2 * SUB)) == (col // (2 * SUB))
    m_l32 = (row // (4 * SUB)) == (col // (4 * SUB))
    masks = (tri, m_diag, m_l16, m_l32)
    refs = dict(q=q_ref, f=f_ref, i=i_ref, gs=gs_ref, gn=gn_ref, sel=sel_ref, o=o_ref,
                st=st_ref, a=a_ref, b=b_ref, c=c_ref, p=p_ref, q0=q0_ref, kl=kl_ref, dl=dl_ref)

    st_ref[...] = jnp.zeros_like(st_ref)

    def run(n_out, n_score, n_decay):
        for hh in range(hpb):
            if n_out is not None:
                _hgrn_output_stage(n_out, hh, refs)
            if n_score is not None:
                _hgrn_score_stage(n_score, hh, refs, masks)
            if n_decay is not None:
                _hgrn_decay_stage(n_decay, hh, refs, masks)

    run(None, None, 0)
    run(None, 0, 1)

    def steady(n, carry):
        run(n - 2, n - 1, n)
        return carry

    lax.fori_loop(2, nc, steady, 0)
    run(nc - 2, nc - 1, None)
    run(nc - 1, None, None)


def _hgrn2(qs, f, iv, gs, gnorm, batch, seq, heads, hpb=4):
    n = batch * seq
    d = HG_DIM
    sel = (jnp.arange(SUB * d)[:, None] // d == jnp.arange(d)[None, :] % SUB).astype(BF16)
    spec = pl.BlockSpec((seq, hpb * d), lambda b, h: (b, h))
    return pl.pallas_call(
        functools.partial(_hgrn_kernel, hpb=hpb), grid=(batch, heads // hpb), name="hgrn2",
        in_specs=[spec, spec, spec, spec, pl.BlockSpec((1, d), lambda b, h: (0, 0)),
                  pl.BlockSpec((SUB * d, d), lambda b, h: (0, 0))],
        out_specs=spec,
        out_shape=jax.ShapeDtypeStruct((n, heads * d), BF16),
        scratch_shapes=[pltpu.VMEM((hpb, d, d), F32),
                        pltpu.VMEM((hpb, HG_CHUNK, SUB * d), BF16),
                        pltpu.VMEM((2, hpb, HG_CHUNK, d), F32),
                        pltpu.VMEM((2, hpb, HG_CHUNK, d), F32),
                        pltpu.VMEM((2, hpb, HG_CHUNK, d), BF16),
                        pltpu.VMEM((2, hpb, HG_CHUNK, d), BF16),
                        pltpu.VMEM((2, hpb, HG_CHUNK, d), BF16),
                        pltpu.VMEM((2, hpb, SUBLANES, d), F32)],
        compiler_params=_cparams(2),
    )(qs, f, iv, gs, gnorm.reshape(1, d), sel)


def _ep_silu(accs, extras):
    return [_silu(accs[0])]


def _ep_ident(accs, extras):
    return [accs[0]]


def _ep_forget(accs, extras):
    lbp = extras[0]
    m = jnp.max(lbp, axis=0, keepdims=True)
    e = jnp.exp(lbp - m)
    lb = e[0:1, :] / jnp.sum(e, axis=0, keepdims=True)
    return [lb + (1.0 - lb) * _sigmoid(accs[0])]


def _ep_merge(accs, extras):
    ga, a, gb, b = accs
    return [_sigmoid(ga) * a + _sigmoid(gb) * b]


def _ep_residual(accs, extras):
    y, mu, rstd, g, b = extras
    return [ALPHA * _ln_apply(y, mu[:, :1], rstd[:, :1], g, b) + accs[0]]


def _ep_swiglu(accs, extras):
    return [_silu(accs[0]) * accs[1]]


def _uq_weight(w_uq):
    r = w_uq.shape[0]
    w = w_uq.reshape(r, MLA_HEADS, NOPE + ROPE)
    w = jnp.concatenate([w, jnp.zeros((r, MLA_HEADS, HEAD_W - NOPE - ROPE), w.dtype)], axis=-1)
    return w.reshape(r, MLA_HEADS * HEAD_W).astype(BF16)


def _ukv_weight(w_ukv):
    r = w_ukv.shape[0]
    w = w_ukv.reshape(r, MLA_HEADS, NOPE + V_DIM)
    return jnp.concatenate([w[:, :, :NOPE].reshape(r, -1), w[:, :, NOPE:].reshape(r, -1)],
                           axis=1).astype(BF16)


def kernel(x, positions, ln_in_g, ln_in_b, w_in, q_norm_g, w_uq, kv_norm_g, w_ukv, hg_lb,
           hg_norm_g, w_branch_a, w_branch_b, w_out, ln1_g, ln1_b, w_gate, w_up, w_down,
           ln2_g, ln2_b):
    batch, seq, d = x.shape
    n = batch * seq
    assert w_in.shape[0] == DEPTH == 1
    l = 0
    wt = w_in.reshape(d, w_in.shape[-1]).T

    o_kr = Q_LORA + KV_LORA
    o_hq = o_kr + ROPE
    o_hf, o_hi, o_hg = o_hq + HG_WIDTH, o_hq + 2 * HG_WIDTH, o_hq + 3 * HG_WIDTH
    o_ga = o_hq + 4 * HG_WIDTH
    o_gb = o_ga + D_MODEL

    x2 = x.reshape(n, d)
    h16, h_mu, h_rstd = _layer_norm("ln_in", x2, ln_in_g, ln_in_b, BF16, True)
    rc, rsa, rsb = _rope_tables(positions)

    def residual_specs(y, mu, rstd, g, b, tm, tn):
        return [(y, (tm, tn), _tile_ij), (mu, (tm, LANES), _row_i), (rstd, (tm, LANES), _row_i),
                (g.reshape(1, d), (1, tn), _col_j), (b.reshape(1, d), (1, tn), _col_j)]

    lat_w = o_kr + LANES
    w_lat = jnp.concatenate([wt[:o_hq], jnp.zeros((LANES - ROPE, d), wt.dtype)],
                            axis=0).astype(BF16)
    tm = 512
    rope_specs = [(rc, (tm, LANES), _row_i), (rsa, (tm, LANES), _row_i), (rsb, (tm, LANES), _row_i)]
    qn, kvn, kr = _matmul(
        "latent_proj", [h16], [(0, w_lat, "nk", 0, lat_w)], _latent_epilogue,
        [((n, Q_LORA), BF16, (tm, Q_LORA), _row_i),
         ((n, KV_LORA), BF16, (tm, KV_LORA), _row_i),
         ((n, LANES), BF16, (tm, LANES), _row_i)],
        tm=tm, tn=lat_w,
        extras=[(q_norm_g[l].reshape(1, -1), (1, Q_LORA), lambda i, j: (0, 0)),
                (kv_norm_g[l].reshape(1, -1), (1, KV_LORA), lambda i, j: (0, 0))] + rope_specs)

    scale = (NOPE + ROPE) ** -0.5
    q_w = MLA_HEADS * HEAD_W
    (qf,) = _matmul("q_up_proj", [qn], [(0, _uq_weight(w_uq[l]), "kn", 0, q_w)],
                    functools.partial(_uq_epilogue, scale=scale),
                    [((n, q_w), BF16, (tm, 2048), _tile_ij)],
                    tm=tm, tn=2048, extras=rope_specs)
    kv_w = MLA_HEADS * (NOPE + V_DIM)
    (kv,) = _matmul("kv_up_proj", [kvn], [(0, _ukv_weight(w_ukv[l]), "kn", 0, kv_w)], _ep_ident,
                    [((n, kv_w), BF16, (2048, 2048), _tile_ij)], tm=2048, tn=2048)
    o_a = _attention(qf, kv, kr, batch, seq, MLA_HEADS)

    tm, tn = 1024, 1024

    def proj(name, col0, ep, dtype, extras=()):
        (r,) = _matmul(name, [h16], [(0, wt, "nk", col0, HG_WIDTH)], ep,
                       [((n, HG_WIDTH), dtype, (tm, tn), _tile_ij)],
                       tm=tm, tn=tn, extras=list(extras))
        return r

    qs = proj("hg_q_proj", o_hq, _ep_silu, BF16)
    fg = proj("hg_f_proj", o_hf, _ep_forget, F32, [(hg_lb, (hg_lb.shape[0], tn), _col_j)])
    iv = proj("hg_i_proj", o_hi, _ep_ident, BF16)
    gs = proj("hg_g_proj", o_hg, _ep_silu, BF16)
    o_b = _hgrn2(qs, fg, iv, gs, hg_norm_g[l], batch, seq, HG_HEADS)

    tm, tn = 512, 512
    (merged,) = _matmul(
        "gated_merge", [h16, o_a, o_b],
        [(0, wt, "nk", o_ga, d), (1, w_branch_a[l], "kn", 0, d),
         (0, wt, "nk", o_gb, d), (2, w_branch_b[l], "kn", 0, d)],
        _ep_merge, [((n, d), BF16, (tm, tn), _tile_ij)], tm=tm, tn=tn)

    tm, tn = 1024, 512
    (y1,) = _matmul("out_proj", [merged], [(0, w_out[l], "kn", 0, d)], _ep_residual,
                    [((n, d), F32, (tm, tn), _tile_ij)], tm=tm, tn=tn,
                    extras=residual_specs(x2, h_mu, h_rstd, ln_in_g, ln_in_b, tm, tn))
    h1_16, h1_mu, h1_rstd = _layer_norm("ln1", y1, ln1_g[l], ln1_b[l], BF16, True)

    tm, tn = 2048, 256
    (act,) = _matmul("swiglu_up", [h1_16], [(0, w_gate[l], "kn", 0, D_FF), (0, w_up[l], "kn", 0, D_FF)],
                     _ep_swiglu, [((n, D_FF), BF16, (tm, tn), _tile_ij)], tm=tm, tn=tn)
    tm, tn = 512, 512
    (y2,) = _matmul("swiglu_down", [act], [(0, w_down[l], "kn", 0, d)], _ep_residual,
                    [((n, d), F32, (tm, tn), _tile_ij)], tm=tm, tn=tn,
                    extras=residual_specs(y1, h1_mu, h1_rstd, ln1_g[l], ln1_b[l], tm, tn))
    (out,) = _layer_norm("ln2", y2, ln2_g[l], ln2_b[l], F32, False)
    return out.reshape(batch, seq, d)
```

```python
import functools

import jax
import jax.numpy as jnp
from jax import lax
from jax.experimental import pallas as pl
from jax.experimental.pallas import tpu as pltpu

F32 = jnp.float32
BF16 = jnp.bfloat16

D_MODEL = 4096
CHUNK = 64
MLA_HEADS = 16
NOPE = 128
ROPE = 64
V_DIM = 128
Q_LORA = 1024
KV_LORA = 512
ROPE_THETA = 10000.0
HG_HEADS = 16
HG_DIM = 128
HG_WIDTH = HG_HEADS * HG_DIM
D_FF = 11008
DEPTH = 1
ALPHA = (2 * DEPTH) ** 0.25
EPS = 1e-5
LOG2_E = 1.4426950408889634

LANES = 128
SUBLANES = 8
VMEM_LIMIT = 60 * 1024 * 1024

HEAD_W = 2 * LANES
SUB = 8
HG_CHUNK = 128
ATT_TQ = 256


def _cparams(n_grid, vmem=VMEM_LIMIT):
    return pltpu.CompilerParams(dimension_semantics=("arbitrary",) * n_grid,
                                vmem_limit_bytes=vmem)


def _weight_chunk_copy(w_hbm, stage, sem, spec, jj, ci):
    kind, (col0, tn, cr) = spec
    if kind == "kn":
        src = w_hbm.at[pl.ds(pl.multiple_of(ci * cr, SUBLANES), cr),
                       pl.ds(pl.multiple_of(col0 + jj * tn, LANES), tn)]
    else:
        src = w_hbm.at[pl.ds(pl.multiple_of(col0 + jj * tn + ci * cr, SUBLANES), cr), :]
    return pltpu.make_async_copy(src, stage, sem)


def _mm_kernel(*refs, pair_x, w_specs, n_x, n_extra, n_out, epilogue):
    n_pairs = len(pair_x)
    xs = refs[:n_x]
    w_refs = refs[n_x:n_x + n_pairs]
    extras = refs[n_x + n_pairs:n_x + n_pairs + n_extra]
    outs = refs[n_x + n_pairs + n_extra:n_x + n_pairs + n_extra + n_out]
    scratch = list(refs[n_x + n_pairs + n_extra + n_out:])
    j, i = pl.program_id(0), pl.program_id(1)
    nj, ni = pl.num_programs(0), pl.num_programs(1)
    slot = j % 2

    streamed = []
    tiles = []
    for wr, spec in zip(w_refs, w_specs):
        if spec[1] is None:
            tiles.append(wr)
        else:
            tile16, stage, sem = scratch[:3]
            del scratch[:3]
            streamed.append((wr, spec, tile16, stage, sem))
            tiles.append(tile16.at[slot])

    def chunk_rows(spec, ci):
        cr = spec[1][2]
        return pl.ds(pl.multiple_of(ci * cr, SUBLANES), cr)

    @pl.when((j == 0) & (i == 0))
    def _():
        for wr, spec, tile16, stage, sem in streamed:
            def load(ci, carry):
                cp = _weight_chunk_copy(wr, stage, sem, spec, 0, ci)
                cp.start()
                cp.wait()
                tile16[0, chunk_rows(spec, ci), :] = stage[...].astype(BF16)
                return carry
            lax.fori_loop(0, ni, load, 0)

            @pl.when(nj > 1)
            def _():
                _weight_chunk_copy(wr, stage, sem, spec, 1, 0).start()

    accs = []
    for xi, w, spec in zip(pair_x, tiles, w_specs):
        if spec[0] == "kn":
            accs.append(jnp.dot(xs[xi][...], w[...], preferred_element_type=F32))
        else:
            accs.append(lax.dot_general(xs[xi][...], w[...], (((1,), (1,)), ((), ())),
                                        preferred_element_type=F32))
    res = epilogue(accs, [e[...] for e in extras])
    for o, r in zip(outs, res):
        o[...] = r.astype(o.dtype)

    for wr, spec, tile16, stage, sem in streamed:
        @pl.when(j + 1 < nj)
        def _():
            _weight_chunk_copy(wr, stage, sem, spec, j + 1, i).wait()
            tile16[1 - slot, chunk_rows(spec, i), :] = stage[...].astype(BF16)

        @pl.when((j + 1 < nj) & (i + 1 < ni))
        def _():
            _weight_chunk_copy(wr, stage, sem, spec, j + 1, i + 1).start()

        @pl.when((i + 1 == ni) & (j + 2 < nj))
        def _():
            _weight_chunk_copy(wr, stage, sem, spec, j + 2, 0).start()


def _matmul(name, xs, pairs, epilogue, outs, *, tm, tn, extras=()):
    m = xs[0].shape[0]
    n = pairs[0][4]
    assert m % tm == 0 and n % tn == 0
    nj, ni = n // tn, m // tm
    grid = (nj, ni)

    in_specs = [pl.BlockSpec((tm, x.shape[1]), lambda j, i: (i, 0)) for x in xs]
    scratch, w_specs = [], []
    for _, w, kind, col0, ncols in pairs:
        assert ncols == n
        if w.dtype == BF16:
            assert col0 % tn == 0
            if kind == "kn":
                in_specs.append(pl.BlockSpec((w.shape[0], tn),
                                             lambda j, i, c0=col0 // tn: (0, c0 + j)))
            else:
                in_specs.append(pl.BlockSpec((tn, w.shape[1]),
                                             lambda j, i, c0=col0 // tn: (c0 + j, 0)))
            w_specs.append((kind, None))
            continue
        rows = w.shape[0] if kind == "kn" else tn
        lanes = tn if kind == "kn" else w.shape[1]
        assert rows % (ni * SUBLANES) == 0 and col0 % SUBLANES == 0
        if kind == "kn":
            assert col0 % LANES == 0 and tn % LANES == 0
        cr = rows // ni
        in_specs.append(pl.BlockSpec(memory_space=pl.ANY))
        scratch += [pltpu.VMEM((2, rows, lanes), BF16), pltpu.VMEM((cr, lanes), F32),
                    pltpu.SemaphoreType.DMA(())]
        w_specs.append((kind, (col0, tn, cr)))
    for _, bs, im in extras:
        in_specs.append(pl.BlockSpec(bs, functools.partial(_swap_ji, im)))
    out_specs = [pl.BlockSpec(bs, functools.partial(_swap_ji, im)) for _, _, bs, im in outs]
    out_shape = [jax.ShapeDtypeStruct(s, d) for s, d, _, _ in outs]

    kern = functools.partial(_mm_kernel, pair_x=tuple(p[0] for p in pairs),
                             w_specs=tuple(w_specs), n_x=len(xs), n_extra=len(extras),
                             n_out=len(outs), epilogue=epilogue)
    return pl.pallas_call(
        kern, grid=grid, in_specs=in_specs, out_specs=out_specs, out_shape=out_shape,
        scratch_shapes=scratch, compiler_params=_cparams(2), name=name,
    )(*xs, *[p[1] for p in pairs], *[e[0] for e in extras])


def _swap_ji(im, j, i):
    return im(i, j)


def _tile_ij(i, j):
    return (i, j)


def _row_i(i, j):
    return (i, 0)


def _col_j(i, j):
    return (0, j)


def _sigmoid(x):
    return 1.0 / (1.0 + jnp.exp(-x))


def _silu(x):
    return x * _sigmoid(x)


def _ln_apply(y, mu, rstd, g, b):
    return (y - mu) * rstd * g + b


def _ln_kernel(y_ref, g_ref, b_ref, o_ref, *stat_refs):
    y = y_ref[...]
    mu = jnp.mean(y, axis=-1, keepdims=True)
    d = y - mu
    rstd = lax.rsqrt(jnp.mean(d * d, axis=-1, keepdims=True) + EPS)
    o_ref[...] = _ln_apply(y, mu, rstd, g_ref[...], b_ref[...]).astype(o_ref.dtype)
    if stat_refs:
        mu_ref, rstd_ref = stat_refs
        mu_ref[...] = jnp.broadcast_to(mu, mu_ref.shape)
        rstd_ref[...] = jnp.broadcast_to(rstd, rstd_ref.shape)


def _layer_norm(name, y, g, b, out_dtype, with_stats, tm=256):
    m, d = y.shape
    out_specs = [pl.BlockSpec((tm, d), lambda i: (i, 0))]
    out_shape = [jax.ShapeDtypeStruct((m, d), out_dtype)]
    if with_stats:
        out_specs += [pl.BlockSpec((tm, LANES), lambda i: (i, 0))] * 2
        out_shape += [jax.ShapeDtypeStruct((m, LANES), F32)] * 2
    return pl.pallas_call(
        _ln_kernel, grid=(m // tm,), name=name,
        in_specs=[pl.BlockSpec((tm, d), lambda i: (i, 0)),
                  pl.BlockSpec((1, d), lambda i: (0, 0)),
                  pl.BlockSpec((1, d), lambda i: (0, 0))],
        out_specs=out_specs, out_shape=out_shape,
        compiler_params=_cparams(1),
    )(y, g.reshape(1, d), b.reshape(1, d))


def _rope_table_kernel(pos_ref, invf_ref, c_ref, sa_ref, sb_ref):
    ang = pos_ref[...].astype(F32) * invf_ref[...]
    lane = lax.broadcasted_iota(jnp.int32, ang.shape, 1)
    cos = jnp.cos(ang)
    sin = jnp.sin(ang)
    half = ROPE // 2
    c_ref[...] = cos
    sa_ref[...] = jnp.where(lane < half, -sin, 0.0)
    sb_ref[...] = jnp.where((lane >= half) & (lane < ROPE), sin, 0.0)


def _rope_tables(positions, tm=1024):
    n = positions.size
    half = ROPE // 2
    inv_freq = ROPE_THETA ** (-jnp.arange(half, dtype=F32) / half)
    invf = jnp.concatenate([inv_freq, inv_freq, jnp.zeros((LANES - ROPE,), F32)]).reshape(1, LANES)
    pos = positions.reshape(n, 1)
    return pl.pallas_call(
        _rope_table_kernel, grid=(n // tm,), name="rope_tables",
        in_specs=[pl.BlockSpec((tm, 1), lambda i: (i, 0)),
                  pl.BlockSpec((1, LANES), lambda i: (0, 0))],
        out_specs=[pl.BlockSpec((tm, LANES), lambda i: (i, 0))] * 3,
        out_shape=[jax.ShapeDtypeStruct((n, LANES), F32)] * 3,
        compiler_params=_cparams(1),
    )(pos, invf)


def _rope_lanes(x, c, sa, sb):
    half = ROPE // 2
    return (x * c + pltpu.roll(x, LANES - half, axis=1) * sa
            + pltpu.roll(x, half, axis=1) * sb)


def _rms(x, g):
    ms = jnp.mean(x * x, axis=-1, keepdims=True)
    return x * lax.rsqrt(ms + EPS) * g


def _latent_epilogue(accs, extras):
    acc = accs[0]
    gq, gkv, c, sa, sb = extras
    qn = _rms(acc[:, :Q_LORA], gq)
    kvn = _rms(acc[:, Q_LORA:Q_LORA + KV_LORA], gkv)
    kr = _rope_lanes(acc[:, Q_LORA + KV_LORA:], c, sa, sb)
    return [qn, kvn, kr]


def _uq_epilogue(accs, extras, *, scale):
    acc = accs[0]
    c, sa, sb = extras
    tn = acc.shape[1]
    cols = []
    for h in range(tn // HEAD_W):
        base = h * HEAD_W
        cols.append(acc[:, base:base + NOPE] * scale)
        cols.append(_rope_lanes(acc[:, base + NOPE:base + HEAD_W], c, sa, sb) * scale)
    return [jnp.concatenate(cols, axis=1)]


def _attn_kernel(q_ref, kn_ref, v_ref, kr_ref, o_ref, kcat_ref, *, tq, hpb):
    t = q_ref.shape[0]
    for hh in range(hpb):
        kcat_ref[hh, :, :NOPE] = kn_ref[:, hh * NOPE:(hh + 1) * NOPE]
        kcat_ref[hh, :, NOPE:] = kr_ref[...]
    row_chunk = lax.broadcasted_iota(jnp.int32, (tq, tq), 0) // CHUNK
    col_chunk = lax.broadcasted_iota(jnp.int32, (tq, tq), 1) // CHUNK
    visible = col_chunk <= row_chunk
    nt = (((1,), (1,)), ((), ()))
    for jq in range(t // tq):
        s0, s1 = jq * tq, (jq + 1) * tq
        for hh in range(hpb):
            vl = slice(hh * V_DIM, (hh + 1) * V_DIM)
            q = q_ref[s0:s1, hh * HEAD_W:(hh + 1) * HEAD_W]
            sd = lax.dot_general(q, kcat_ref[hh, s0:s1, :], nt, preferred_element_type=F32)
            sd = jnp.where(visible, sd, -jnp.inf)
            m = jnp.max(sd, axis=-1, keepdims=True)
            if jq > 0:
                sp = lax.dot_general(q, kcat_ref[hh, 0:s0, :], nt, preferred_element_type=F32)
                m = jnp.maximum(m, jnp.max(sp, axis=-1, keepdims=True))
            pd = jnp.exp(sd - m)
            l = jnp.sum(pd, axis=-1, keepdims=True)
            o = jnp.dot(pd.astype(BF16), v_ref[s0:s1, vl], preferred_element_type=F32)
            if jq > 0:
                pp = jnp.exp(sp - m)
                l = l + jnp.sum(pp, axis=-1, keepdims=True)
                o = o + jnp.dot(pp.astype(BF16), v_ref[0:s0, vl], preferred_element_type=F32)
            o_ref[s0:s1, vl] = (o / l).astype(o_ref.dtype)


def _attention(qf, kv, kr, batch, seq, heads, hpb=2):
    n = batch * seq
    kern = functools.partial(_attn_kernel, tq=ATT_TQ, hpb=hpb)
    groups = heads // hpb
    return pl.pallas_call(
        kern, grid=(batch, groups), name="mla_attention",
        in_specs=[pl.BlockSpec((seq, hpb * HEAD_W), lambda b, h: (b, h)),
                  pl.BlockSpec((seq, hpb * NOPE), lambda b, h: (b, h)),
                  pl.BlockSpec((seq, hpb * V_DIM), lambda b, h: (b, groups + h)),
                  pl.BlockSpec((seq, LANES), lambda b, h: (b, 0))],
        out_specs=pl.BlockSpec((seq, hpb * V_DIM), lambda b, h: (b, h)),
        out_shape=jax.ShapeDtypeStruct((n, heads * V_DIM), BF16),
        scratch_shapes=[pltpu.VMEM((hpb, seq, HEAD_W), BF16)],
        compiler_params=_cparams(2),
    )(qf, kv, kv, kr)


def _split3(x):
    a = x.astype(BF16)
    r = x - a.astype(F32)
    b = r.astype(BF16)
    c = (r - b.astype(F32)).astype(BF16)
    return a, b, c


def _chunk_rows(n):
    r0 = n * HG_CHUNK
    return pl.ds(r0 if isinstance(n, int) else pl.multiple_of(r0, HG_CHUNK), HG_CHUNK)


def _hgrn_decay_stage(n, hh, refs, masks):
    d = HG_DIM
    lanes = slice(hh * d, (hh + 1) * d)
    f = refs["f"][_chunk_rows(n), lanes]
    g = jnp.log(f) * LOG2_E
    cs = jnp.dot(masks[0], jnp.concatenate(_split3(g), axis=1), preferred_element_type=F32)
    b = cs[:, :d] + cs[:, d:2 * d] + cs[:, 2 * d:]
    refs["b"][n % 2, hh] = b
    refs["c"][n % 2, hh] = b - jnp.log(1.0 - f) * LOG2_E


def _hgrn_score_stage(n, hh, refs, masks):
    c, d, n_sub = HG_CHUNK, HG_DIM, HG_CHUNK // SUB
    m_diag, m_levels = masks[1], masks[2:]
    lanes = slice(hh * d, (hh + 1) * d)
    rows = _chunk_rows(n)
    slot = n % 2
    nt = (((1,), (1,)), ((), ()))

    q = refs["q"][rows, lanes].astype(F32)
    k = 1.0 - refs["f"][rows, lanes]
    b = refs["b"][slot, hh]
    b_last = b[c - 1:c, :]

    q3 = q.reshape(n_sub, SUB, d)
    b3 = b.reshape(n_sub, SUB, d)
    c3 = refs["c"][slot, hh].reshape(n_sub, SUB, d)
    a_ref = refs["a"]
    for s in range(SUB):
        lo = (s // SUBLANES) * SUBLANES
        a = q3[:, lo:, :] * jnp.exp2(jnp.minimum(b3[:, lo:, :] - c3[:, s:s + 1, :], 0.0))
        if lo:
            a = jnp.concatenate([jnp.zeros((n_sub, lo, d), F32), a], axis=1)
        a_ref[hh, :, s * d:(s + 1) * d] = a.reshape(c, d).astype(BF16)
    p_diag = jnp.dot(a_ref[hh], refs["sel"][...], preferred_element_type=F32)

    def level(hs):
        blk = 2 * hs
        qt, kt = [], []
        for i in range(0, c, hs):
            edge = (i // blk) * blk + hs - 1
            e = jnp.exp2(-jnp.abs(b[i:i + hs, :] - b[edge:edge + 1, :]))
            zero = jnp.zeros((hs, d), F32)
            upper = (i % blk) >= hs
            qt.append(q[i:i + hs, :] * e if upper else zero)
            kt.append(zero if upper else k[i:i + hs, :] * e)
        qt = jnp.concatenate(qt, axis=0).astype(BF16)
        kt = jnp.concatenate(kt, axis=0).astype(BF16)
        return lax.dot_general(qt, kt, nt, preferred_element_type=F32)

    p = level(c // 2)
    for idx in reversed(range(len(m_levels))):
        p = jnp.where(m_levels[idx], level(SUB << idx), p)
    p = jnp.where(m_diag, p_diag, p)
    refs["p"][slot, hh] = p.astype(BF16)
    refs["q0"][slot, hh] = (q * jnp.exp2(b)).astype(BF16)
    refs["kl"][slot, hh] = (k * jnp.exp2(b_last - b)).astype(BF16)
    refs["dl"][slot, hh] = jnp.broadcast_to(jnp.exp2(b_last), (SUBLANES, d))


def _hgrn_output_stage(n, hh, refs):
    d = HG_DIM
    lanes = slice(hh * d, (hh + 1) * d)
    rows = _chunk_rows(n)
    slot = n % 2
    nt = (((1,), (1,)), ((), ()))
    v = refs["i"][rows, lanes]
    st = refs["st"][hh]
    o = (jnp.dot(refs["p"][slot, hh], v, preferred_element_type=F32)
         + lax.dot_general(refs["q0"][slot, hh], st.astype(BF16), nt, preferred_element_type=F32))
    vt = v.astype(F32).T.astype(BF16)
    refs["st"][hh] = (refs["dl"][slot, hh][0:1, :] * st
                      + jnp.dot(vt, refs["kl"][slot, hh], preferred_element_type=F32))
    ms = jnp.mean(o * o, axis=-1, keepdims=True)
    ob = o * lax.rsqrt(ms + EPS) * refs["gn"][...] * refs["gs"][rows, lanes].astype(F32)
    refs["o"][rows, lanes] = ob.astype(refs["o"].dtype)


def _hgrn_kernel(q_ref, f_ref, i_ref, gs_ref, gn_ref, sel_ref, o_ref, st_ref, a_ref, b_ref,
                 c_ref, p_ref, q0_ref, kl_ref, dl_ref, *, hpb):
    nc = q_ref.shape[0] // HG_CHUNK
    assert nc >= 3
    c, d = HG_CHUNK, HG_DIM
    row = lax.broadcasted_iota(jnp.int32, (c, d), 0)
    col = lax.broadcasted_iota(jnp.int32, (c, d), 1)
    tri = (col <= row).astype(BF16)
    m_diag = ((row // SUB) == (col // SUB)) & ((col % SUB) <= (row % SUB))
    blks = [2 * SUB << i for i in range(64) if 2 * SUB << i < c]
    masks = (tri, m_diag) + tuple((row // blk) == (col // blk) for blk in blks)
    refs = dict(q=q_ref, f=f_ref, i=i_ref, gs=gs_ref, gn=gn_ref, sel=sel_ref, o=o_ref,
                st=st_ref, a=a_ref, b=b_ref, c=c_ref, p=p_ref, q0=q0_ref, kl=kl_ref, dl=dl_ref)

    st_ref[...] = jnp.zeros_like(st_ref)

    def run(n_out, n_score, n_decay):
        for hh in range(hpb):
            if n_out is not None:
                _hgrn_output_stage(n_out, hh, refs)
            if n_score is not None:
                _hgrn_score_stage(n_score, hh, refs, masks)
            if n_decay is not None:
                _hgrn_decay_stage(n_decay, hh, refs, masks)

    run(None, None, 0)
    run(None, 0, 1)

    def steady(n, carry):
        run(n - 2, n - 1, n)
        return carry

    lax.fori_loop(2, nc, steady, 0)
    run(nc - 2, nc - 1, None)
    run(nc - 1, None, None)


def _hgrn2(qs, f, iv, gs, gnorm, batch, seq, heads, hpb=4):
    n = batch * seq
    d = HG_DIM
    sel = (jnp.arange(SUB * d)[:, None] // d == jnp.arange(d)[None, :] % SUB).astype(BF16)
    spec = pl.BlockSpec((seq, hpb * d), lambda b, h: (b, h))
    return pl.pallas_call(
        functools.partial(_hgrn_kernel, hpb=hpb), grid=(batch, heads // hpb), name="hgrn2",
        in_specs=[spec, spec, spec, spec, pl.BlockSpec((1, d), lambda b, h: (0, 0)),
                  pl.BlockSpec((SUB * d, d), lambda b, h: (0, 0))],
        out_specs=spec,
        out_shape=jax.ShapeDtypeStruct((n, heads * d), BF16),
        scratch_shapes=[pltpu.VMEM((hpb, d, d), F32),
                        pltpu.VMEM((hpb, HG_CHUNK, SUB * d), BF16),
                        pltpu.VMEM((2, hpb, HG_CHUNK, d), F32),
                        pltpu.VMEM((2, hpb, HG_CHUNK, d), F32),
                        pltpu.VMEM((2, hpb, HG_CHUNK, d), BF16),
                        pltpu.VMEM((2, hpb, HG_CHUNK, d), BF16),
                        pltpu.VMEM((2, hpb, HG_CHUNK, d), BF16),
                        pltpu.VMEM((2, hpb, SUBLANES, d), F32)],
        compiler_params=_cparams(2),
    )(qs, f, iv, gs, gnorm.reshape(1, d), sel)


def _ep_silu(accs, extras):
    return [_silu(accs[0])]


def _ep_ident(accs, extras):
    return [accs[0]]


def _ep_forget(accs, extras):
    lbp = extras[0]
    m = jnp.max(lbp, axis=0, keepdims=True)
    e = jnp.exp(lbp - m)
    lb = e[0:1, :] / jnp.sum(e, axis=0, keepdims=True)
    return [lb + (1.0 - lb) * _sigmoid(accs[0])]


def _ep_merge(accs, extras):
    ga, a, gb, b = accs
    return [_sigmoid(ga) * a + _sigmoid(gb) * b]


def _ep_residual(accs, extras):
    y, mu, rstd, g, b = extras
    return [ALPHA * _ln_apply(y, mu[:, :1], rstd[:, :1], g, b) + accs[0]]


def _ep_swiglu(accs, extras):
    return [_silu(accs[0]) * accs[1]]


def _uq_weight(w_uq):
    r = w_uq.shape[0]
    w = w_uq.reshape(r, MLA_HEADS, NOPE + ROPE)
    w = jnp.concatenate([w, jnp.zeros((r, MLA_HEADS, HEAD_W - NOPE - ROPE), w.dtype)], axis=-1)
    return w.reshape(r, MLA_HEADS * HEAD_W).astype(BF16)


def _ukv_weight(w_ukv):
    r = w_ukv.shape[0]
    w = w_ukv.reshape(r, MLA_HEADS, NOPE + V_DIM)
    return jnp.concatenate([w[:, :, :NOPE].reshape(r, -1), w[:, :, NOPE:].reshape(r, -1)],
                           axis=1).astype(BF16)


def kernel(x, positions, ln_in_g, ln_in_b, w_in, q_norm_g, w_uq, kv_norm_g, w_ukv, hg_lb,
           hg_norm_g, w_branch_a, w_branch_b, w_out, ln1_g, ln1_b, w_gate, w_up, w_down,
           ln2_g, ln2_b):
    batch, seq, d = x.shape
    n = batch * seq
    assert w_in.shape[0] == DEPTH == 1
    l = 0
    wt = w_in.reshape(d, w_in.shape[-1]).T

    o_kr = Q_LORA + KV_LORA
    o_hq = o_kr + ROPE
    o_hf, o_hi, o_hg = o_hq + HG_WIDTH, o_hq + 2 * HG_WIDTH, o_hq + 3 * HG_WIDTH
    o_ga = o_hq + 4 * HG_WIDTH
    o_gb = o_ga + D_MODEL

    x2 = x.reshape(n, d)
    h16, h_mu, h_rstd = _layer_norm("ln_in", x2, ln_in_g, ln_in_b, BF16, True)
    rc, rsa, rsb = _rope_tables(positions)

    def residual_specs(y, mu, rstd, g, b, tm, tn):
        return [(y, (tm, tn), _tile_ij), (mu, (tm, LANES), _row_i), (rstd, (tm, LANES), _row_i),
                (g.reshape(1, d), (1, tn), _col_j), (b.reshape(1, d), (1, tn), _col_j)]

    lat_w = o_kr + LANES
    w_lat = jnp.concatenate([wt[:o_hq], jnp.zeros((LANES - ROPE, d), wt.dtype)],
                            axis=0).astype(BF16)
    tm = 512
    rope_specs = [(rc, (tm, LANES), _row_i), (rsa, (tm, LANES), _row_i), (rsb, (tm, LANES), _row_i)]
    qn, kvn, kr = _matmul(
        "latent_proj", [h16], [(0, w_lat, "nk", 0, lat_w)], _latent_epilogue,
        [((n, Q_LORA), BF16, (tm, Q_LORA), _row_i),
         ((n, KV_LORA), BF16, (tm, KV_LORA), _row_i),
         ((n, LANES), BF16, (tm, LANES), _row_i)],
        tm=tm, tn=lat_w,
        extras=[(q_norm_g[l].reshape(1, -1), (1, Q_LORA), lambda i, j: (0, 0)),
                (kv_norm_g[l].reshape(1, -1), (1, KV_LORA), lambda i, j: (0, 0))] + rope_specs)

    scale = (NOPE + ROPE) ** -0.5
    q_w = MLA_HEADS * HEAD_W
    (qf,) = _matmul("q_up_proj", [qn], [(0, _uq_weight(w_uq[l]), "kn", 0, q_w)],
                    functools.partial(_uq_epilogue, scale=scale),
                    [((n, q_w), BF16, (tm, 2048), _tile_ij)],
                    tm=tm, tn=2048, extras=rope_specs)
    kv_w = MLA_HEADS * (NOPE + V_DIM)
    (kv,) = _matmul("kv_up_proj", [kvn], [(0, _ukv_weight(w_ukv[l]), "kn", 0, kv_w)], _ep_ident,
                    [((n, kv_w), BF16, (2048, 2048), _tile_ij)], tm=2048, tn=2048)
    o_a = _attention(qf, kv, kr, batch, seq, MLA_HEADS)

    tm, tn = 1024, 1024

    def proj(name, col0, ep, dtype, extras=()):
        (r,) = _matmul(name, [h16], [(0, wt, "nk", col0, HG_WIDTH)], ep,
                       [((n, HG_WIDTH), dtype, (tm, tn), _tile_ij)],
                       tm=tm, tn=tn, extras=list(extras))
        return r

    qs = proj("hg_q_proj", o_hq, _ep_silu, BF16)
    fg = proj("hg_f_proj", o_hf, _ep_forget, F32, [(hg_lb, (hg_lb.shape[0], tn), _col_j)])
    iv = proj("hg_i_proj", o_hi, _ep_ident, BF16)
    gs = proj("hg_g_proj", o_hg, _ep_silu, BF16)
    o_b = _hgrn2(qs, fg, iv, gs, hg_norm_g[l], batch, seq, HG_HEADS)

    tm, tn = 512, 512
    (merged,) = _matmul(
        "gated_merge", [h16, o_a, o_b],
        [(0, wt, "nk", o_ga, d), (1, w_branch_a[l], "kn", 0, d),
         (0, wt, "nk", o_gb, d), (2, w_branch_b[l], "kn", 0, d)],
        _ep_merge, [((n, d), BF16, (tm, tn), _tile_ij)], tm=tm, tn=tn)

    tm, tn = 1024, 512
    (y1,) = _matmul("out_proj", [merged], [(0, w_out[l], "kn", 0, d)], _ep_residual,
                    [((n, d), F32, (tm, tn), _tile_ij)], tm=tm, tn=tn,
                    extras=residual_specs(x2, h_mu, h_rstd, ln_in_g, ln_in_b, tm, tn))
    h1_16, h1_mu, h1_rstd = _layer_norm("ln1", y1, ln1_g[l], ln1_b[l], BF16, True)

    tm, tn = 2048, 256
    (act,) = _matmul("swiglu_up", [h1_16], [(0, w_gate[l], "kn", 0, D_FF), (0, w_up[l], "kn", 0, D_FF)],
                     _ep_swiglu, [((n, D_FF), BF16, (tm, tn), _tile_ij)], tm=tm, tn=tn)
    tm, tn = 512, 512
    (y2,) = _matmul("swiglu_down", [act], [(0, w_down[l], "kn", 0, d)], _ep_residual,
                    [((n, d), F32, (tm, tn), _tile_ij)], tm=tm, tn=tn,
                    extras=residual_specs(y1, h1_mu, h1_rstd, ln1_g[l], ln1_b[l], tm, tn))
    (out,) = _layer_norm("ln2", y2, ln2_g[l], ln2_b[l], F32, False)
    return out.reshape(batch, seq, d)
```

```python
import functools

import jax
import jax.numpy as jnp
from jax import lax
from jax.experimental import pallas as pl
from jax.experimental.pallas import tpu as pltpu

F32 = jnp.float32
BF16 = jnp.bfloat16

D_MODEL = 4096
CHUNK = 64
MLA_HEADS = 16
NOPE = 128
ROPE = 64
V_DIM = 128
Q_LORA = 1024
KV_LORA = 512
ROPE_THETA = 10000.0
HG_HEADS = 16
HG_DIM = 128
HG_WIDTH = HG_HEADS * HG_DIM
D_FF = 11008
DEPTH = 1
ALPHA = (2 * DEPTH) ** 0.25
EPS = 1e-5
LOG2_E = 1.4426950408889634

LANES = 128
SUBLANES = 8
VMEM_LIMIT = 60 * 1024 * 1024

HEAD_W = 2 * LANES
SUB = 8
HG_CHUNK = 128
ATT_TQ = 256


def _cparams(n_grid, vmem=VMEM_LIMIT):
    return pltpu.CompilerParams(dimension_semantics=("arbitrary",) * n_grid,
                                vmem_limit_bytes=vmem)


def _weight_chunk_copy(w_hbm, stage, sem, spec, jj, ci):
    kind, (col0, tn, cr) = spec
    if kind == "kn":
        src = w_hbm.at[pl.ds(pl.multiple_of(ci * cr, SUBLANES), cr),
                       pl.ds(pl.multiple_of(col0 + jj * tn, LANES), tn)]
    else:
        src = w_hbm.at[pl.ds(pl.multiple_of(col0 + jj * tn + ci * cr, SUBLANES), cr), :]
    return pltpu.make_async_copy(src, stage, sem)


def _mm_kernel(*refs, pair_x, w_specs, n_x, n_extra, n_out, epilogue):
    n_pairs = len(pair_x)
    xs = refs[:n_x]
    w_refs = refs[n_x:n_x + n_pairs]
    extras = refs[n_x + n_pairs:n_x + n_pairs + n_extra]
    outs = refs[n_x + n_pairs + n_extra:n_x + n_pairs + n_extra + n_out]
    scratch = list(refs[n_x + n_pairs + n_extra + n_out:])
    j, i = pl.program_id(0), pl.program_id(1)
    nj, ni = pl.num_programs(0), pl.num_programs(1)
    slot = j % 2

    streamed = []
    tiles = []
    for wr, spec in zip(w_refs, w_specs):
        if spec[1] is None:
            tiles.append(wr)
        else:
            tile16, stage, sem = scratch[:3]
            del scratch[:3]
            streamed.append((wr, spec, tile16, stage, sem))
            tiles.append(tile16.at[slot])

    def chunk_rows(spec, ci):
        cr = spec[1][2]
        return pl.ds(pl.multiple_of(ci * cr, SUBLANES), cr)

    @pl.when((j == 0) & (i == 0))
    def _():
        for wr, spec, tile16, stage, sem in streamed:
            def load(ci, carry):
                cp = _weight_chunk_copy(wr, stage, sem, spec, 0, ci)
                cp.start()
                cp.wait()
                tile16[0, chunk_rows(spec, ci), :] = stage[...].astype(BF16)
                return carry
            lax.fori_loop(0, ni, load, 0)

            @pl.when(nj > 1)
            def _():
                _weight_chunk_copy(wr, stage, sem, spec, 1, 0).start()

    accs = []
    for xi, w, spec in zip(pair_x, tiles, w_specs):
        if spec[0] == "kn":
            accs.append(jnp.dot(xs[xi][...], w[...], preferred_element_type=F32))
        else:
            accs.append(lax.dot_general(xs[xi][...], w[...], (((1,), (1,)), ((), ())),
                                        preferred_element_type=F32))
    res = epilogue(accs, [e[...] for e in extras])
    for o, r in zip(outs, res):
        o[...] = r.astype(o.dtype)

    for wr, spec, tile16, stage, sem in streamed:
        @pl.when(j + 1 < nj)
        def _():
            _weight_chunk_copy(wr, stage, sem, spec, j + 1, i).wait()
            tile16[1 - slot, chunk_rows(spec, i), :] = stage[...].astype(BF16)

        @pl.when((j + 1 < nj) & (i + 1 < ni))
        def _():
            _weight_chunk_copy(wr, stage, sem, spec, j + 1, i + 1).start()

        @pl.when((i + 1 == ni) & (j + 2 < nj))
        def _():
            _weight_chunk_copy(wr, stage, sem, spec, j + 2, 0).start()


def _matmul(name, xs, pairs, epilogue, outs, *, tm, tn, extras=()):
    m = xs[0].shape[0]
    n = pairs[0][4]
    assert m % tm == 0 and n % tn == 0
    nj, ni = n // tn, m // tm
    grid = (nj, ni)

    in_specs = [pl.BlockSpec((tm, x.shape[1]), lambda j, i: (i, 0)) for x in xs]
    scratch, w_specs = [], []
    for _, w, kind, col0, ncols in pairs:
        assert ncols == n
        if w.dtype == BF16:
            assert col0 % tn == 0
            if kind == "kn":
                in_specs.append(pl.BlockSpec((w.shape[0], tn),
                                             lambda j, i, c0=col0 // tn: (0, c0 + j)))
            else:
                in_specs.append(pl.BlockSpec((tn, w.shape[1]),
                                             lambda j, i, c0=col0 // tn: (c0 + j, 0)))
            w_specs.append((kind, None))
            continue
        rows = w.shape[0] if kind == "kn" else tn
        lanes = tn if kind == "kn" else w.shape[1]
        assert rows % (ni * SUBLANES) == 0 and col0 % SUBLANES == 0
        if kind == "kn":
            assert col0 % LANES == 0 and tn % LANES == 0
        cr = rows // ni
        in_specs.append(pl.BlockSpec(memory_space=pl.ANY))
        scratch += [pltpu.VMEM((2, rows, lanes), BF16), pltpu.VMEM((cr, lanes), F32),
                    pltpu.SemaphoreType.DMA(())]
        w_specs.append((kind, (col0, tn, cr)))
    for _, bs, im in extras:
        in_specs.append(pl.BlockSpec(bs, functools.partial(_swap_ji, im)))
    out_specs = [pl.BlockSpec(bs, functools.partial(_swap_ji, im)) for _, _, bs, im in outs]
    out_shape = [jax.ShapeDtypeStruct(s, d) for s, d, _, _ in outs]

    kern = functools.partial(_mm_kernel, pair_x=tuple(p[0] for p in pairs),
                             w_specs=tuple(w_specs), n_x=len(xs), n_extra=len(extras),
                             n_out=len(outs), epilogue=epilogue)
    return pl.pallas_call(
        kern, grid=grid, in_specs=in_specs, out_specs=out_specs, out_shape=out_shape,
        scratch_shapes=scratch, compiler_params=_cparams(2), name=name,
    )(*xs, *[p[1] for p in pairs], *[e[0] for e in extras])


def _swap_ji(im, j, i):
    return im(i, j)


def _tile_ij(i, j):
    return (i, j)


def _row_i(i, j):
    return (i, 0)


def _col_j(i, j):
    return (0, j)


def _sigmoid(x):
    return 1.0 / (1.0 + jnp.exp(-x))


def _silu(x):
    return x * _sigmoid(x)


def _ln_apply(y, mu, rstd, g, b):
    return (y - mu) * rstd * g + b


def _ln_kernel(y_ref, g_ref, b_ref, o_ref, *stat_refs):
    y = y_ref[...]
    mu = jnp.mean(y, axis=-1, keepdims=True)
    d = y - mu
    rstd = lax.rsqrt(jnp.mean(d * d, axis=-1, keepdims=True) + EPS)
    o_ref[...] = _ln_apply(y, mu, rstd, g_ref[...], b_ref[...]).astype(o_ref.dtype)
    if stat_refs:
        mu_ref, rstd_ref = stat_refs
        mu_ref[...] = jnp.broadcast_to(mu, mu_ref.shape)
        rstd_ref[...] = jnp.broadcast_to(rstd, rstd_ref.shape)


def _layer_norm(name, y, g, b, out_dtype, with_stats, tm=512):
    m, d = y.shape
    out_specs = [pl.BlockSpec((tm, d), lambda i: (i, 0))]
    out_shape = [jax.ShapeDtypeStruct((m, d), out_dtype)]
    if with_stats:
        out_specs += [pl.BlockSpec((tm, LANES), lambda i: (i, 0))] * 2
        out_shape += [jax.ShapeDtypeStruct((m, LANES), F32)] * 2
    return pl.pallas_call(
        _ln_kernel, grid=(m // tm,), name=name,
        in_specs=[pl.BlockSpec((tm, d), lambda i: (i, 0)),
                  pl.BlockSpec((1, d), lambda i: (0, 0)),
                  pl.BlockSpec((1, d), lambda i: (0, 0))],
        out_specs=out_specs, out_shape=out_shape,
        compiler_params=_cparams(1),
    )(y, g.reshape(1, d), b.reshape(1, d))


def _rope_table_kernel(pos_ref, invf_ref, c_ref, sa_ref, sb_ref):
    ang = pos_ref[...].astype(F32) * invf_ref[...]
    lane = lax.broadcasted_iota(jnp.int32, ang.shape, 1)
    cos = jnp.cos(ang)
    sin = jnp.sin(ang)
    half = ROPE // 2
    c_ref[...] = cos
    sa_ref[...] = jnp.where(lane < half, -sin, 0.0)
    sb_ref[...] = jnp.where((lane >= half) & (lane < ROPE), sin, 0.0)


def _rope_tables(positions, tm=1024):
    n = positions.size
    half = ROPE // 2
    inv_freq = ROPE_THETA ** (-jnp.arange(half, dtype=F32) / half)
    invf = jnp.concatenate([inv_freq, inv_freq, jnp.zeros((LANES - ROPE,), F32)]).reshape(1, LANES)
    pos = positions.reshape(n, 1)
    return pl.pallas_call(
        _rope_table_kernel, grid=(n // tm,), name="rope_tables",
        in_specs=[pl.BlockSpec((tm, 1), lambda i: (i, 0)),
                  pl.BlockSpec((1, LANES), lambda i: (0, 0))],
        out_specs=[pl.BlockSpec((tm, LANES), lambda i: (i, 0))] * 3,
        out_shape=[jax.ShapeDtypeStruct((n, LANES), F32)] * 3,
        compiler_params=_cparams(1),
    )(pos, invf)


def _rope_lanes(x, c, sa, sb):
    half = ROPE // 2
    return (x * c + pltpu.roll(x, LANES - half, axis=1) * sa
            + pltpu.roll(x, half, axis=1) * sb)


def _rms(x, g):
    ms = jnp.mean(x * x, axis=-1, keepdims=True)
    return x * lax.rsqrt(ms + EPS) * g


def _latent_epilogue(accs, extras):
    acc = accs[0]
    gq, gkv, c, sa, sb = extras
    qn = _rms(acc[:, :Q_LORA], gq)
    kvn = _rms(acc[:, Q_LORA:Q_LORA + KV_LORA], gkv)
    kr = _rope_lanes(acc[:, Q_LORA + KV_LORA:], c, sa, sb)
    return [qn, kvn, kr]


def _uq_epilogue(accs, extras, *, scale):
    acc = accs[0]
    c, sa, sb = extras
    tn = acc.shape[1]
    cols = []
    for h in range(tn // HEAD_W):
        base = h * HEAD_W
        cols.append(acc[:, base:base + NOPE] * scale)
        cols.append(_rope_lanes(acc[:, base + NOPE:base + HEAD_W], c, sa, sb) * scale)
    return [jnp.concatenate(cols, axis=1)]


def _attn_kernel(q_ref, kn_ref, v_ref, kr_ref, o_ref, kcat_ref, *, tq, hpb):
    t = q_ref.shape[0]
    for hh in range(hpb):
        kcat_ref[hh, :, :NOPE] = kn_ref[:, hh * NOPE:(hh + 1) * NOPE]
        kcat_ref[hh, :, NOPE:] = kr_ref[...]
    row_chunk = lax.broadcasted_iota(jnp.int32, (tq, tq), 0) // CHUNK
    col_chunk = lax.broadcasted_iota(jnp.int32, (tq, tq), 1) // CHUNK
    visible = col_chunk <= row_chunk
    nt = (((1,), (1,)), ((), ()))
    for jq in range(t // tq):
        s0, s1 = jq * tq, (jq + 1) * tq
        for hh in range(hpb):
            vl = slice(hh * V_DIM, (hh + 1) * V_DIM)
            q = q_ref[s0:s1, hh * HEAD_W:(hh + 1) * HEAD_W]
            sd = lax.dot_general(q, kcat_ref[hh, s0:s1, :], nt, preferred_element_type=F32)
            sd = jnp.where(visible, sd, -jnp.inf)
            m = jnp.max(sd, axis=-1, keepdims=True)
            if jq > 0:
                sp = lax.dot_general(q, kcat_ref[hh, 0:s0, :], nt, preferred_element_type=F32)
                m = jnp.maximum(m, jnp.max(sp, axis=-1, keepdims=True))
            pd = jnp.exp(sd - m)
            l = jnp.sum(pd, axis=-1, keepdims=True)
            o = jnp.dot(pd.astype(BF16), v_ref[s0:s1, vl], preferred_element_type=F32)
            if jq > 0:
                pp = jnp.exp(sp - m)
                l = l + jnp.sum(pp, axis=-1, keepdims=True)
                o = o + jnp.dot(pp.astype(BF16), v_ref[0:s0, vl], preferred_element_type=F32)
            o_ref[s0:s1, vl] = (o / l).astype(o_ref.dtype)


def _attention(qf, kv, kr, batch, seq, heads, hpb=2):
    n = batch * seq
    kern = functools.partial(_attn_kernel, tq=ATT_TQ, hpb=hpb)
    groups = heads // hpb
    return pl.pallas_call(
        kern, grid=(batch, groups), name="mla_attention",
        in_specs=[pl.BlockSpec((seq, hpb * HEAD_W), lambda b, h: (b, h)),
                  pl.BlockSpec((seq, hpb * NOPE), lambda b, h: (b, h)),
                  pl.BlockSpec((seq, hpb * V_DIM), lambda b, h: (b, groups + h)),
                  pl.BlockSpec((seq, LANES), lambda b, h: (b, 0))],
        out_specs=pl.BlockSpec((seq, hpb * V_DIM), lambda b, h: (b, h)),
        out_shape=jax.ShapeDtypeStruct((n, heads * V_DIM), BF16),
        scratch_shapes=[pltpu.VMEM((hpb, seq, HEAD_W), BF16)],
        compiler_params=_cparams(2),
    )(qf, kv, kv, kr)


def _split3(x):
    a = x.astype(BF16)
    r = x - a.astype(F32)
    b = r.astype(BF16)
    c = (r - b.astype(F32)).astype(BF16)
    return a, b, c


def _chunk_rows(n):
    r0 = n * HG_CHUNK
    return pl.ds(r0 if isinstance(n, int) else pl.multiple_of(r0, HG_CHUNK), HG_CHUNK)


def _hgrn_decay_stage(n, hh, refs, masks):
    d = HG_DIM
    lanes = slice(hh * d, (hh + 1) * d)
    f = refs["f"][_chunk_rows(n), lanes]
    g = jnp.log(f) * LOG2_E
    cs = jnp.dot(masks[0], jnp.concatenate(_split3(g), axis=1), preferred_element_type=F32)
    b = cs[:, :d] + cs[:, d:2 * d] + cs[:, 2 * d:]
    refs["b"][n % 2, hh] = b
    refs["c"][n % 2, hh] = b - jnp.log(1.0 - f) * LOG2_E


def _hgrn_score_stage(n, hh, refs, masks):
    c, d, n_sub = HG_CHUNK, HG_DIM, HG_CHUNK // SUB
    m_diag, m_levels = masks[1], masks[2:]
    lanes = slice(hh * d, (hh + 1) * d)
    rows = _chunk_rows(n)
    slot = n % 2
    nt = (((1,), (1,)), ((), ()))

    q = refs["q"][rows, lanes].astype(F32)
    k = 1.0 - refs["f"][rows, lanes]
    b = refs["b"][slot, hh]
    b_last = b[c - 1:c, :]

    q3 = q.reshape(n_sub, SUB, d)
    b3 = b.reshape(n_sub, SUB, d)
    c3 = refs["c"][slot, hh].reshape(n_sub, SUB, d)
    a_ref = refs["a"]
    for s in range(SUB):
        lo = (s // SUBLANES) * SUBLANES
        a = q3[:, lo:, :] * jnp.exp2(jnp.minimum(b3[:, lo:, :] - c3[:, s:s + 1, :], 0.0))
        if lo:
            a = jnp.concatenate([jnp.zeros((n_sub, lo, d), F32), a], axis=1)
        a_ref[hh, :, s * d:(s + 1) * d] = a.reshape(c, d).astype(BF16)
    p_diag = jnp.dot(a_ref[hh], refs["sel"][...], preferred_element_type=F32)

    def level(hs):
        blk = 2 * hs
        qt, kt = [], []
        for i in range(0, c, hs):
            edge = (i // blk) * blk + hs - 1
            e = jnp.exp2(-jnp.abs(b[i:i + hs, :] - b[edge:edge + 1, :]))
            zero = jnp.zeros((hs, d), F32)
            upper = (i % blk) >= hs
            qt.append(q[i:i + hs, :] * e if upper else zero)
            kt.append(zero if upper else k[i:i + hs, :] * e)
        qt = jnp.concatenate(qt, axis=0).astype(BF16)
        kt = jnp.concatenate(kt, axis=0).astype(BF16)
        return lax.dot_general(qt, kt, nt, preferred_element_type=F32)

    p = level(c // 2)
    for idx in reversed(range(len(m_levels))):
        p = jnp.where(m_levels[idx], level(SUB << idx), p)
    p = jnp.where(m_diag, p_diag, p)
    refs["p"][slot, hh] = p.astype(BF16)
    refs["q0"][slot, hh] = (q * jnp.exp2(b)).astype(BF16)
    refs["kl"][slot, hh] = (k * jnp.exp2(b_last - b)).astype(BF16)
    refs["dl"][slot, hh] = jnp.broadcast_to(jnp.exp2(b_last), (SUBLANES, d))


def _hgrn_output_stage(n, hh, refs):
    d = HG_DIM
    lanes = slice(hh * d, (hh + 1) * d)
    rows = _chunk_rows(n)
    slot = n % 2
    nt = (((1,), (1,)), ((), ()))
    v = refs["i"][rows, lanes]
    st = refs["st"][hh]
    o = (jnp.dot(refs["p"][slot, hh], v, preferred_element_type=F32)
         + lax.dot_general(refs["q0"][slot, hh], st.astype(BF16), nt, preferred_element_type=F32))
    vt = v.astype(F32).T.astype(BF16)
    refs["st"][hh] = (refs["dl"][slot, hh][0:1, :] * st
                      + jnp.dot(vt, refs["kl"][slot, hh], preferred_element_type=F32))
    ms = jnp.mean(o * o, axis=-1, keepdims=True)
    ob = o * lax.rsqrt(ms + EPS) * refs["gn"][...] * refs["gs"][rows, lanes].astype(F32)
    refs["o"][rows, lanes] = ob.astype(refs["o"].dtype)


def _hgrn_kernel(q_ref, f_ref, i_ref, gs_ref, gn_ref, sel_ref, o_ref, st_ref, a_ref, b_ref,
                 c_ref, p_ref, q0_ref, kl_ref, dl_ref, *, hpb):
    nc = q_ref.shape[0] // HG_CHUNK
    assert nc >= 3
    c, d = HG_CHUNK, HG_DIM
    row = lax.broadcasted_iota(jnp.int32, (c, d), 0)
    col = lax.broadcasted_iota(jnp.int32, (c, d), 1)
    tri = (col <= row).astype(BF16)
    m_diag = ((row // SUB) == (col // SUB)) & ((col % SUB) <= (row % SUB))
    blks = [2 * SUB << i for i in range(64) if 2 * SUB << i < c]
    masks = (tri, m_diag) + tuple((row // blk) == (col // blk) for blk in blks)
    refs = dict(q=q_ref, f=f_ref, i=i_ref, gs=gs_ref, gn=gn_ref, sel=sel_ref, o=o_ref,
                st=st_ref, a=a_ref, b=b_ref, c=c_ref, p=p_ref, q0=q0_ref, kl=kl_ref, dl=dl_ref)

    st_ref[...] = jnp.zeros_like(st_ref)

    def run(n_out, n_score, n_decay):
        for hh in range(hpb):
            if n_out is not None:
                _hgrn_output_stage(n_out, hh, refs)
            if n_score is not None:
                _hgrn_score_stage(n_score, hh, refs, masks)
            if n_decay is not None:
                _hgrn_decay_stage(n_decay, hh, refs, masks)

    run(None, None, 0)
    run(None, 0, 1)

    def steady(n, carry):
        run(n - 2, n - 1, n)
        return carry

    lax.fori_loop(2, nc, steady, 0)
    run(nc - 2, nc - 1, None)
    run(nc - 1, None, None)


def _hgrn2(qs, f, iv, gs, gnorm, batch, seq, heads, hpb=4):
    n = batch * seq
    d = HG_DIM
    sel = (jnp.arange(SUB * d)[:, None] // d == jnp.arange(d)[None, :] % SUB).astype(BF16)
    spec = pl.BlockSpec((seq, hpb * d), lambda b, h: (b, h))
    return pl.pallas_call(
        functools.partial(_hgrn_kernel, hpb=hpb), grid=(batch, heads // hpb), name="hgrn2",
        in_specs=[spec, spec, spec, spec, pl.BlockSpec((1, d), lambda b, h: (0, 0)),
                  pl.BlockSpec((SUB * d, d), lambda b, h: (0, 0))],
        out_specs=spec,
        out_shape=jax.ShapeDtypeStruct((n, heads * d), BF16),
        scratch_shapes=[pltpu.VMEM((hpb, d, d), F32),
                        pltpu.VMEM((hpb, HG_CHUNK, SUB * d), BF16),
                        pltpu.VMEM((2, hpb, HG_CHUNK, d), F32),
                        pltpu.VMEM((2, hpb, HG_CHUNK, d), F32),
                        pltpu.VMEM((2, hpb, HG_CHUNK, d), BF16),
                        pltpu.VMEM((2, hpb, HG_CHUNK, d), BF16),
                        pltpu.VMEM((2, hpb, HG_CHUNK, d), BF16),
                        pltpu.VMEM((2, hpb, SUBLANES, d), F32)],
        compiler_params=_cparams(2),
    )(qs, f, iv, gs, gnorm.reshape(1, d), sel)


def _ep_silu(accs, extras):
    return [_silu(accs[0])]


def _ep_ident(accs, extras):
    return [accs[0]]


def _ep_forget(accs, extras):
    lbp = extras[0]
    m = jnp.max(lbp, axis=0, keepdims=True)
    e = jnp.exp(lbp - m)
    lb = e[0:1, :] / jnp.sum(e, axis=0, keepdims=True)
    return [lb + (1.0 - lb) * _sigmoid(accs[0])]


def _ep_merge(accs, extras):
    ga, a, gb, b = accs
    return [_sigmoid(ga) * a + _sigmoid(gb) * b]


def _ep_residual(accs, extras):
    y, mu, rstd, g, b = extras
    return [ALPHA * _ln_apply(y, mu[:, :1], rstd[:, :1], g, b) + accs[0]]


def _ep_swiglu(accs, extras):
    return [_silu(accs[0]) * accs[1]]


def _uq_weight_kernel(w_ref, o_ref):
    src = NOPE + ROPE
    pad = jnp.zeros((w_ref.shape[0], HEAD_W - src), o_ref.dtype)
    for h in range(MLA_HEADS):
        o_ref[:, h * HEAD_W:h * HEAD_W + src] = w_ref[:, h * src:(h + 1) * src].astype(o_ref.dtype)
        o_ref[:, h * HEAD_W + src:(h + 1) * HEAD_W] = pad


def _uq_weight(w_uq, tr=256):
    r = w_uq.shape[0]
    return pl.pallas_call(
        _uq_weight_kernel, grid=(r // tr,), name="q_up_weight_layout",
        in_specs=[pl.BlockSpec((tr, w_uq.shape[1]), lambda i: (i, 0))],
        out_specs=pl.BlockSpec((tr, MLA_HEADS * HEAD_W), lambda i: (i, 0)),
        out_shape=jax.ShapeDtypeStruct((r, MLA_HEADS * HEAD_W), BF16),
        compiler_params=_cparams(1),
    )(w_uq)


def _ukv_weight(w_ukv):
    r = w_ukv.shape[0]
    w = w_ukv.reshape(r, MLA_HEADS, NOPE + V_DIM)
    return jnp.concatenate([w[:, :, :NOPE].reshape(r, -1), w[:, :, NOPE:].reshape(r, -1)],
                           axis=1).astype(BF16)


def kernel(x, positions, ln_in_g, ln_in_b, w_in, q_norm_g, w_uq, kv_norm_g, w_ukv, hg_lb,
           hg_norm_g, w_branch_a, w_branch_b, w_out, ln1_g, ln1_b, w_gate, w_up, w_down,
           ln2_g, ln2_b):
    batch, seq, d = x.shape
    n = batch * seq
    assert w_in.shape[0] == DEPTH == 1
    l = 0
    wt = w_in.reshape(d, w_in.shape[-1]).T

    o_kr = Q_LORA + KV_LORA
    o_hq = o_kr + ROPE
    o_hf, o_hi, o_hg = o_hq + HG_WIDTH, o_hq + 2 * HG_WIDTH, o_hq + 3 * HG_WIDTH
    o_ga = o_hq + 4 * HG_WIDTH
    o_gb = o_ga + D_MODEL

    x2 = x.reshape(n, d)
    h16, h_mu, h_rstd = _layer_norm("ln_in", x2, ln_in_g, ln_in_b, BF16, True)
    rc, rsa, rsb = _rope_tables(positions)

    def residual_specs(y, mu, rstd, g, b, tm, tn):
        return [(y, (tm, tn), _tile_ij), (mu, (tm, LANES), _row_i), (rstd, (tm, LANES), _row_i),
                (g.reshape(1, d), (1, tn), _col_j), (b.reshape(1, d), (1, tn), _col_j)]

    lat_w = o_kr + LANES
    w_lat = jnp.concatenate([wt[:o_hq], jnp.zeros((LANES - ROPE, d), wt.dtype)],
                            axis=0).astype(BF16)
    tm = 512
    rope_specs = [(rc, (tm, LANES), _row_i), (rsa, (tm, LANES), _row_i), (rsb, (tm, LANES), _row_i)]
    qn, kvn, kr = _matmul(
        "latent_proj", [h16], [(0, w_lat, "nk", 0, lat_w)], _latent_epilogue,
        [((n, Q_LORA), BF16, (tm, Q_LORA), _row_i),
         ((n, KV_LORA), BF16, (tm, KV_LORA), _row_i),
         ((n, LANES), BF16, (tm, LANES), _row_i)],
        tm=tm, tn=lat_w,
        extras=[(q_norm_g[l].reshape(1, -1), (1, Q_LORA), lambda i, j: (0, 0)),
                (kv_norm_g[l].reshape(1, -1), (1, KV_LORA), lambda i, j: (0, 0))] + rope_specs)

    scale = (NOPE + ROPE) ** -0.5
    q_w = MLA_HEADS * HEAD_W
    (qf,) = _matmul("q_up_proj", [qn], [(0, _uq_weight(w_uq[l]), "kn", 0, q_w)],
                    functools.partial(_uq_epilogue, scale=scale),
                    [((n, q_w), BF16, (tm, 2048), _tile_ij)],
                    tm=tm, tn=2048, extras=rope_specs)
    kv_w = MLA_HEADS * (NOPE + V_DIM)
    (kv,) = _matmul("kv_up_proj", [kvn], [(0, _ukv_weight(w_ukv[l]), "kn", 0, kv_w)], _ep_ident,
                    [((n, kv_w), BF16, (2048, 2048), _tile_ij)], tm=2048, tn=2048)
    o_a = _attention(qf, kv, kr, batch, seq, MLA_HEADS)

    tm, tn = 1024, 1024

    def proj(name, col0, ep, dtype, extras=()):
        (r,) = _matmul(name, [h16], [(0, wt, "nk", col0, HG_WIDTH)], ep,
                       [((n, HG_WIDTH), dtype, (tm, tn), _tile_ij)],
                       tm=tm, tn=tn, extras=list(extras))
        return r

    qs = proj("hg_q_proj", o_hq, _ep_silu, BF16)
    fg = proj("hg_f_proj", o_hf, _ep_forget, F32, [(hg_lb, (hg_lb.shape[0], tn), _col_j)])
    iv = proj("hg_i_proj", o_hi, _ep_ident, BF16)
    gs = proj("hg_g_proj", o_hg, _ep_silu, BF16)
    o_b = _hgrn2(qs, fg, iv, gs, hg_norm_g[l], batch, seq, HG_HEADS)

    tm, tn = 512, 512
    (merged,) = _matmul(
        "gated_merge", [h16, o_a, o_b],
        [(0, wt, "nk", o_ga, d), (1, w_branch_a[l], "kn", 0, d),
         (0, wt, "nk", o_gb, d), (2, w_branch_b[l], "kn", 0, d)],
        _ep_merge, [((n, d), BF16, (tm, tn), _tile_ij)], tm=tm, tn=tn)

    tm, tn = 1024, 512
    (y1,) = _matmul("out_proj", [merged], [(0, w_out[l], "kn", 0, d)], _ep_residual,
                    [((n, d), F32, (tm, tn), _tile_ij)], tm=tm, tn=tn,
                    extras=residual_specs(x2, h_mu, h_rstd, ln_in_g, ln_in_b, tm, tn))
    h1_16, h1_mu, h1_rstd = _layer_norm("ln1", y1, ln1_g[l], ln1_b[l], BF16, True)

    tm, tn = 2048, 256
    (act,) = _matmul("swiglu_up", [h1_16], [(0, w_gate[l], "kn", 0, D_FF), (0, w_up[l], "kn", 0, D_FF)],
                     _ep_swiglu, [((n, D_FF), BF16, (tm, tn), _tile_ij)], tm=tm, tn=tn)
    tm, tn = 512, 512
    (y2,) = _matmul("swiglu_down", [act], [(0, w_down[l], "kn", 0, d)], _ep_residual,
                    [((n, d), F32, (tm, tn), _tile_ij)], tm=tm, tn=tn,
                    extras=residual_specs(y1, h1_mu, h1_rstd, ln1_g[l], ln1_b[l], tm, tn))
    (out,) = _layer_norm("ln2", y2, ln2_g[l], ln2_b[l], F32, False)
    return out.reshape(batch, seq, d)
```

```python
import functools

import jax
import jax.numpy as jnp
from jax import lax
from jax.experimental import pallas as pl
from jax.experimental.pallas import tpu as pltpu

F32 = jnp.float32
BF16 = jnp.bfloat16

D_MODEL = 4096
CHUNK = 64
MLA_HEADS = 16
NOPE = 128
ROPE = 64
V_DIM = 128
Q_LORA = 1024
KV_LORA = 512
ROPE_THETA = 10000.0
HG_HEADS = 16
HG_DIM = 128
HG_WIDTH = HG_HEADS * HG_DIM
D_FF = 11008
DEPTH = 1
ALPHA = (2 * DEPTH) ** 0.25
EPS = 1e-5
LOG2_E = 1.4426950408889634

LANES = 128
SUBLANES = 8
VMEM_LIMIT = 60 * 1024 * 1024

HEAD_W = 2 * LANES
SUB = 8
HG_CHUNK = 128
ATT_TQ = 512


def _cparams(n_grid, vmem=VMEM_LIMIT):
    return pltpu.CompilerParams(dimension_semantics=("arbitrary",) * n_grid,
                                vmem_limit_bytes=vmem)


def _weight_chunk_copy(w_hbm, stage, sem, spec, jj, ci):
    kind, (col0, tn, cr) = spec
    if kind == "kn":
        src = w_hbm.at[pl.ds(pl.multiple_of(ci * cr, SUBLANES), cr),
                       pl.ds(pl.multiple_of(col0 + jj * tn, LANES), tn)]
    else:
        src = w_hbm.at[pl.ds(pl.multiple_of(col0 + jj * tn + ci * cr, SUBLANES), cr), :]
    return pltpu.make_async_copy(src, stage, sem)


def _mm_kernel(*refs, pair_x, w_specs, n_x, n_extra, n_out, epilogue):
    n_pairs = len(pair_x)
    xs = refs[:n_x]
    w_refs = refs[n_x:n_x + n_pairs]
    extras = refs[n_x + n_pairs:n_x + n_pairs + n_extra]
    outs = refs[n_x + n_pairs + n_extra:n_x + n_pairs + n_extra + n_out]
    scratch = list(refs[n_x + n_pairs + n_extra + n_out:])
    j, i = pl.program_id(0), pl.program_id(1)
    nj, ni = pl.num_programs(0), pl.num_programs(1)
    slot = j % 2

    streamed = []
    tiles = []
    for wr, spec in zip(w_refs, w_specs):
        if spec[1] is None:
            tiles.append(wr)
        else:
            tile16, stage, sem = scratch[:3]
            del scratch[:3]
            streamed.append((wr, spec, tile16, stage, sem))
            tiles.append(tile16.at[slot])

    def chunk_rows(spec, ci):
        cr = spec[1][2]
        return pl.ds(pl.multiple_of(ci * cr, SUBLANES), cr)

    @pl.when((j == 0) & (i == 0))
    def _():
        for wr, spec, tile16, stage, sem in streamed:
            def load(ci, carry):
                cp = _weight_chunk_copy(wr, stage, sem, spec, 0, ci)
                cp.start()
                cp.wait()
                tile16[0, chunk_rows(spec, ci), :] = stage[...].astype(BF16)
                return carry
            lax.fori_loop(0, ni, load, 0)

            @pl.when(nj > 1)
            def _():
                _weight_chunk_copy(wr, stage, sem, spec, 1, 0).start()

    accs = []
    for xi, w, spec in zip(pair_x, tiles, w_specs):
        if spec[0] == "kn":
            accs.append(jnp.dot(xs[xi][...], w[...], preferred_element_type=F32))
        else:
            accs.append(lax.dot_general(xs[xi][...], w[...], (((1,), (1,)), ((), ())),
                                        preferred_element_type=F32))
    res = epilogue(accs, [e[...] for e in extras])
    for o, r in zip(outs, res):
        o[...] = r.astype(o.dtype)

    for wr, spec, tile16, stage, sem in streamed:
        @pl.when(j + 1 < nj)
        def _():
            _weight_chunk_copy(wr, stage, sem, spec, j + 1, i).wait()
            tile16[1 - slot, chunk_rows(spec, i), :] = stage[...].astype(BF16)

        @pl.when((j + 1 < nj) & (i + 1 < ni))
        def _():
            _weight_chunk_copy(wr, stage, sem, spec, j + 1, i + 1).start()

        @pl.when((i + 1 == ni) & (j + 2 < nj))
        def _():
            _weight_chunk_copy(wr, stage, sem, spec, j + 2, 0).start()


def _matmul(name, xs, pairs, epilogue, outs, *, tm, tn, extras=()):
    m = xs[0].shape[0]
    n = pairs[0][4]
    assert m % tm == 0 and n % tn == 0
    nj, ni = n // tn, m // tm
    grid = (nj, ni)

    in_specs = [pl.BlockSpec((tm, x.shape[1]), lambda j, i: (i, 0)) for x in xs]
    scratch, w_specs = [], []
    for _, w, kind, col0, ncols in pairs:
        assert ncols == n
        if w.dtype == BF16:
            assert col0 % tn == 0
            if kind == "kn":
                in_specs.append(pl.BlockSpec((w.shape[0], tn),
                                             lambda j, i, c0=col0 // tn: (0, c0 + j)))
            else:
                in_specs.append(pl.BlockSpec((tn, w.shape[1]),
                                             lambda j, i, c0=col0 // tn: (c0 + j, 0)))
            w_specs.append((kind, None))
            continue
        rows = w.shape[0] if kind == "kn" else tn
        lanes = tn if kind == "kn" else w.shape[1]
        assert rows % (ni * SUBLANES) == 0 and col0 % SUBLANES == 0
        if kind == "kn":
            assert col0 % LANES == 0 and tn % LANES == 0
        cr = rows // ni
        in_specs.append(pl.BlockSpec(memory_space=pl.ANY))
        scratch += [pltpu.VMEM((2, rows, lanes), BF16), pltpu.VMEM((cr, lanes), F32),
                    pltpu.SemaphoreType.DMA(())]
        w_specs.append((kind, (col0, tn, cr)))
    for _, bs, im in extras:
        in_specs.append(pl.BlockSpec(bs, functools.partial(_swap_ji, im)))
    out_specs = [pl.BlockSpec(bs, functools.partial(_swap_ji, im)) for _, _, bs, im in outs]
    out_shape = [jax.ShapeDtypeStruct(s, d) for s, d, _, _ in outs]

    kern = functools.partial(_mm_kernel, pair_x=tuple(p[0] for p in pairs),
                             w_specs=tuple(w_specs), n_x=len(xs), n_extra=len(extras),
                             n_out=len(outs), epilogue=epilogue)
    return pl.pallas_call(
        kern, grid=grid, in_specs=in_specs, out_specs=out_specs, out_shape=out_shape,
        scratch_shapes=scratch, compiler_params=_cparams(2), name=name,
    )(*xs, *[p[1] for p in pairs], *[e[0] for e in extras])


def _swap_ji(im, j, i):
    return im(i, j)


def _tile_ij(i, j):
    return (i, j)


def _row_i(i, j):
    return (i, 0)


def _col_j(i, j):
    return (0, j)


def _sigmoid(x):
    return 1.0 / (1.0 + jnp.exp(-x))


def _silu(x):
    return x * _sigmoid(x)


def _ln_apply(y, mu, rstd, g, b):
    return (y - mu) * rstd * g + b


def _ln_kernel(y_ref, g_ref, b_ref, o_ref, *stat_refs):
    y = y_ref[...]
    mu = jnp.mean(y, axis=-1, keepdims=True)
    d = y - mu
    rstd = lax.rsqrt(jnp.mean(d * d, axis=-1, keepdims=True) + EPS)
    o_ref[...] = _ln_apply(y, mu, rstd, g_ref[...], b_ref[...]).astype(o_ref.dtype)
    if stat_refs:
        mu_ref, rstd_ref = stat_refs
        mu_ref[...] = jnp.broadcast_to(mu, mu_ref.shape)
        rstd_ref[...] = jnp.broadcast_to(rstd, rstd_ref.shape)


def _layer_norm(name, y, g, b, out_dtype, with_stats, tm=512):
    m, d = y.shape
    out_specs = [pl.BlockSpec((tm, d), lambda i: (i, 0))]
    out_shape = [jax.ShapeDtypeStruct((m, d), out_dtype)]
    if with_stats:
        out_specs += [pl.BlockSpec((tm, LANES), lambda i: (i, 0))] * 2
        out_shape += [jax.ShapeDtypeStruct((m, LANES), F32)] * 2
    return pl.pallas_call(
        _ln_kernel, grid=(m // tm,), name=name,
        in_specs=[pl.BlockSpec((tm, d), lambda i: (i, 0)),
                  pl.BlockSpec((1, d), lambda i: (0, 0)),
                  pl.BlockSpec((1, d), lambda i: (0, 0))],
        out_specs=out_specs, out_shape=out_shape,
        compiler_params=_cparams(1),
    )(y, g.reshape(1, d), b.reshape(1, d))


def _rope_table_kernel(pos_ref, invf_ref, c_ref, sa_ref, sb_ref):
    ang = pos_ref[...].astype(F32) * invf_ref[...]
    lane = lax.broadcasted_iota(jnp.int32, ang.shape, 1)
    cos = jnp.cos(ang)
    sin = jnp.sin(ang)
    half = ROPE // 2
    c_ref[...] = cos
    sa_ref[...] = jnp.where(lane < half, -sin, 0.0)
    sb_ref[...] = jnp.where((lane >= half) & (lane < ROPE), sin, 0.0)


def _rope_tables(positions, tm=1024):
    n = positions.size
    half = ROPE // 2
    inv_freq = ROPE_THETA ** (-jnp.arange(half, dtype=F32) / half)
    invf = jnp.concatenate([inv_freq, inv_freq, jnp.zeros((LANES - ROPE,), F32)]).reshape(1, LANES)
    pos = positions.reshape(n, 1)
    return pl.pallas_call(
        _rope_table_kernel, grid=(n // tm,), name="rope_tables",
        in_specs=[pl.BlockSpec((tm, 1), lambda i: (i, 0)),
                  pl.BlockSpec((1, LANES), lambda i: (0, 0))],
        out_specs=[pl.BlockSpec((tm, LANES), lambda i: (i, 0))] * 3,
        out_shape=[jax.ShapeDtypeStruct((n, LANES), F32)] * 3,
        compiler_params=_cparams(1),
    )(pos, invf)


def _rope_lanes(x, c, sa, sb):
    half = ROPE // 2
    return (x * c + pltpu.roll(x, LANES - half, axis=1) * sa
            + pltpu.roll(x, half, axis=1) * sb)


def _rms(x, g):
    ms = jnp.mean(x * x, axis=-1, keepdims=True)
    return x * lax.rsqrt(ms + EPS) * g


def _latent_epilogue(accs, extras):
    acc = accs[0]
    gq, gkv, c, sa, sb = extras
    qn = _rms(acc[:, :Q_LORA], gq)
    kvn = _rms(acc[:, Q_LORA:Q_LORA + KV_LORA], gkv)
    kr = _rope_lanes(acc[:, Q_LORA + KV_LORA:], c, sa, sb)
    return [qn, kvn, kr]


def _uq_epilogue(accs, extras, *, scale):
    acc = accs[0]
    c, sa, sb = extras
    tn = acc.shape[1]
    cols = []
    for h in range(tn // HEAD_W):
        base = h * HEAD_W
        cols.append(acc[:, base:base + NOPE] * scale)
        cols.append(_rope_lanes(acc[:, base + NOPE:base + HEAD_W], c, sa, sb) * scale)
    return [jnp.concatenate(cols, axis=1)]


def _attn_kernel(q_ref, kn_ref, v_ref, kr_ref, o_ref, kcat_ref, *, tq, hpb):
    t = q_ref.shape[0]
    for hh in range(hpb):
        kcat_ref[hh, :, :NOPE] = kn_ref[:, hh * NOPE:(hh + 1) * NOPE]
        kcat_ref[hh, :, NOPE:] = kr_ref[...]
    row_chunk = lax.broadcasted_iota(jnp.int32, (tq, tq), 0) // CHUNK
    col_chunk = lax.broadcasted_iota(jnp.int32, (tq, tq), 1) // CHUNK
    visible = col_chunk <= row_chunk
    nt = (((1,), (1,)), ((), ()))
    for jq in range(t // tq):
        s0, s1 = jq * tq, (jq + 1) * tq
        for hh in range(hpb):
            vl = slice(hh * V_DIM, (hh + 1) * V_DIM)
            q = q_ref[s0:s1, hh * HEAD_W:(hh + 1) * HEAD_W]
            sd = lax.dot_general(q, kcat_ref[hh, s0:s1, :], nt, preferred_element_type=F32)
            sd = jnp.where(visible, sd, -jnp.inf)
            m = jnp.max(sd, axis=-1, keepdims=True)
            if jq > 0:
                sp = lax.dot_general(q, kcat_ref[hh, 0:s0, :], nt, preferred_element_type=F32)
                m = jnp.maximum(m, jnp.max(sp, axis=-1, keepdims=True))
            pd = jnp.exp(sd - m)
            l = jnp.sum(pd, axis=-1, keepdims=True)
            o = jnp.dot(pd.astype(BF16), v_ref[s0:s1, vl], preferred_element_type=F32)
            if jq > 0:
                pp = jnp.exp(sp - m)
                l = l + jnp.sum(pp, axis=-1, keepdims=True)
                o = o + jnp.dot(pp.astype(BF16), v_ref[0:s0, vl], preferred_element_type=F32)
            o_ref[s0:s1, vl] = (o / l).astype(o_ref.dtype)


def _attention(qf, kv, kr, batch, seq, heads, hpb=2):
    n = batch * seq
    kern = functools.partial(_attn_kernel, tq=ATT_TQ, hpb=hpb)
    groups = heads // hpb
    return pl.pallas_call(
        kern, grid=(batch, groups), name="mla_attention",
        in_specs=[pl.BlockSpec((seq, hpb * HEAD_W), lambda b, h: (b, h)),
                  pl.BlockSpec((seq, hpb * NOPE), lambda b, h: (b, h)),
                  pl.BlockSpec((seq, hpb * V_DIM), lambda b, h: (b, groups + h)),
                  pl.BlockSpec((seq, LANES), lambda b, h: (b, 0))],
        out_specs=pl.BlockSpec((seq, hpb * V_DIM), lambda b, h: (b, h)),
        out_shape=jax.ShapeDtypeStruct((n, heads * V_DIM), BF16),
        scratch_shapes=[pltpu.VMEM((hpb, seq, HEAD_W), BF16)],
        compiler_params=_cparams(2),
    )(qf, kv, kv, kr)


def _split3(x):
    a = x.astype(BF16)
    r = x - a.astype(F32)
    b = r.astype(BF16)
    c = (r - b.astype(F32)).astype(BF16)
    return a, b, c


def _chunk_rows(n):
    r0 = n * HG_CHUNK
    return pl.ds(r0 if isinstance(n, int) else pl.multiple_of(r0, HG_CHUNK), HG_CHUNK)


def _hgrn_decay_stage(n, hh, refs, masks):
    d = HG_DIM
    lanes = slice(hh * d, (hh + 1) * d)
    f = refs["f"][_chunk_rows(n), lanes]
    g = jnp.log(f) * LOG2_E
    cs = jnp.dot(masks[0], jnp.concatenate(_split3(g), axis=1), preferred_element_type=F32)
    b = cs[:, :d] + cs[:, d:2 * d] + cs[:, 2 * d:]
    refs["b"][n % 2, hh] = b
    refs["c"][n % 2, hh] = b - jnp.log(1.0 - f) * LOG2_E


def _hgrn_score_stage(n, hh, refs, masks):
    c, d, n_sub = HG_CHUNK, HG_DIM, HG_CHUNK // SUB
    m_diag, m_levels = masks[1], masks[2:]
    lanes = slice(hh * d, (hh + 1) * d)
    rows = _chunk_rows(n)
    slot = n % 2
    nt = (((1,), (1,)), ((), ()))

    q = refs["q"][rows, lanes].astype(F32)
    k = 1.0 - refs["f"][rows, lanes]
    b = refs["b"][slot, hh]
    b_last = b[c - 1:c, :]

    q3 = q.reshape(n_sub, SUB, d)
    b3 = b.reshape(n_sub, SUB, d)
    c3 = refs["c"][slot, hh].reshape(n_sub, SUB, d)
    a_ref = refs["a"]
    for s in range(SUB):
        lo = (s // SUBLANES) * SUBLANES
        a = q3[:, lo:, :] * jnp.exp2(jnp.minimum(b3[:, lo:, :] - c3[:, s:s + 1, :], 0.0))
        if lo:
            a = jnp.concatenate([jnp.zeros((n_sub, lo, d), F32), a], axis=1)
        a_ref[hh, :, s * d:(s + 1) * d] = a.reshape(c, d).astype(BF16)
    p_diag = jnp.dot(a_ref[hh], refs["sel"][...], preferred_element_type=F32)

    def level(hs):
        blk = 2 * hs
        qt, kt = [], []
        for i in range(0, c, hs):
            edge = (i // blk) * blk + hs - 1
            e = jnp.exp2(-jnp.abs(b[i:i + hs, :] - b[edge:edge + 1, :]))
            zero = jnp.zeros((hs, d), F32)
            upper = (i % blk) >= hs
            qt.append(q[i:i + hs, :] * e if upper else zero)
            kt.append(zero if upper else k[i:i + hs, :] * e)
        qt = jnp.concatenate(qt, axis=0).astype(BF16)
        kt = jnp.concatenate(kt, axis=0).astype(BF16)
        return lax.dot_general(qt, kt, nt, preferred_element_type=F32)

    p = level(c // 2)
    for idx in reversed(range(len(m_levels))):
        p = jnp.where(m_levels[idx], level(SUB << idx), p)
    p = jnp.where(m_diag, p_diag, p)
    refs["p"][slot, hh] = p.astype(BF16)
    refs["q0"][slot, hh] = (q * jnp.exp2(b)).astype(BF16)
    refs["kl"][slot, hh] = (k * jnp.exp2(b_last - b)).astype(BF16)
    refs["dl"][slot, hh] = jnp.broadcast_to(jnp.exp2(b_last), (SUBLANES, d))


def _hgrn_output_stage(n, hh, refs):
    d = HG_DIM
    lanes = slice(hh * d, (hh + 1) * d)
    rows = _chunk_rows(n)
    slot = n % 2
    nt = (((1,), (1,)), ((), ()))
    v = refs["i"][rows, lanes]
    st = refs["st"][hh]
    o = (jnp.dot(refs["p"][slot, hh], v, preferred_element_type=F32)
         + lax.dot_general(refs["q0"][slot, hh], st.astype(BF16), nt, preferred_element_type=F32))
    vt = v.astype(F32).T.astype(BF16)
    refs["st"][hh] = (refs["dl"][slot, hh][0:1, :] * st
                      + jnp.dot(vt, refs["kl"][slot, hh], preferred_element_type=F32))
    ms = jnp.mean(o * o, axis=-1, keepdims=True)
    ob = o * lax.rsqrt(ms + EPS) * refs["gn"][...] * refs["gs"][rows, lanes].astype(F32)
    refs["o"][rows, lanes] = ob.astype(refs["o"].dtype)


def _hgrn_kernel(q_ref, f_ref, i_ref, gs_ref, gn_ref, sel_ref, o_ref, st_ref, a_ref, b_ref,
                 c_ref, p_ref, q0_ref, kl_ref, dl_ref, *, hpb):
    nc = q_ref.shape[0] // HG_CHUNK
    assert nc >= 3
    c, d = HG_CHUNK, HG_DIM
    row = lax.broadcasted_iota(jnp.int32, (c, d), 0)
    col = lax.broadcasted_iota(jnp.int32, (c, d), 1)
    tri = (col <= row).astype(BF16)
    m_diag = ((row // SUB) == (col // SUB)) & ((col % SUB) <= (row % SUB))
    blks = [2 * SUB << i for i in range(64) if 2 * SUB << i < c]
    masks = (tri, m_diag) + tuple((row // blk) == (col // blk) for blk in blks)
    refs = dict(q=q_ref, f=f_ref, i=i_ref, gs=gs_ref, gn=gn_ref, sel=sel_ref, o=o_ref,
                st=st_ref, a=a_ref, b=b_ref, c=c_ref, p=p_ref, q0=q0_ref, kl=kl_ref, dl=dl_ref)

    st_ref[...] = jnp.zeros_like(st_ref)

    def run(n_out, n_score, n_decay):
        for hh in range(hpb):
            if n_out is not None:
                _hgrn_output_stage(n_out, hh, refs)
            if n_score is not None:
                _hgrn_score_stage(n_score, hh, refs, masks)
            if n_decay is not None:
                _hgrn_decay_stage(n_decay, hh, refs, masks)

    run(None, None, 0)
    run(None, 0, 1)

    def steady(n, carry):
        run(n - 2, n - 1, n)
        return carry

    lax.fori_loop(2, nc, steady, 0)
    run(nc - 2, nc - 1, None)
    run(nc - 1, None, None)


def _hgrn2(qs, f, iv, gs, gnorm, batch, seq, heads, hpb=4):
    n = batch * seq
    d = HG_DIM
    sel = (jnp.arange(SUB * d)[:, None] // d == jnp.arange(d)[None, :] % SUB).astype(BF16)
    spec = pl.BlockSpec((seq, hpb * d), lambda b, h: (b, h))
    return pl.pallas_call(
        functools.partial(_hgrn_kernel, hpb=hpb), grid=(batch, heads // hpb), name="hgrn2",
        in_specs=[spec, spec, spec, spec, pl.BlockSpec((1, d), lambda b, h: (0, 0)),
                  pl.BlockSpec((SUB * d, d), lambda b, h: (0, 0))],
        out_specs=spec,
        out_shape=jax.ShapeDtypeStruct((n, heads * d), BF16),
        scratch_shapes=[pltpu.VMEM((hpb, d, d), F32),
                        pltpu.VMEM((hpb, HG_CHUNK, SUB * d), BF16),
                        pltpu.VMEM((2, hpb, HG_CHUNK, d), F32),
                        pltpu.VMEM((2, hpb, HG_CHUNK, d), F32),
                        pltpu.VMEM((2, hpb, HG_CHUNK, d), BF16),
                        pltpu.VMEM((2, hpb, HG_CHUNK, d), BF16),
                        pltpu.VMEM((2, hpb, HG_CHUNK, d), BF16),
                        pltpu.VMEM((2, hpb, SUBLANES, d), F32)],
        compiler_params=_cparams(2),
    )(qs, f, iv, gs, gnorm.reshape(1, d), sel)


def _ep_silu(accs, extras):
    return [_silu(accs[0])]


def _ep_ident(accs, extras):
    return [accs[0]]


def _ep_forget(accs, extras):
    lbp = extras[0]
    m = jnp.max(lbp, axis=0, keepdims=True)
    e = jnp.exp(lbp - m)
    lb = e[0:1, :] / jnp.sum(e, axis=0, keepdims=True)
    return [lb + (1.0 - lb) * _sigmoid(accs[0])]


def _ep_merge(accs, extras):
    ga, a, gb, b = accs
    return [_sigmoid(ga) * a + _sigmoid(gb) * b]


def _ep_residual(accs, extras):
    y, mu, rstd, g, b = extras
    return [ALPHA * _ln_apply(y, mu[:, :1], rstd[:, :1], g, b) + accs[0]]


def _ep_swiglu(accs, extras):
    return [_silu(accs[0]) * accs[1]]


def _uq_weight_kernel(w_ref, o_ref):
    src = NOPE + ROPE
    pad = jnp.zeros((w_ref.shape[0], HEAD_W - src), o_ref.dtype)
    for h in range(MLA_HEADS):
        o_ref[:, h * HEAD_W:h * HEAD_W + src] = w_ref[:, h * src:(h + 1) * src].astype(o_ref.dtype)
        o_ref[:, h * HEAD_W + src:(h + 1) * HEAD_W] = pad


def _uq_weight(w_uq, tr=256):
    r = w_uq.shape[0]
    return pl.pallas_call(
        _uq_weight_kernel, grid=(r // tr,), name="q_up_weight_layout",
        in_specs=[pl.BlockSpec((tr, w_uq.shape[1]), lambda i: (i, 0))],
        out_specs=pl.BlockSpec((tr, MLA_HEADS * HEAD_W), lambda i: (i, 0)),
        out_shape=jax.ShapeDtypeStruct((r, MLA_HEADS * HEAD_W), BF16),
        compiler_params=_cparams(1),
    )(w_uq)


def _ukv_weight(w_ukv):
    r = w_ukv.shape[0]
    w = w_ukv.reshape(r, MLA_HEADS, NOPE + V_DIM)
    return jnp.concatenate([w[:, :, :NOPE].reshape(r, -1), w[:, :, NOPE:].reshape(r, -1)],
                           axis=1).astype(BF16)


def kernel(x, positions, ln_in_g, ln_in_b, w_in, q_norm_g, w_uq, kv_norm_g, w_ukv, hg_lb,
           hg_norm_g, w_branch_a, w_branch_b, w_out, ln1_g, ln1_b, w_gate, w_up, w_down,
           ln2_g, ln2_b):
    batch, seq, d = x.shape
    n = batch * seq
    assert w_in.shape[0] == DEPTH == 1
    l = 0
    wt = w_in.reshape(d, w_in.shape[-1]).T

    o_kr = Q_LORA + KV_LORA
    o_hq = o_kr + ROPE
    o_hf, o_hi, o_hg = o_hq + HG_WIDTH, o_hq + 2 * HG_WIDTH, o_hq + 3 * HG_WIDTH
    o_ga = o_hq + 4 * HG_WIDTH
    o_gb = o_ga + D_MODEL

    x2 = x.reshape(n, d)
    h16, h_mu, h_rstd = _layer_norm("ln_in", x2, ln_in_g, ln_in_b, BF16, True)
    rc, rsa, rsb = _rope_tables(positions)

    def residual_specs(y, mu, rstd, g, b, tm, tn):
        return [(y, (tm, tn), _tile_ij), (mu, (tm, LANES), _row_i), (rstd, (tm, LANES), _row_i),
                (g.reshape(1, d), (1, tn), _col_j), (b.reshape(1, d), (1, tn), _col_j)]

    lat_w = o_kr + LANES
    w_lat = jnp.concatenate([wt[:o_hq], jnp.zeros((LANES - ROPE, d), wt.dtype)],
                            axis=0).astype(BF16)
    tm = 1024
    rope_specs = [(rc, (tm, LANES), _row_i), (rsa, (tm, LANES), _row_i), (rsb, (tm, LANES), _row_i)]
    qn, kvn, kr = _matmul(
        "latent_proj", [h16], [(0, w_lat, "nk", 0, lat_w)], _latent_epilogue,
        [((n, Q_LORA), BF16, (tm, Q_LORA), _row_i),
         ((n, KV_LORA), BF16, (tm, KV_LORA), _row_i),
         ((n, LANES), BF16, (tm, LANES), _row_i)],
        tm=tm, tn=lat_w,
        extras=[(q_norm_g[l].reshape(1, -1), (1, Q_LORA), lambda i, j: (0, 0)),
                (kv_norm_g[l].reshape(1, -1), (1, KV_LORA), lambda i, j: (0, 0))] + rope_specs)

    scale = (NOPE + ROPE) ** -0.5
    q_w = MLA_HEADS * HEAD_W
    (qf,) = _matmul("q_up_proj", [qn], [(0, _uq_weight(w_uq[l]), "kn", 0, q_w)],
                    functools.partial(_uq_epilogue, scale=scale),
                    [((n, q_w), BF16, (tm, 2048), _tile_ij)],
                    tm=tm, tn=2048, extras=rope_specs)
    kv_w = MLA_HEADS * (NOPE + V_DIM)
    (kv,) = _matmul("kv_up_proj", [kvn], [(0, _ukv_weight(w_ukv[l]), "kn", 0, kv_w)], _ep_ident,
                    [((n, kv_w), BF16, (2048, 2048), _tile_ij)], tm=2048, tn=2048)
    o_a = _attention(qf, kv, kr, batch, seq, MLA_HEADS)

    tm, tn = 1024, 1024

    def proj(name, col0, ep, dtype, extras=()):
        (r,) = _matmul(name, [h16], [(0, wt, "nk", col0, HG_WIDTH)], ep,
                       [((n, HG_WIDTH), dtype, (tm, tn), _tile_ij)],
                       tm=tm, tn=tn, extras=list(extras))
        return r

    qs = proj("hg_q_proj", o_hq, _ep_silu, BF16)
    fg = proj("hg_f_proj", o_hf, _ep_forget, F32, [(hg_lb, (hg_lb.shape[0], tn), _col_j)])
    iv = proj("hg_i_proj", o_hi, _ep_ident, BF16)
    gs = proj("hg_g_proj", o_hg, _ep_silu, BF16)
    o_b = _hgrn2(qs, fg, iv, gs, hg_norm_g[l], batch, seq, HG_HEADS)

    tm, tn = 512, 512
    (merged,) = _matmul(
        "gated_merge", [h16, o_a, o_b],
        [(0, wt, "nk", o_ga, d), (1, w_branch_a[l], "kn", 0, d),
         (0, wt, "nk", o_gb, d), (2, w_branch_b[l], "kn", 0, d)],
        _ep_merge, [((n, d), BF16, (tm, tn), _tile_ij)], tm=tm, tn=tn)

    tm, tn = 512, 1024
    (y1,) = _matmul("out_proj", [merged], [(0, w_out[l], "kn", 0, d)], _ep_residual,
                    [((n, d), F32, (tm, tn), _tile_ij)], tm=tm, tn=tn,
                    extras=residual_specs(x2, h_mu, h_rstd, ln_in_g, ln_in_b, tm, tn))
    h1_16, h1_mu, h1_rstd = _layer_norm("ln1", y1, ln1_g[l], ln1_b[l], BF16, True)

    tm, tn = 2048, 256
    (act,) = _matmul("swiglu_up", [h1_16], [(0, w_gate[l], "kn", 0, D_FF), (0, w_up[l], "kn", 0, D_FF)],
                     _ep_swiglu, [((n, D_FF), BF16, (tm, tn), _tile_ij)], tm=tm, tn=tn)
    tm, tn = 512, 512
    (y2,) = _matmul("swiglu_down", [act], [(0, w_down[l], "kn", 0, d)], _ep_residual,
                    [((n, d), F32, (tm, tn), _tile_ij)], tm=tm, tn=tn,
                    extras=residual_specs(y1, h1_mu, h1_rstd, ln1_g[l], ln1_b[l], tm, tn))
    (out,) = _layer_norm("ln2", y2, ln2_g[l], ln2_b[l], F32, False)
    return out.reshape(batch, seq, d)
```

```python
import functools

import jax
import jax.numpy as jnp
from jax import lax
from jax.experimental import pallas as pl
from jax.experimental.pallas import tpu as pltpu

F32 = jnp.float32
BF16 = jnp.bfloat16

D_MODEL = 4096
CHUNK = 64
MLA_HEADS = 16
NOPE = 128
ROPE = 64
V_DIM = 128
Q_LORA = 1024
KV_LORA = 512
ROPE_THETA = 10000.0
HG_HEADS = 16
HG_DIM = 128
HG_WIDTH = HG_HEADS * HG_DIM
D_FF = 11008
DEPTH = 1
ALPHA = (2 * DEPTH) ** 0.25
EPS = 1e-5
LOG2_E = 1.4426950408889634

LANES = 128
SUBLANES = 8
VMEM_LIMIT = 60 * 1024 * 1024

HEAD_W = 2 * LANES
SUB = 8
HG_CHUNK = 128
ATT_TQ = 512


def _cparams(n_grid, vmem=VMEM_LIMIT):
    return pltpu.CompilerParams(dimension_semantics=("arbitrary",) * n_grid,
                                vmem_limit_bytes=vmem)


def _weight_chunk_copy(w_hbm, stage, sem, spec, jj, ci):
    kind, (col0, tn, cr) = spec
    if kind == "kn":
        src = w_hbm.at[pl.ds(pl.multiple_of(ci * cr, SUBLANES), cr),
                       pl.ds(pl.multiple_of(col0 + jj * tn, LANES), tn)]
    else:
        src = w_hbm.at[pl.ds(pl.multiple_of(col0 + jj * tn + ci * cr, SUBLANES), cr), :]
    return pltpu.make_async_copy(src, stage, sem)


def _mm_kernel(*refs, pair_x, w_specs, n_x, n_extra, n_out, epilogue):
    n_pairs = len(pair_x)
    xs = refs[:n_x]
    w_refs = refs[n_x:n_x + n_pairs]
    extras = refs[n_x + n_pairs:n_x + n_pairs + n_extra]
    outs = refs[n_x + n_pairs + n_extra:n_x + n_pairs + n_extra + n_out]
    scratch = list(refs[n_x + n_pairs + n_extra + n_out:])
    j, i = pl.program_id(0), pl.program_id(1)
    nj, ni = pl.num_programs(0), pl.num_programs(1)
    slot = j % 2

    streamed = []
    tiles = []
    for wr, spec in zip(w_refs, w_specs):
        if spec[1] is None:
            tiles.append(wr)
        else:
            tile16, stages, sems = scratch[:3]
            del scratch[:3]
            streamed.append((wr, spec, tile16, stages, sems))
            tiles.append(tile16.at[slot])

    def chunk_rows(spec, ci):
        cr = spec[1][2]
        return pl.ds(pl.multiple_of(ci * cr, SUBLANES), cr)

    @pl.when((j == 0) & (i == 0))
    def _():
        for wr, spec, tile16, stages, sems in streamed:
            def first(ci, b):
                return _weight_chunk_copy(wr, stages.at[b], sems.at[b], spec, 0, ci)

            first(0, 0).start()

            def load(ci, carry):
                b = ci % 2

                @pl.when(ci + 1 < ni)
                def _():
                    first(ci + 1, 1 - b).start()

                first(ci, b).wait()
                tile16[0, chunk_rows(spec, ci), :] = stages[b].astype(BF16)
                return carry
            lax.fori_loop(0, ni, load, 0)

            @pl.when(nj > 1)
            def _():
                _weight_chunk_copy(wr, stages.at[0], sems.at[0], spec, 1, 0).start()

    accs = []
    for xi, w, spec in zip(pair_x, tiles, w_specs):
        if spec[0] == "kn":
            accs.append(jnp.dot(xs[xi][...], w[...], preferred_element_type=F32))
        else:
            accs.append(lax.dot_general(xs[xi][...], w[...], (((1,), (1,)), ((), ())),
                                        preferred_element_type=F32))
    res = epilogue(accs, [e[...] for e in extras])
    for o, r in zip(outs, res):
        o[...] = r.astype(o.dtype)

    for wr, spec, tile16, stages, sems in streamed:
        stage, sem = stages.at[0], sems.at[0]

        @pl.when(j + 1 < nj)
        def _():
            _weight_chunk_copy(wr, stage, sem, spec, j + 1, i).wait()
            tile16[1 - slot, chunk_rows(spec, i), :] = stage[...].astype(BF16)

        @pl.when((j + 1 < nj) & (i + 1 < ni))
        def _():
            _weight_chunk_copy(wr, stage, sem, spec, j + 1, i + 1).start()

        @pl.when((i + 1 == ni) & (j + 2 < nj))
        def _():
            _weight_chunk_copy(wr, stage, sem, spec, j + 2, 0).start()


def _matmul(name, xs, pairs, epilogue, outs, *, tm, tn, extras=()):
    m = xs[0].shape[0]
    n = pairs[0][4]
    assert m % tm == 0 and n % tn == 0
    nj, ni = n // tn, m // tm
    grid = (nj, ni)

    in_specs = [pl.BlockSpec((tm, x.shape[1]), lambda j, i: (i, 0)) for x in xs]
    scratch, w_specs = [], []
    for _, w, kind, col0, ncols in pairs:
        assert ncols == n
        if w.dtype == BF16:
            assert col0 % tn == 0
            if kind == "kn":
                in_specs.append(pl.BlockSpec((w.shape[0], tn),
                                             lambda j, i, c0=col0 // tn: (0, c0 + j)))
            else:
                in_specs.append(pl.BlockSpec((tn, w.shape[1]),
                                             lambda j, i, c0=col0 // tn: (c0 + j, 0)))
            w_specs.append((kind, None))
            continue
        rows = w.shape[0] if kind == "kn" else tn
        lanes = tn if kind == "kn" else w.shape[1]
        assert rows % (ni * SUBLANES) == 0 and col0 % SUBLANES == 0
        if kind == "kn":
            assert col0 % LANES == 0 and tn % LANES == 0
        cr = rows // ni
        in_specs.append(pl.BlockSpec(memory_space=pl.ANY))
        scratch += [pltpu.VMEM((2, rows, lanes), BF16), pltpu.VMEM((2, cr, lanes), F32),
                    pltpu.SemaphoreType.DMA((2,))]
        w_specs.append((kind, (col0, tn, cr)))
    for _, bs, im in extras:
        in_specs.append(pl.BlockSpec(bs, functools.partial(_swap_ji, im)))
    out_specs = [pl.BlockSpec(bs, functools.partial(_swap_ji, im)) for _, _, bs, im in outs]
    out_shape = [jax.ShapeDtypeStruct(s, d) for s, d, _, _ in outs]

    kern = functools.partial(_mm_kernel, pair_x=tuple(p[0] for p in pairs),
                             w_specs=tuple(w_specs), n_x=len(xs), n_extra=len(extras),
                             n_out=len(outs), epilogue=epilogue)
    return pl.pallas_call(
        kern, grid=grid, in_specs=in_specs, out_specs=out_specs, out_shape=out_shape,
        scratch_shapes=scratch, compiler_params=_cparams(2), name=name,
    )(*xs, *[p[1] for p in pairs], *[e[0] for e in extras])


def _swap_ji(im, j, i):
    return im(i, j)


def _tile_ij(i, j):
    return (i, j)


def _row_i(i, j):
    return (i, 0)


def _col_j(i, j):
    return (0, j)


def _sigmoid(x):
    return 1.0 / (1.0 + jnp.exp(-x))


def _silu(x):
    return x * _sigmoid(x)


def _ln_apply(y, mu, rstd, g, b):
    return (y - mu) * rstd * g + b


def _ln_kernel(y_ref, g_ref, b_ref, o_ref, *stat_refs):
    y = y_ref[...]
    mu = jnp.mean(y, axis=-1, keepdims=True)
    d = y - mu
    rstd = lax.rsqrt(jnp.mean(d * d, axis=-1, keepdims=True) + EPS)
    o_ref[...] = _ln_apply(y, mu, rstd, g_ref[...], b_ref[...]).astype(o_ref.dtype)
    if stat_refs:
        mu_ref, rstd_ref = stat_refs
        mu_ref[...] = jnp.broadcast_to(mu, mu_ref.shape)
        rstd_ref[...] = jnp.broadcast_to(rstd, rstd_ref.shape)


def _layer_norm(name, y, g, b, out_dtype, with_stats, tm=512):
    m, d = y.shape
    out_specs = [pl.BlockSpec((tm, d), lambda i: (i, 0))]
    out_shape = [jax.ShapeDtypeStruct((m, d), out_dtype)]
    if with_stats:
        out_specs += [pl.BlockSpec((tm, LANES), lambda i: (i, 0))] * 2
        out_shape += [jax.ShapeDtypeStruct((m, LANES), F32)] * 2
    return pl.pallas_call(
        _ln_kernel, grid=(m // tm,), name=name,
        in_specs=[pl.BlockSpec((tm, d), lambda i: (i, 0)),
                  pl.BlockSpec((1, d), lambda i: (0, 0)),
                  pl.BlockSpec((1, d), lambda i: (0, 0))],
        out_specs=out_specs, out_shape=out_shape,
        compiler_params=_cparams(1),
    )(y, g.reshape(1, d), b.reshape(1, d))


def _rope_table_kernel(pos_ref, invf_ref, c_ref, sa_ref, sb_ref):
    ang = pos_ref[...].astype(F32) * invf_ref[...]
    lane = lax.broadcasted_iota(jnp.int32, ang.shape, 1)
    cos = jnp.cos(ang)
    sin = jnp.sin(ang)
    half = ROPE // 2
    c_ref[...] = cos
    sa_ref[...] = jnp.where(lane < half, -sin, 0.0)
    sb_ref[...] = jnp.where((lane >= half) & (lane < ROPE), sin, 0.0)


def _rope_tables(positions, tm=1024):
    n = positions.size
    half = ROPE // 2
    inv_freq = ROPE_THETA ** (-jnp.arange(half, dtype=F32) / half)
    invf = jnp.concatenate([inv_freq, inv_freq, jnp.zeros((LANES - ROPE,), F32)]).reshape(1, LANES)
    pos = positions.reshape(n, 1)
    return pl.pallas_call(
        _rope_table_kernel, grid=(n // tm,), name="rope_tables",
        in_specs=[pl.BlockSpec((tm, 1), lambda i: (i, 0)),
                  pl.BlockSpec((1, LANES), lambda i: (0, 0))],
        out_specs=[pl.BlockSpec((tm, LANES), lambda i: (i, 0))] * 3,
        out_shape=[jax.ShapeDtypeStruct((n, LANES), F32)] * 3,
        compiler_params=_cparams(1),
    )(pos, invf)


def _rope_lanes(x, c, sa, sb):
    half = ROPE // 2
    return (x * c + pltpu.roll(x, LANES - half, axis=1) * sa
            + pltpu.roll(x, half, axis=1) * sb)


def _rms(x, g):
    ms = jnp.mean(x * x, axis=-1, keepdims=True)
    return x * lax.rsqrt(ms + EPS) * g


def _latent_epilogue(accs, extras):
    acc = accs[0]
    gq, gkv, c, sa, sb = extras
    qn = _rms(acc[:, :Q_LORA], gq)
    kvn = _rms(acc[:, Q_LORA:Q_LORA + KV_LORA], gkv)
    kr = _rope_lanes(acc[:, Q_LORA + KV_LORA:], c, sa, sb)
    return [qn, kvn, kr]


def _uq_epilogue(accs, extras, *, scale):
    acc = accs[0]
    c, sa, sb = extras
    tn = acc.shape[1]
    cols = []
    for h in range(tn // HEAD_W):
        base = h * HEAD_W
        cols.append(acc[:, base:base + NOPE] * scale)
        cols.append(_rope_lanes(acc[:, base + NOPE:base + HEAD_W], c, sa, sb) * scale)
    return [jnp.concatenate(cols, axis=1)]


def _attn_kernel(q_ref, kn_ref, v_ref, kr_ref, o_ref, kcat_ref, *, tq, hpb):
    t = q_ref.shape[0]
    for hh in range(hpb):
        kcat_ref[hh, :, :NOPE] = kn_ref[:, hh * NOPE:(hh + 1) * NOPE]
        kcat_ref[hh, :, NOPE:] = kr_ref[...]
    row_chunk = lax.broadcasted_iota(jnp.int32, (tq, tq), 0) // CHUNK
    col_chunk = lax.broadcasted_iota(jnp.int32, (tq, tq), 1) // CHUNK
    visible = col_chunk <= row_chunk
    nt = (((1,), (1,)), ((), ()))
    for jq in range(t // tq):
        s0, s1 = jq * tq, (jq + 1) * tq
        for hh in range(hpb):
            vl = slice(hh * V_DIM, (hh + 1) * V_DIM)
            q = q_ref[s0:s1, hh * HEAD_W:(hh + 1) * HEAD_W]
            sd = lax.dot_general(q, kcat_ref[hh, s0:s1, :], nt, preferred_element_type=F32)
            sd = jnp.where(visible, sd, -jnp.inf)
            m = jnp.max(sd, axis=-1, keepdims=True)
            if jq > 0:
                sp = lax.dot_general(q, kcat_ref[hh, 0:s0, :], nt, preferred_element_type=F32)
                m = jnp.maximum(m, jnp.max(sp, axis=-1, keepdims=True))
            pd = jnp.exp(sd - m)
            l = jnp.sum(pd, axis=-1, keepdims=True)
            o = jnp.dot(pd.astype(BF16), v_ref[s0:s1, vl], preferred_element_type=F32)
            if jq > 0:
                pp = jnp.exp(sp - m)
                l = l + jnp.sum(pp, axis=-1, keepdims=True)
                o = o + jnp.dot(pp.astype(BF16), v_ref[0:s0, vl], preferred_element_type=F32)
            o_ref[s0:s1, vl] = (o / l).astype(o_ref.dtype)


def _attention(qf, kv, kr, batch, seq, heads, hpb=2):
    n = batch * seq
    kern = functools.partial(_attn_kernel, tq=ATT_TQ, hpb=hpb)
    groups = heads // hpb
    return pl.pallas_call(
        kern, grid=(batch, groups), name="mla_attention",
        in_specs=[pl.BlockSpec((seq, hpb * HEAD_W), lambda b, h: (b, h)),
                  pl.BlockSpec((seq, hpb * NOPE), lambda b, h: (b, h)),
                  pl.BlockSpec((seq, hpb * V_DIM), lambda b, h: (b, groups + h)),
                  pl.BlockSpec((seq, LANES), lambda b, h: (b, 0))],
        out_specs=pl.BlockSpec((seq, hpb * V_DIM), lambda b, h: (b, h)),
        out_shape=jax.ShapeDtypeStruct((n, heads * V_DIM), BF16),
        scratch_shapes=[pltpu.VMEM((hpb, seq, HEAD_W), BF16)],
        compiler_params=_cparams(2),
    )(qf, kv, kv, kr)


def _split3(x):
    a = x.astype(BF16)
    r = x - a.astype(F32)
    b = r.astype(BF16)
    c = (r - b.astype(F32)).astype(BF16)
    return a, b, c


def _chunk_rows(n):
    r0 = n * HG_CHUNK
    return pl.ds(r0 if isinstance(n, int) else pl.multiple_of(r0, HG_CHUNK), HG_CHUNK)


def _hgrn_decay_stage(n, hh, refs, masks):
    d = HG_DIM
    lanes = slice(hh * d, (hh + 1) * d)
    f = refs["f"][_chunk_rows(n), lanes]
    g = jnp.log(f) * LOG2_E
    cs = jnp.dot(masks[0], jnp.concatenate(_split3(g), axis=1), preferred_element_type=F32)
    b = cs[:, :d] + cs[:, d:2 * d] + cs[:, 2 * d:]
    refs["b"][n % 2, hh] = b
    refs["c"][n % 2, hh] = b - jnp.log(1.0 - f) * LOG2_E


def _hgrn_score_stage(n, hh, refs, masks):
    c, d, n_sub = HG_CHUNK, HG_DIM, HG_CHUNK // SUB
    m_diag, m_levels = masks[1], masks[2:]
    lanes = slice(hh * d, (hh + 1) * d)
    rows = _chunk_rows(n)
    slot = n % 2
    nt = (((1,), (1,)), ((), ()))

    q = refs["q"][rows, lanes].astype(F32)
    k = 1.0 - refs["f"][rows, lanes]
    b = refs["b"][slot, hh]
    b_last = b[c - 1:c, :]

    q3 = q.reshape(n_sub, SUB, d)
    b3 = b.reshape(n_sub, SUB, d)
    c3 = refs["c"][slot, hh].reshape(n_sub, SUB, d)
    a_ref = refs["a"]
    for s in range(SUB):
        lo = (s // SUBLANES) * SUBLANES
        a = q3[:, lo:, :] * jnp.exp2(jnp.minimum(b3[:, lo:, :] - c3[:, s:s + 1, :], 0.0))
        if lo:
            a = jnp.concatenate([jnp.zeros((n_sub, lo, d), F32), a], axis=1)
        a_ref[hh, :, s * d:(s + 1) * d] = a.reshape(c, d).astype(BF16)
    p_diag = jnp.dot(a_ref[hh], refs["sel"][...], preferred_element_type=F32)

    def level(hs):
        blk = 2 * hs
        qt, kt = [], []
        for i in range(0, c, hs):
            edge = (i // blk) * blk + hs - 1
            e = jnp.exp2(-jnp.abs(b[i:i + hs, :] - b[edge:edge + 1, :]))
            zero = jnp.zeros((hs, d), F32)
            upper = (i % blk) >= hs
            qt.append(q[i:i + hs, :] * e if upper else zero)
            kt.append(zero if upper else k[i:i + hs, :] * e)
        qt = jnp.concatenate(qt, axis=0).astype(BF16)
        kt = jnp.concatenate(kt, axis=0).astype(BF16)
        return lax.dot_general(qt, kt, nt, preferred_element_type=F32)

    p = level(c // 2)
    for idx in reversed(range(len(m_levels))):
        p = jnp.where(m_levels[idx], level(SUB << idx), p)
    p = jnp.where(m_diag, p_diag, p)
    refs["p"][slot, hh] = p.astype(BF16)
    refs["q0"][slot, hh] = (q * jnp.exp2(b)).astype(BF16)
    refs["kl"][slot, hh] = (k * jnp.exp2(b_last - b)).astype(BF16)
    refs["dl"][slot, hh] = jnp.broadcast_to(jnp.exp2(b_last), (SUBLANES, d))


def _hgrn_output_stage(n, hh, refs):
    d = HG_DIM
    lanes = slice(hh * d, (hh + 1) * d)
    rows = _chunk_rows(n)
    slot = n % 2
    nt = (((1,), (1,)), ((), ()))
    v = refs["i"][rows, lanes]
    st = refs["st"][hh]
    o = (jnp.dot(refs["p"][slot, hh], v, preferred_element_type=F32)
         + lax.dot_general(refs["q0"][slot, hh], st.astype(BF16), nt, preferred_element_type=F32))
    vt = v.astype(F32).T.astype(BF16)
    refs["st"][hh] = (refs["dl"][slot, hh][0:1, :] * st
                      + jnp.dot(vt, refs["kl"][slot, hh], preferred_element_type=F32))
    ms = jnp.mean(o * o, axis=-1, keepdims=True)
    ob = o * lax.rsqrt(ms + EPS) * refs["gn"][...] * refs["gs"][rows, lanes].astype(F32)
    refs["o"][rows, lanes] = ob.astype(refs["o"].dtype)


def _hgrn_kernel(q_ref, f_ref, i_ref, gs_ref, gn_ref, sel_ref, o_ref, st_ref, a_ref, b_ref,
                 c_ref, p_ref, q0_ref, kl_ref, dl_ref, *, hpb):
    nc = q_ref.shape[0] // HG_CHUNK
    assert nc >= 3
    c, d = HG_CHUNK, HG_DIM
    row = lax.broadcasted_iota(jnp.int32, (c, d), 0)
    col = lax.broadcasted_iota(jnp.int32, (c, d), 1)
    tri = (col <= row).astype(BF16)
    m_diag = ((row // SUB) == (col // SUB)) & ((col % SUB) <= (row % SUB))
    blks = [2 * SUB << i for i in range(64) if 2 * SUB << i < c]
    masks = (tri, m_diag) + tuple((row // blk) == (col // blk) for blk in blks)
    refs = dict(q=q_ref, f=f_ref, i=i_ref, gs=gs_ref, gn=gn_ref, sel=sel_ref, o=o_ref,
                st=st_ref, a=a_ref, b=b_ref, c=c_ref, p=p_ref, q0=q0_ref, kl=kl_ref, dl=dl_ref)

    st_ref[...] = jnp.zeros_like(st_ref)

    def run(n_out, n_score, n_decay):
        for hh in range(hpb):
            if n_out is not None:
                _hgrn_output_stage(n_out, hh, refs)
            if n_score is not None:
                _hgrn_score_stage(n_score, hh, refs, masks)
            if n_decay is not None:
                _hgrn_decay_stage(n_decay, hh, refs, masks)

    run(None, None, 0)
    run(None, 0, 1)

    def steady(n, carry):
        run(n - 2, n - 1, n)
        return carry

    lax.fori_loop(2, nc, steady, 0)
    run(nc - 2, nc - 1, None)
    run(nc - 1, None, None)


def _hgrn2(qs, f, iv, gs, gnorm, batch, seq, heads, hpb=4):
    n = batch * seq
    d = HG_DIM
    sel = (jnp.arange(SUB * d)[:, None] // d == jnp.arange(d)[None, :] % SUB).astype(BF16)
    spec = pl.BlockSpec((seq, hpb * d), lambda b, h: (b, h))
    return pl.pallas_call(
        functools.partial(_hgrn_kernel, hpb=hpb), grid=(batch, heads // hpb), name="hgrn2",
        in_specs=[spec, spec, spec, spec, pl.BlockSpec((1, d), lambda b, h: (0, 0)),
                  pl.BlockSpec((SUB * d, d), lambda b, h: (0, 0))],
        out_specs=spec,
        out_shape=jax.ShapeDtypeStruct((n, heads * d), BF16),
        scratch_shapes=[pltpu.VMEM((hpb, d, d), F32),
                        pltpu.VMEM((hpb, HG_CHUNK, SUB * d), BF16),
                        pltpu.VMEM((2, hpb, HG_CHUNK, d), F32),
                        pltpu.VMEM((2, hpb, HG_CHUNK, d), F32),
                        pltpu.VMEM((2, hpb, HG_CHUNK, d), BF16),
                        pltpu.VMEM((2, hpb, HG_CHUNK, d), BF16),
                        pltpu.VMEM((2, hpb, HG_CHUNK, d), BF16),
                        pltpu.VMEM((2, hpb, SUBLANES, d), F32)],
        compiler_params=_cparams(2),
    )(qs, f, iv, gs, gnorm.reshape(1, d), sel)


def _ep_silu(accs, extras):
    return [_silu(accs[0])]


def _ep_ident(accs, extras):
    return [accs[0]]


def _ep_forget(accs, extras):
    lbp = extras[0]
    m = jnp.max(lbp, axis=0, keepdims=True)
    e = jnp.exp(lbp - m)
    lb = e[0:1, :] / jnp.sum(e, axis=0, keepdims=True)
    return [lb + (1.0 - lb) * _sigmoid(accs[0])]


def _ep_merge(accs, extras):
    ga, a, gb, b = accs
    return [_sigmoid(ga) * a + _sigmoid(gb) * b]


def _ep_residual(accs, extras):
    y, mu, rstd, g, b = extras
    return [ALPHA * _ln_apply(y, mu[:, :1], rstd[:, :1], g, b) + accs[0]]


def _ep_swiglu(accs, extras):
    return [_silu(accs[0]) * accs[1]]


def _uq_weight_kernel(w_ref, o_ref):
    src = NOPE + ROPE
    pad = jnp.zeros((w_ref.shape[0], HEAD_W - src), o_ref.dtype)
    for h in range(MLA_HEADS):
        o_ref[:, h * HEAD_W:h * HEAD_W + src] = w_ref[:, h * src:(h + 1) * src].astype(o_ref.dtype)
        o_ref[:, h * HEAD_W + src:(h + 1) * HEAD_W] = pad


def _uq_weight(w_uq, tr=256):
    r = w_uq.shape[0]
    return pl.pallas_call(
        _uq_weight_kernel, grid=(r // tr,), name="q_up_weight_layout",
        in_specs=[pl.BlockSpec((tr, w_uq.shape[1]), lambda i: (i, 0))],
        out_specs=pl.BlockSpec((tr, MLA_HEADS * HEAD_W), lambda i: (i, 0)),
        out_shape=jax.ShapeDtypeStruct((r, MLA_HEADS * HEAD_W), BF16),
        compiler_params=_cparams(1),
    )(w_uq)


def _ukv_weight(w_ukv):
    r = w_ukv.shape[0]
    w = w_ukv.reshape(r, MLA_HEADS, NOPE + V_DIM)
    return jnp.concatenate([w[:, :, :NOPE].reshape(r, -1), w[:, :, NOPE:].reshape(r, -1)],
                           axis=1).astype(BF16)


def kernel(x, positions, ln_in_g, ln_in_b, w_in, q_norm_g, w_uq, kv_norm_g, w_ukv, hg_lb,
           hg_norm_g, w_branch_a, w_branch_b, w_out, ln1_g, ln1_b, w_gate, w_up, w_down,
           ln2_g, ln2_b):
    batch, seq, d = x.shape
    n = batch * seq
    assert w_in.shape[0] == DEPTH == 1
    l = 0
    wt = w_in.reshape(d, w_in.shape[-1]).T

    o_kr = Q_LORA + KV_LORA
    o_hq = o_kr + ROPE
    o_hf, o_hi, o_hg = o_hq + HG_WIDTH, o_hq + 2 * HG_WIDTH, o_hq + 3 * HG_WIDTH
    o_ga = o_hq + 4 * HG_WIDTH
    o_gb = o_ga + D_MODEL

    x2 = x.reshape(n, d)
    h16, h_mu, h_rstd = _layer_norm("ln_in", x2, ln_in_g, ln_in_b, BF16, True)
    rc, rsa, rsb = _rope_tables(positions)

    def residual_specs(y, mu, rstd, g, b, tm, tn):
        return [(y, (tm, tn), _tile_ij), (mu, (tm, LANES), _row_i), (rstd, (tm, LANES), _row_i),
                (g.reshape(1, d), (1, tn), _col_j), (b.reshape(1, d), (1, tn), _col_j)]

    lat_w = o_kr + LANES
    w_lat = jnp.concatenate([wt[:o_hq], jnp.zeros((LANES - ROPE, d), wt.dtype)],
                            axis=0).astype(BF16)
    tm = 1024
    rope_specs = [(rc, (tm, LANES), _row_i), (rsa, (tm, LANES), _row_i), (rsb, (tm, LANES), _row_i)]
    qn, kvn, kr = _matmul(
        "latent_proj", [h16], [(0, w_lat, "nk", 0, lat_w)], _latent_epilogue,
        [((n, Q_LORA), BF16, (tm, Q_LORA), _row_i),
         ((n, KV_LORA), BF16, (tm, KV_LORA), _row_i),
         ((n, LANES), BF16, (tm, LANES), _row_i)],
        tm=tm, tn=lat_w,
        extras=[(q_norm_g[l].reshape(1, -1), (1, Q_LORA), lambda i, j: (0, 0)),
                (kv_norm_g[l].reshape(1, -1), (1, KV_LORA), lambda i, j: (0, 0))] + rope_specs)

    scale = (NOPE + ROPE) ** -0.5
    q_w = MLA_HEADS * HEAD_W
    (qf,) = _matmul("q_up_proj", [qn], [(0, _uq_weight(w_uq[l]), "kn", 0, q_w)],
                    functools.partial(_uq_epilogue, scale=scale),
                    [((n, q_w), BF16, (tm, 2048), _tile_ij)],
                    tm=tm, tn=2048, extras=rope_specs)
    kv_w = MLA_HEADS * (NOPE + V_DIM)
    (kv,) = _matmul("kv_up_proj", [kvn], [(0, _ukv_weight(w_ukv[l]), "kn", 0, kv_w)], _ep_ident,
                    [((n, kv_w), BF16, (2048, 2048), _tile_ij)], tm=2048, tn=2048)
    o_a = _attention(qf, kv, kr, batch, seq, MLA_HEADS)

    tm, tn = 1024, 1024

    def proj(name, col0, ep, dtype, extras=()):
        (r,) = _matmul(name, [h16], [(0, wt, "nk", col0, HG_WIDTH)], ep,
                       [((n, HG_WIDTH), dtype, (tm, tn), _tile_ij)],
                       tm=tm, tn=tn, extras=list(extras))
        return r

    qs = proj("hg_q_proj", o_hq, _ep_silu, BF16)
    fg = proj("hg_f_proj", o_hf, _ep_forget, F32, [(hg_lb, (hg_lb.shape[0], tn), _col_j)])
    iv = proj("hg_i_proj", o_hi, _ep_ident, BF16)
    gs = proj("hg_g_proj", o_hg, _ep_silu, BF16)
    o_b = _hgrn2(qs, fg, iv, gs, hg_norm_g[l], batch, seq, HG_HEADS)

    tm, tn = 512, 512
    (merged,) = _matmul(
        "gated_merge", [h16, o_a, o_b],
        [(0, wt, "nk", o_ga, d), (1, w_branch_a[l], "kn", 0, d),
         (0, wt, "nk", o_gb, d), (2, w_branch_b[l], "kn", 0, d)],
        _ep_merge, [((n, d), BF16, (tm, tn), _tile_ij)], tm=tm, tn=tn)

    tm, tn = 512, 1024
    (y1,) = _matmul("out_proj", [merged], [(0, w_out[l], "kn", 0, d)], _ep_residual,
                    [((n, d), F32, (tm, tn), _tile_ij)], tm=tm, tn=tn,
                    extras=residual_specs(x2, h_mu, h_rstd, ln_in_g, ln_in_b, tm, tn))
    h1_16, h1_mu, h1_rstd = _layer_norm("ln1", y1, ln1_g[l], ln1_b[l], BF16, True)

    tm, tn = 2048, 256
    (act,) = _matmul("swiglu_up", [h1_16], [(0, w_gate[l], "kn", 0, D_FF), (0, w_up[l], "kn", 0, D_FF)],
                     _ep_swiglu, [((n, D_FF), BF16, (tm, tn), _tile_ij)], tm=tm, tn=tn)
    tm, tn = 512, 512
    (y2,) = _matmul("swiglu_down", [act], [(0, w_down[l], "kn", 0, d)], _ep_residual,
                    [((n, d), F32, (tm, tn), _tile_ij)], tm=tm, tn=tn,
                    extras=residual_specs(y1, h1_mu, h1_rstd, ln1_g[l], ln1_b[l], tm, tn))
    (out,) = _layer_norm("ln2", y2, ln2_g[l], ln2_b[l], F32, False)
    return out.reshape(batch, seq, d)
```

```python
import functools

import jax
import jax.numpy as jnp
from jax import lax
from jax.experimental import pallas as pl
from jax.experimental.pallas import tpu as pltpu

F32 = jnp.float32
BF16 = jnp.bfloat16

D_MODEL = 4096
CHUNK = 64
MLA_HEADS = 16
NOPE = 128
ROPE = 64
V_DIM = 128
Q_LORA = 1024
KV_LORA = 512
ROPE_THETA = 10000.0
HG_HEADS = 16
HG_DIM = 128
HG_WIDTH = HG_HEADS * HG_DIM
D_FF = 11008
DEPTH = 1
ALPHA = (2 * DEPTH) ** 0.25
EPS = 1e-5
LOG2_E = 1.4426950408889634

LANES = 128
SUBLANES = 8
VMEM_LIMIT = 60 * 1024 * 1024

HEAD_W = 2 * LANES
SUB = 8
HG_CHUNK = 128
ATT_TQ = 512
FIRST_TILE_DEPTH = 4


def _cparams(n_grid, vmem=VMEM_LIMIT):
    return pltpu.CompilerParams(dimension_semantics=("arbitrary",) * n_grid,
                                vmem_limit_bytes=vmem)


def _weight_chunk_copy(w_hbm, stage, sem, spec, jj, ci):
    kind, (col0, tn, cr) = spec
    if kind == "kn":
        src = w_hbm.at[pl.ds(pl.multiple_of(ci * cr, SUBLANES), cr),
                       pl.ds(pl.multiple_of(col0 + jj * tn, LANES), tn)]
    else:
        src = w_hbm.at[pl.ds(pl.multiple_of(col0 + jj * tn + ci * cr, SUBLANES), cr), :]
    return pltpu.make_async_copy(src, stage, sem)


def _mm_kernel(*refs, pair_x, w_specs, n_x, n_extra, n_out, epilogue):
    n_pairs = len(pair_x)
    xs = refs[:n_x]
    w_refs = refs[n_x:n_x + n_pairs]
    extras = refs[n_x + n_pairs:n_x + n_pairs + n_extra]
    outs = refs[n_x + n_pairs + n_extra:n_x + n_pairs + n_extra + n_out]
    scratch = list(refs[n_x + n_pairs + n_extra + n_out:])
    j, i = pl.program_id(0), pl.program_id(1)
    nj, ni = pl.num_programs(0), pl.num_programs(1)
    slot = j % 2

    streamed = []
    tiles = []
    for wr, spec in zip(w_refs, w_specs):
        if spec[1] is None:
            tiles.append(wr)
        else:
            tile16, stages, sems = scratch[:3]
            del scratch[:3]
            streamed.append((wr, spec, tile16, stages, sems))
            tiles.append(tile16.at[slot])

    def chunk_rows(spec, ci):
        cr = spec[1][2]
        return pl.ds(pl.multiple_of(ci * cr, SUBLANES), cr)

    @pl.when((j == 0) & (i == 0))
    def _():
        depth = FIRST_TILE_DEPTH

        def first(k, ci):
            wr, spec, _, stages, sems = streamed[k]
            b = ci % depth
            return _weight_chunk_copy(wr, stages.at[b], sems.at[b], spec, 0, ci)

        for ci in range(depth - 1):
            @pl.when(ci < ni)
            def _():
                for k in range(len(streamed)):
                    first(k, ci).start()

        def load(ci, carry):
            for k, (_, spec, tile16, stages, _) in enumerate(streamed):
                @pl.when(ci + depth - 1 < ni)
                def _():
                    first(k, ci + depth - 1).start()

                first(k, ci).wait()
                tile16[0, chunk_rows(spec, ci), :] = stages[ci % depth].astype(BF16)
            return carry
        lax.fori_loop(0, ni, load, 0)

        for wr, spec, _, stages, sems in streamed:
            @pl.when(nj > 1)
            def _():
                _weight_chunk_copy(wr, stages.at[0], sems.at[0], spec, 1, 0).start()

    accs = []
    for xi, w, spec in zip(pair_x, tiles, w_specs):
        if spec[0] == "kn":
            accs.append(jnp.dot(xs[xi][...], w[...], preferred_element_type=F32))
        else:
            accs.append(lax.dot_general(xs[xi][...], w[...], (((1,), (1,)), ((), ())),
                                        preferred_element_type=F32))
    res = epilogue(accs, [e[...] for e in extras])
    for o, r in zip(outs, res):
        o[...] = r.astype(o.dtype)

    for wr, spec, tile16, stages, sems in streamed:
        stage, sem = stages.at[0], sems.at[0]

        @pl.when(j + 1 < nj)
        def _():
            _weight_chunk_copy(wr, stage, sem, spec, j + 1, i).wait()
            tile16[1 - slot, chunk_rows(spec, i), :] = stage[...].astype(BF16)

        @pl.when((j + 1 < nj) & (i + 1 < ni))
        def _():
            _weight_chunk_copy(wr, stage, sem, spec, j + 1, i + 1).start()

        @pl.when((i + 1 == ni) & (j + 2 < nj))
        def _():
            _weight_chunk_copy(wr, stage, sem, spec, j + 2, 0).start()


def _matmul(name, xs, pairs, epilogue, outs, *, tm, tn, extras=()):
    m = xs[0].shape[0]
    n = pairs[0][4]
    assert m % tm == 0 and n % tn == 0
    nj, ni = n // tn, m // tm
    grid = (nj, ni)

    in_specs = [pl.BlockSpec((tm, x.shape[1]), lambda j, i: (i, 0)) for x in xs]
    scratch, w_specs = [], []
    for _, w, kind, col0, ncols in pairs:
        assert ncols == n
        if w.dtype == BF16:
            assert col0 % tn == 0
            if kind == "kn":
                in_specs.append(pl.BlockSpec((w.shape[0], tn),
                                             lambda j, i, c0=col0 // tn: (0, c0 + j)))
            else:
                in_specs.append(pl.BlockSpec((tn, w.shape[1]),
                                             lambda j, i, c0=col0 // tn: (c0 + j, 0)))
            w_specs.append((kind, None))
            continue
        rows = w.shape[0] if kind == "kn" else tn
        lanes = tn if kind == "kn" else w.shape[1]
        assert rows % (ni * SUBLANES) == 0 and col0 % SUBLANES == 0
        if kind == "kn":
            assert col0 % LANES == 0 and tn % LANES == 0
        cr = rows // ni
        in_specs.append(pl.BlockSpec(memory_space=pl.ANY))
        scratch += [pltpu.VMEM((2, rows, lanes), BF16),
                    pltpu.VMEM((FIRST_TILE_DEPTH, cr, lanes), F32),
                    pltpu.SemaphoreType.DMA((FIRST_TILE_DEPTH,))]
        w_specs.append((kind, (col0, tn, cr)))
    for _, bs, im in extras:
        in_specs.append(pl.BlockSpec(bs, functools.partial(_swap_ji, im)))
    out_specs = [pl.BlockSpec(bs, functools.partial(_swap_ji, im)) for _, _, bs, im in outs]
    out_shape = [jax.ShapeDtypeStruct(s, d) for s, d, _, _ in outs]

    kern = functools.partial(_mm_kernel, pair_x=tuple(p[0] for p in pairs),
                             w_specs=tuple(w_specs), n_x=len(xs), n_extra=len(extras),
                             n_out=len(outs), epilogue=epilogue)
    return pl.pallas_call(
        kern, grid=grid, in_specs=in_specs, out_specs=out_specs, out_shape=out_shape,
        scratch_shapes=scratch, compiler_params=_cparams(2), name=name,
    )(*xs, *[p[1] for p in pairs], *[e[0] for e in extras])


def _swap_ji(im, j, i):
    return im(i, j)


def _tile_ij(i, j):
    return (i, j)


def _row_i(i, j):
    return (i, 0)


def _col_j(i, j):
    return (0, j)


def _sigmoid(x):
    return 1.0 / (1.0 + jnp.exp(-x))


def _silu(x):
    return x * _sigmoid(x)


def _ln_apply(y, mu, rstd, g, b):
    return (y - mu) * rstd * g + b


def _ln_kernel(y_ref, g_ref, b_ref, o_ref, *stat_refs):
    y = y_ref[...]
    mu = jnp.mean(y, axis=-1, keepdims=True)
    d = y - mu
    rstd = lax.rsqrt(jnp.mean(d * d, axis=-1, keepdims=True) + EPS)
    o_ref[...] = _ln_apply(y, mu, rstd, g_ref[...], b_ref[...]).astype(o_ref.dtype)
    if stat_refs:
        mu_ref, rstd_ref = stat_refs
        mu_ref[...] = jnp.broadcast_to(mu, mu_ref.shape)
        rstd_ref[...] = jnp.broadcast_to(rstd, rstd_ref.shape)


def _layer_norm(name, y, g, b, out_dtype, with_stats, tm=512):
    m, d = y.shape
    out_specs = [pl.BlockSpec((tm, d), lambda i: (i, 0))]
    out_shape = [jax.ShapeDtypeStruct((m, d), out_dtype)]
    if with_stats:
        out_specs += [pl.BlockSpec((tm, LANES), lambda i: (i, 0))] * 2
        out_shape += [jax.ShapeDtypeStruct((m, LANES), F32)] * 2
    return pl.pallas_call(
        _ln_kernel, grid=(m // tm,), name=name,
        in_specs=[pl.BlockSpec((tm, d), lambda i: (i, 0)),
                  pl.BlockSpec((1, d), lambda i: (0, 0)),
                  pl.BlockSpec((1, d), lambda i: (0, 0))],
        out_specs=out_specs, out_shape=out_shape,
        compiler_params=_cparams(1),
    )(y, g.reshape(1, d), b.reshape(1, d))


def _rope_tables(pos, invf):
    ang = pos.astype(F32) * invf
    lane = lax.broadcasted_iota(jnp.int32, ang.shape, 1)
    cos = jnp.cos(ang)
    sin = jnp.sin(ang)
    half = ROPE // 2
    return (cos, jnp.where(lane < half, -sin, 0.0),
            jnp.where((lane >= half) & (lane < ROPE), sin, 0.0))


def _rope_frequencies():
    half = ROPE // 2
    inv_freq = ROPE_THETA ** (-jnp.arange(half, dtype=F32) / half)
    return jnp.concatenate([inv_freq, inv_freq, jnp.zeros((LANES - ROPE,), F32)]).reshape(1, LANES)


def _rope_lanes(x, c, sa, sb):
    half = ROPE // 2
    return (x * c + pltpu.roll(x, LANES - half, axis=1) * sa
            + pltpu.roll(x, half, axis=1) * sb)


def _rms(x, g):
    ms = jnp.mean(x * x, axis=-1, keepdims=True)
    return x * lax.rsqrt(ms + EPS) * g


def _latent_epilogue(accs, extras):
    acc = accs[0]
    gq, gkv, pos, invf = extras
    c, sa, sb = _rope_tables(pos, invf)
    qn = _rms(acc[:, :Q_LORA], gq)
    kvn = _rms(acc[:, Q_LORA:Q_LORA + KV_LORA], gkv)
    kr = _rope_lanes(acc[:, Q_LORA + KV_LORA:], c, sa, sb)
    return [qn, kvn, kr, c, sa, sb]


def _uq_epilogue(accs, extras, *, scale):
    acc = accs[0]
    c, sa, sb = extras
    tn = acc.shape[1]
    cols = []
    for h in range(tn // HEAD_W):
        base = h * HEAD_W
        cols.append(acc[:, base:base + NOPE] * scale)
        cols.append(_rope_lanes(acc[:, base + NOPE:base + HEAD_W], c, sa, sb) * scale)
    return [jnp.concatenate(cols, axis=1)]


def _attn_kernel(q_ref, kn_ref, v_ref, kr_ref, o_ref, kcat_ref, *, tq, hpb):
    t = q_ref.shape[0]
    for hh in range(hpb):
        kcat_ref[hh, :, :NOPE] = kn_ref[:, hh * NOPE:(hh + 1) * NOPE]
        kcat_ref[hh, :, NOPE:] = kr_ref[...]
    row_chunk = lax.broadcasted_iota(jnp.int32, (tq, tq), 0) // CHUNK
    col_chunk = lax.broadcasted_iota(jnp.int32, (tq, tq), 1) // CHUNK
    visible = col_chunk <= row_chunk
    nt = (((1,), (1,)), ((), ()))
    for jq in range(t // tq):
        s0, s1 = jq * tq, (jq + 1) * tq
        for hh in range(hpb):
            vl = slice(hh * V_DIM, (hh + 1) * V_DIM)
            q = q_ref[s0:s1, hh * HEAD_W:(hh + 1) * HEAD_W]
            sd = lax.dot_general(q, kcat_ref[hh, s0:s1, :], nt, preferred_element_type=F32)
            sd = jnp.where(visible, sd, -jnp.inf)
            m = jnp.max(sd, axis=-1, keepdims=True)
            if jq > 0:
                sp = lax.dot_general(q, kcat_ref[hh, 0:s0, :], nt, preferred_element_type=F32)
                m = jnp.maximum(m, jnp.max(sp, axis=-1, keepdims=True))
            pd = jnp.exp(sd - m)
            l = jnp.sum(pd, axis=-1, keepdims=True)
            o = jnp.dot(pd.astype(BF16), v_ref[s0:s1, vl], preferred_element_type=F32)
            if jq > 0:
                pp = jnp.exp(sp - m)
                l = l + jnp.sum(pp, axis=-1, keepdims=True)
                o = o + jnp.dot(pp.astype(BF16), v_ref[0:s0, vl], preferred_element_type=F32)
            o_ref[s0:s1, vl] = (o / l).astype(o_ref.dtype)


def _attention(qf, kv, kr, batch, seq, heads, hpb=2):
    n = batch * seq
    kern = functools.partial(_attn_kernel, tq=ATT_TQ, hpb=hpb)
    groups = heads // hpb
    return pl.pallas_call(
        kern, grid=(batch, groups), name="mla_attention",
        in_specs=[pl.BlockSpec((seq, hpb * HEAD_W), lambda b, h: (b, h)),
                  pl.BlockSpec((seq, hpb * NOPE), lambda b, h: (b, h)),
                  pl.BlockSpec((seq, hpb * V_DIM), lambda b, h: (b, groups + h)),
                  pl.BlockSpec((seq, LANES), lambda b, h: (b, 0))],
        out_specs=pl.BlockSpec((seq, hpb * V_DIM), lambda b, h: (b, h)),
        out_shape=jax.ShapeDtypeStruct((n, heads * V_DIM), BF16),
        scratch_shapes=[pltpu.VMEM((hpb, seq, HEAD_W), BF16)],
        compiler_params=_cparams(2),
    )(qf, kv, kv, kr)


def _split3(x):
    a = x.astype(BF16)
    r = x - a.astype(F32)
    b = r.astype(BF16)
    c = (r - b.astype(F32)).astype(BF16)
    return a, b, c


def _chunk_rows(n):
    r0 = n * HG_CHUNK
    return pl.ds(r0 if isinstance(n, int) else pl.multiple_of(r0, HG_CHUNK), HG_CHUNK)


def _hgrn_decay_stage(n, hh, refs, masks):
    d = HG_DIM
    lanes = slice(hh * d, (hh + 1) * d)
    f = refs["f"][_chunk_rows(n), lanes]
    g = jnp.log(f) * LOG2_E
    cs = jnp.dot(masks[0], jnp.concatenate(_split3(g), axis=1), preferred_element_type=F32)
    b = cs[:, :d] + cs[:, d:2 * d] + cs[:, 2 * d:]
    refs["b"][n % 2, hh] = b
    refs["c"][n % 2, hh] = b - jnp.log(1.0 - f) * LOG2_E


def _hgrn_score_stage(n, hh, refs, masks):
    c, d, n_sub = HG_CHUNK, HG_DIM, HG_CHUNK // SUB
    m_diag, m_levels = masks[1], masks[2:]
    lanes = slice(hh * d, (hh + 1) * d)
    rows = _chunk_rows(n)
    slot = n % 2
    nt = (((1,), (1,)), ((), ()))

    q = refs["q"][rows, lanes].astype(F32)
    k = 1.0 - refs["f"][rows, lanes]
    b = refs["b"][slot, hh]
    b_last = b[c - 1:c, :]

    q3 = q.reshape(n_sub, SUB, d)
    b3 = b.reshape(n_sub, SUB, d)
    c3 = refs["c"][slot, hh].reshape(n_sub, SUB, d)
    a_ref = refs["a"]
    for s in range(SUB):
        lo = (s // SUBLANES) * SUBLANES
        a = q3[:, lo:, :] * jnp.exp2(jnp.minimum(b3[:, lo:, :] - c3[:, s:s + 1, :], 0.0))
        if lo:
            a = jnp.concatenate([jnp.zeros((n_sub, lo, d), F32), a], axis=1)
        a_ref[hh, :, s * d:(s + 1) * d] = a.reshape(c, d).astype(BF16)
    p_diag = jnp.dot(a_ref[hh], refs["sel"][...], preferred_element_type=F32)

    def level(hs):
        blk = 2 * hs
        qt, kt = [], []
        for i in range(0, c, hs):
            edge = (i // blk) * blk + hs - 1
            e = jnp.exp2(-jnp.abs(b[i:i + hs, :] - b[edge:edge + 1, :]))
            zero = jnp.zeros((hs, d), F32)
            upper = (i % blk) >= hs
            qt.append(q[i:i + hs, :] * e if upper else zero)
            kt.append(zero if upper else k[i:i + hs, :] * e)
        qt = jnp.concatenate(qt, axis=0).astype(BF16)
        kt = jnp.concatenate(kt, axis=0).astype(BF16)
        return lax.dot_general(qt, kt, nt, preferred_element_type=F32)

    p = level(c // 2)
    for idx in reversed(range(len(m_levels))):
        p = jnp.where(m_levels[idx], level(SUB << idx), p)
    p = jnp.where(m_diag, p_diag, p)
    refs["p"][slot, hh] = p.astype(BF16)
    refs["q0"][slot, hh] = (q * jnp.exp2(b)).astype(BF16)
    refs["kl"][slot, hh] = (k * jnp.exp2(b_last - b)).astype(BF16)
    refs["dl"][slot, hh] = jnp.broadcast_to(jnp.exp2(b_last), (SUBLANES, d))


def _hgrn_output_stage(n, hh, refs):
    d = HG_DIM
    lanes = slice(hh * d, (hh + 1) * d)
    rows = _chunk_rows(n)
    slot = n % 2
    nt = (((1,), (1,)), ((), ()))
    v = refs["i"][rows, lanes]
    st = refs["st"][hh]
    o = (jnp.dot(refs["p"][slot, hh], v, preferred_element_type=F32)
         + lax.dot_general(refs["q0"][slot, hh], st.astype(BF16), nt, preferred_element_type=F32))
    vt = v.astype(F32).T.astype(BF16)
    refs["st"][hh] = (refs["dl"][slot, hh][0:1, :] * st
                      + jnp.dot(vt, refs["kl"][slot, hh], preferred_element_type=F32))
    ms = jnp.mean(o * o, axis=-1, keepdims=True)
    ob = o * lax.rsqrt(ms + EPS) * refs["gn"][...] * refs["gs"][rows, lanes].astype(F32)
    refs["o"][rows, lanes] = ob.astype(refs["o"].dtype)


def _hgrn_kernel(q_ref, f_ref, i_ref, gs_ref, gn_ref, sel_ref, o_ref, st_ref, a_ref, b_ref,
                 c_ref, p_ref, q0_ref, kl_ref, dl_ref, *, hpb):
    nc = q_ref.shape[0] // HG_CHUNK
    assert nc >= 3
    c, d = HG_CHUNK, HG_DIM
    row = lax.broadcasted_iota(jnp.int32, (c, d), 0)
    col = lax.broadcasted_iota(jnp.int32, (c, d), 1)
    tri = (col <= row).astype(BF16)
    m_diag = ((row // SUB) == (col // SUB)) & ((col % SUB) <= (row % SUB))
    blks = [2 * SUB << i for i in range(64) if 2 * SUB << i < c]
    masks = (tri, m_diag) + tuple((row // blk) == (col // blk) for blk in blks)
    refs = dict(q=q_ref, f=f_ref, i=i_ref, gs=gs_ref, gn=gn_ref, sel=sel_ref, o=o_ref,
                st=st_ref, a=a_ref, b=b_ref, c=c_ref, p=p_ref, q0=q0_ref, kl=kl_ref, dl=dl_ref)

    st_ref[...] = jnp.zeros_like(st_ref)

    def run(n_out, n_score, n_decay):
        for hh in range(hpb):
            if n_out is not None:
                _hgrn_output_stage(n_out, hh, refs)
            if n_score is not None:
                _hgrn_score_stage(n_score, hh, refs, masks)
            if n_decay is not None:
                _hgrn_decay_stage(n_decay, hh, refs, masks)

    run(None, None, 0)
    run(None, 0, 1)

    def steady(n, carry):
        run(n - 2, n - 1, n)
        return carry

    lax.fori_loop(2, nc, steady, 0)
    run(nc - 2, nc - 1, None)
    run(nc - 1, None, None)


def _hgrn2(qs, f, iv, gs, gnorm, batch, seq, heads, hpb=4):
    n = batch * seq
    d = HG_DIM
    sel = (jnp.arange(SUB * d)[:, None] // d == jnp.arange(d)[None, :] % SUB).astype(BF16)
    spec = pl.BlockSpec((seq, hpb * d), lambda b, h: (b, h))
    return pl.pallas_call(
        functools.partial(_hgrn_kernel, hpb=hpb), grid=(batch, heads // hpb), name="hgrn2",
        in_specs=[spec, spec, spec, spec, pl.BlockSpec((1, d), lambda b, h: (0, 0)),
                  pl.BlockSpec((SUB * d, d), lambda b, h: (0, 0))],
        out_specs=spec,
        out_shape=jax.ShapeDtypeStruct((n, heads * d), BF16),
        scratch_shapes=[pltpu.VMEM((hpb, d, d), F32),
                        pltpu.VMEM((hpb, HG_CHUNK, SUB * d), BF16),
                        pltpu.VMEM((2, hpb, HG_CHUNK, d), F32),
                        pltpu.VMEM((2, hpb, HG_CHUNK, d), F32),
                        pltpu.VMEM((2, hpb, HG_CHUNK, d), BF16),
                        pltpu.VMEM((2, hpb, HG_CHUNK, d), BF16),
                        pltpu.VMEM((2, hpb, HG_CHUNK, d), BF16),
                        pltpu.VMEM((2, hpb, SUBLANES, d), F32)],
        compiler_params=_cparams(2),
    )(qs, f, iv, gs, gnorm.reshape(1, d), sel)


def _ep_silu(accs, extras):
    return [_silu(accs[0])]


def _ep_ident(accs, extras):
    return [accs[0]]


def _ep_forget(accs, extras):
    lbp = extras[0]
    m = jnp.max(lbp, axis=0, keepdims=True)
    e = jnp.exp(lbp - m)
    lb = e[0:1, :] / jnp.sum(e, axis=0, keepdims=True)
    return [lb + (1.0 - lb) * _sigmoid(accs[0])]


def _ep_merge(accs, extras):
    ga, a, gb, b = accs
    return [_sigmoid(ga) * a + _sigmoid(gb) * b]


def _ep_residual(accs, extras):
    y, mu, rstd, g, b = extras
    return [ALPHA * _ln_apply(y, mu[:, :1], rstd[:, :1], g, b) + accs[0]]


def _ep_swiglu(accs, extras):
    return [_silu(accs[0]) * accs[1]]


def _uq_weight_kernel(w_ref, o_ref):
    src = NOPE + ROPE
    pad = jnp.zeros((w_ref.shape[0], HEAD_W - src), o_ref.dtype)
    for h in range(MLA_HEADS):
        o_ref[:, h * HEAD_W:h * HEAD_W + src] = w_ref[:, h * src:(h + 1) * src].astype(o_ref.dtype)
        o_ref[:, h * HEAD_W + src:(h + 1) * HEAD_W] = pad


def _uq_weight(w_uq, tr=256):
    r = w_uq.shape[0]
    return pl.pallas_call(
        _uq_weight_kernel, grid=(r // tr,), name="q_up_weight_layout",
        in_specs=[pl.BlockSpec((tr, w_uq.shape[1]), lambda i: (i, 0))],
        out_specs=pl.BlockSpec((tr, MLA_HEADS * HEAD_W), lambda i: (i, 0)),
        out_shape=jax.ShapeDtypeStruct((r, MLA_HEADS * HEAD_W), BF16),
        compiler_params=_cparams(1),
    )(w_uq)


def _ukv_weight(w_ukv):
    r = w_ukv.shape[0]
    w = w_ukv.reshape(r, MLA_HEADS, NOPE + V_DIM)
    return jnp.concatenate([w[:, :, :NOPE].reshape(r, -1), w[:, :, NOPE:].reshape(r, -1)],
                           axis=1).astype(BF16)


def kernel(x, positions, ln_in_g, ln_in_b, w_in, q_norm_g, w_uq, kv_norm_g, w_ukv, hg_lb,
           hg_norm_g, w_branch_a, w_branch_b, w_out, ln1_g, ln1_b, w_gate, w_up, w_down,
           ln2_g, ln2_b):
    batch, seq, d = x.shape
    n = batch * seq
    assert w_in.shape[0] == DEPTH == 1
    l = 0
    wt = w_in.reshape(d, w_in.shape[-1]).T

    o_kr = Q_LORA + KV_LORA
    o_hq = o_kr + ROPE
    o_hf, o_hi, o_hg = o_hq + HG_WIDTH, o_hq + 2 * HG_WIDTH, o_hq + 3 * HG_WIDTH
    o_ga = o_hq + 4 * HG_WIDTH
    o_gb = o_ga + D_MODEL

    x2 = x.reshape(n, d)
    h16, h_mu, h_rstd = _layer_norm("ln_in", x2, ln_in_g, ln_in_b, BF16, True)

    def residual_specs(y, mu, rstd, g, b, tm, tn):
        return [(y, (tm, tn), _tile_ij), (mu, (tm, LANES), _row_i), (rstd, (tm, LANES), _row_i),
                (g.reshape(1, d), (1, tn), _col_j), (b.reshape(1, d), (1, tn), _col_j)]

    lat_w = o_kr + LANES
    w_lat = jnp.concatenate([wt[:o_hq], jnp.zeros((LANES - ROPE, d), wt.dtype)],
                            axis=0).astype(BF16)
    tm = 1024
    table = ((n, LANES), F32, (tm, LANES), _row_i)
    qn, kvn, kr, rc, rsa, rsb = _matmul(
        "latent_proj", [h16], [(0, w_lat, "nk", 0, lat_w)], _latent_epilogue,
        [((n, Q_LORA), BF16, (tm, Q_LORA), _row_i),
         ((n, KV_LORA), BF16, (tm, KV_LORA), _row_i),
         ((n, LANES), BF16, (tm, LANES), _row_i), table, table, table],
        tm=tm, tn=lat_w,
        extras=[(q_norm_g[l].reshape(1, -1), (1, Q_LORA), lambda i, j: (0, 0)),
                (kv_norm_g[l].reshape(1, -1), (1, KV_LORA), lambda i, j: (0, 0)),
                (positions.reshape(n, 1), (tm, 1), _row_i),
                (_rope_frequencies(), (1, LANES), lambda i, j: (0, 0))])
    rope_specs = [(rc, (tm, LANES), _row_i), (rsa, (tm, LANES), _row_i), (rsb, (tm, LANES), _row_i)]

    scale = (NOPE + ROPE) ** -0.5
    q_w = MLA_HEADS * HEAD_W
    (qf,) = _matmul("q_up_proj", [qn], [(0, _uq_weight(w_uq[l]), "kn", 0, q_w)],
                    functools.partial(_uq_epilogue, scale=scale),
                    [((n, q_w), BF16, (tm, 2048), _tile_ij)],
                    tm=tm, tn=2048, extras=rope_specs)
    kv_w = MLA_HEADS * (NOPE + V_DIM)
    (kv,) = _matmul("kv_up_proj", [kvn], [(0, _ukv_weight(w_ukv[l]), "kn", 0, kv_w)], _ep_ident,
                    [((n, kv_w), BF16, (2048, 2048), _tile_ij)], tm=2048, tn=2048)
    o_a = _attention(qf, kv, kr, batch, seq, MLA_HEADS)

    tm, tn = 1024, 1024

    def proj(name, col0, ep, dtype, extras=()):
        (r,) = _matmul(name, [h16], [(0, wt, "nk", col0, HG_WIDTH)], ep,
                       [((n, HG_WIDTH), dtype, (tm, tn), _tile_ij)],
                       tm=tm, tn=tn, extras=list(extras))
        return r

    qs = proj("hg_q_proj", o_hq, _ep_silu, BF16)
    fg = proj("hg_f_proj", o_hf, _ep_forget, F32, [(hg_lb, (hg_lb.shape[0], tn), _col_j)])
    iv = proj("hg_i_proj", o_hi, _ep_ident, BF16)
    gs = proj("hg_g_proj", o_hg, _ep_silu, BF16)
    o_b = _hgrn2(qs, fg, iv, gs, hg_norm_g[l], batch, seq, HG_HEADS)

    tm, tn = 512, 512
    (merged,) = _matmul(
        "gated_merge", [h16, o_a, o_b],
        [(0, wt, "nk", o_ga, d), (1, w_branch_a[l], "kn", 0, d),
         (0, wt, "nk", o_gb, d), (2, w_branch_b[l], "kn", 0, d)],
        _ep_merge, [((n, d), BF16, (tm, tn), _tile_ij)], tm=tm, tn=tn)

    tm, tn = 512, 1024
    (y1,) = _matmul("out_proj", [merged], [(0, w_out[l], "kn", 0, d)], _ep_residual,
                    [((n, d), F32, (tm, tn), _tile_ij)], tm=tm, tn=tn,
                    extras=residual_specs(x2, h_mu, h_rstd, ln_in_g, ln_in_b, tm, tn))
    h1_16, h1_mu, h1_rstd = _layer_norm("ln1", y1, ln1_g[l], ln1_b[l], BF16, True)

    tm, tn = 2048, 256
    (act,) = _matmul("swiglu_up", [h1_16], [(0, w_gate[l], "kn", 0, D_FF), (0, w_up[l], "kn", 0, D_FF)],
                     _ep_swiglu, [((n, D_FF), BF16, (tm, tn), _tile_ij)], tm=tm, tn=tn)
    tm, tn = 512, 512
    (y2,) = _matmul("swiglu_down", [act], [(0, w_down[l], "kn", 0, d)], _ep_residual,
                    [((n, d), F32, (tm, tn), _tile_ij)], tm=tm, tn=tn,
                    extras=residual_specs(y1, h1_mu, h1_rstd, ln1_g[l], ln1_b[l], tm, tn))
    (out,) = _layer_norm("ln2", y2, ln2_g[l], ln2_b[l], F32, False)
    return out.reshape(batch, seq, d)
```

```python
import functools

import jax
import jax.numpy as jnp
from jax import lax
from jax.experimental import pallas as pl
from jax.experimental.pallas import tpu as pltpu

F32 = jnp.float32
BF16 = jnp.bfloat16

D_MODEL = 4096
CHUNK = 64
MLA_HEADS = 16
NOPE = 128
ROPE = 64
V_DIM = 128
Q_LORA = 1024
KV_LORA = 512
ROPE_THETA = 10000.0
HG_HEADS = 16
HG_DIM = 128
HG_WIDTH = HG_HEADS * HG_DIM
D_FF = 11008
DEPTH = 1
ALPHA = (2 * DEPTH) ** 0.25
EPS = 1e-5
LOG2_E = 1.4426950408889634

LANES = 128
SUBLANES = 8
VMEM_LIMIT = 60 * 1024 * 1024

HEAD_W = 2 * LANES
SUB = 8
HG_CHUNK = 128
ATT_TQ = 512
FIRST_TILE_DEPTH = 4


def _cparams(n_grid, vmem=VMEM_LIMIT):
    return pltpu.CompilerParams(dimension_semantics=("arbitrary",) * n_grid,
                                vmem_limit_bytes=vmem)


def _weight_chunk_copy(w_hbm, stage, sem, spec, jj, ci):
    kind, (col0, tn, cr) = spec
    if kind == "kn":
        src = w_hbm.at[pl.ds(pl.multiple_of(ci * cr, SUBLANES), cr),
                       pl.ds(pl.multiple_of(col0 + jj * tn, LANES), tn)]
    else:
        src = w_hbm.at[pl.ds(pl.multiple_of(col0 + jj * tn + ci * cr, SUBLANES), cr), :]
    return pltpu.make_async_copy(src, stage, sem)


def _mm_kernel(*refs, pair_x, w_specs, n_x, n_extra, n_out, epilogue):
    n_pairs = len(pair_x)
    xs = refs[:n_x]
    w_refs = refs[n_x:n_x + n_pairs]
    extras = refs[n_x + n_pairs:n_x + n_pairs + n_extra]
    outs = refs[n_x + n_pairs + n_extra:n_x + n_pairs + n_extra + n_out]
    scratch = list(refs[n_x + n_pairs + n_extra + n_out:])
    j, i = pl.program_id(0), pl.program_id(1)
    nj, ni = pl.num_programs(0), pl.num_programs(1)
    slot = j % 2

    streamed = []
    tiles = []
    for wr, spec in zip(w_refs, w_specs):
        if spec[1] is None:
            tiles.append(wr)
        else:
            tile16, stages, sems = scratch[:3]
            del scratch[:3]
            streamed.append((wr, spec, tile16, stages, sems))
            tiles.append(tile16.at[slot])

    def chunk_rows(spec, ci):
        cr = spec[1][2]
        return pl.ds(pl.multiple_of(ci * cr, SUBLANES), cr)

    @pl.when((j == 0) & (i == 0))
    def _():
        depth = FIRST_TILE_DEPTH

        def first(k, ci):
            wr, spec, _, stages, sems = streamed[k]
            b = ci % depth
            return _weight_chunk_copy(wr, stages.at[b], sems.at[b], spec, 0, ci)

        for ci in range(depth - 1):
            @pl.when(ci < ni)
            def _():
                for k in range(len(streamed)):
                    first(k, ci).start()

        def load(ci, carry):
            for k, (_, spec, tile16, stages, _) in enumerate(streamed):
                @pl.when(ci + depth - 1 < ni)
                def _():
                    first(k, ci + depth - 1).start()

                first(k, ci).wait()
                tile16[0, chunk_rows(spec, ci), :] = stages[ci % depth].astype(BF16)
            return carry
        lax.fori_loop(0, ni, load, 0)

        for wr, spec, _, stages, sems in streamed:
            @pl.when(nj > 1)
            def _():
                _weight_chunk_copy(wr, stages.at[0], sems.at[0], spec, 1, 0).start()

    accs = []
    for xi, w, spec in zip(pair_x, tiles, w_specs):
        if spec[0] == "kn":
            accs.append(jnp.dot(xs[xi][...], w[...], preferred_element_type=F32))
        else:
            accs.append(lax.dot_general(xs[xi][...], w[...], (((1,), (1,)), ((), ())),
                                        preferred_element_type=F32))
    res = epilogue(accs, [e[...] for e in extras])
    for o, r in zip(outs, res):
        o[...] = r.astype(o.dtype)

    for wr, spec, tile16, stages, sems in streamed:
        stage, sem = stages.at[0], sems.at[0]

        @pl.when(j + 1 < nj)
        def _():
            _weight_chunk_copy(wr, stage, sem, spec, j + 1, i).wait()
            tile16[1 - slot, chunk_rows(spec, i), :] = stage[...].astype(BF16)

        @pl.when((j + 1 < nj) & (i + 1 < ni))
        def _():
            _weight_chunk_copy(wr, stage, sem, spec, j + 1, i + 1).start()

        @pl.when((i + 1 == ni) & (j + 2 < nj))
        def _():
            _weight_chunk_copy(wr, stage, sem, spec, j + 2, 0).start()


def _matmul(name, xs, pairs, epilogue, outs, *, tm, tn, extras=()):
    m = xs[0].shape[0]
    n = pairs[0][4]
    assert m % tm == 0 and n % tn == 0
    nj, ni = n // tn, m // tm
    grid = (nj, ni)

    in_specs = [pl.BlockSpec((tm, x.shape[1]), lambda j, i: (i, 0)) for x in xs]
    scratch, w_specs = [], []
    for _, w, kind, col0, ncols in pairs:
        assert ncols == n
        if w.dtype == BF16:
            assert col0 % tn == 0
            if kind == "kn":
                in_specs.append(pl.BlockSpec((w.shape[0], tn),
                                             lambda j, i, c0=col0 // tn: (0, c0 + j)))
            else:
                in_specs.append(pl.BlockSpec((tn, w.shape[1]),
                                             lambda j, i, c0=col0 // tn: (c0 + j, 0)))
            w_specs.append((kind, None))
            continue
        rows = w.shape[0] if kind == "kn" else tn
        lanes = tn if kind == "kn" else w.shape[1]
        assert rows % (ni * SUBLANES) == 0 and col0 % SUBLANES == 0
        if kind == "kn":
            assert col0 % LANES == 0 and tn % LANES == 0
        cr = rows // ni
        in_specs.append(pl.BlockSpec(memory_space=pl.ANY))
        scratch += [pltpu.VMEM((2, rows, lanes), BF16),
                    pltpu.VMEM((FIRST_TILE_DEPTH, cr, lanes), F32),
                    pltpu.SemaphoreType.DMA((FIRST_TILE_DEPTH,))]
        w_specs.append((kind, (col0, tn, cr)))
    for _, bs, im in extras:
        in_specs.append(pl.BlockSpec(bs, functools.partial(_swap_ji, im)))
    out_specs = [pl.BlockSpec(bs, functools.partial(_swap_ji, im)) for _, _, bs, im in outs]
    out_shape = [jax.ShapeDtypeStruct(s, d) for s, d, _, _ in outs]

    kern = functools.partial(_mm_kernel, pair_x=tuple(p[0] for p in pairs),
                             w_specs=tuple(w_specs), n_x=len(xs), n_extra=len(extras),
                             n_out=len(outs), epilogue=epilogue)
    return pl.pallas_call(
        kern, grid=grid, in_specs=in_specs, out_specs=out_specs, out_shape=out_shape,
        scratch_shapes=scratch, compiler_params=_cparams(2), name=name,
    )(*xs, *[p[1] for p in pairs], *[e[0] for e in extras])


def _swap_ji(im, j, i):
    return im(i, j)


def _tile_ij(i, j):
    return (i, j)


def _row_i(i, j):
    return (i, 0)


def _col_j(i, j):
    return (0, j)


def _sigmoid(x):
    return 1.0 / (1.0 + jnp.exp(-x))


def _silu(x):
    return x * _sigmoid(x)


def _ln_apply(y, mu, rstd, g, b):
    return (y - mu) * rstd * g + b


def _ln_kernel(y_ref, g_ref, b_ref, o_ref, *stat_refs):
    y = y_ref[...]
    mu = jnp.mean(y, axis=-1, keepdims=True)
    d = y - mu
    rstd = lax.rsqrt(jnp.mean(d * d, axis=-1, keepdims=True) + EPS)
    o_ref[...] = _ln_apply(y, mu, rstd, g_ref[...], b_ref[...]).astype(o_ref.dtype)
    if stat_refs:
        mu_ref, rstd_ref = stat_refs
        mu_ref[...] = jnp.broadcast_to(mu, mu_ref.shape)
        rstd_ref[...] = jnp.broadcast_to(rstd, rstd_ref.shape)


def _layer_norm(name, y, g, b, out_dtype, with_stats, tm=512):
    m, d = y.shape
    out_specs = [pl.BlockSpec((tm, d), lambda i: (i, 0))]
    out_shape = [jax.ShapeDtypeStruct((m, d), out_dtype)]
    if with_stats:
        out_specs += [pl.BlockSpec((tm, LANES), lambda i: (i, 0))] * 2
        out_shape += [jax.ShapeDtypeStruct((m, LANES), F32)] * 2
    return pl.pallas_call(
        _ln_kernel, grid=(m // tm,), name=name,
        in_specs=[pl.BlockSpec((tm, d), lambda i: (i, 0)),
                  pl.BlockSpec((1, d), lambda i: (0, 0)),
                  pl.BlockSpec((1, d), lambda i: (0, 0))],
        out_specs=out_specs, out_shape=out_shape,
        compiler_params=_cparams(1),
    )(y, g.reshape(1, d), b.reshape(1, d))


def _rope_tables(pos, invf):
    ang = pos.astype(F32) * invf
    lane = lax.broadcasted_iota(jnp.int32, ang.shape, 1)
    cos = jnp.cos(ang)
    sin = jnp.sin(ang)
    half = ROPE // 2
    return (cos, jnp.where(lane < half, -sin, 0.0),
            jnp.where((lane >= half) & (lane < ROPE), sin, 0.0))


def _rope_frequencies():
    half = ROPE // 2
    inv_freq = ROPE_THETA ** (-jnp.arange(half, dtype=F32) / half)
    return jnp.concatenate([inv_freq, inv_freq, jnp.zeros((LANES - ROPE,), F32)]).reshape(1, LANES)


def _rope_lanes(x, c, sa, sb):
    half = ROPE // 2
    return (x * c + pltpu.roll(x, LANES - half, axis=1) * sa
            + pltpu.roll(x, half, axis=1) * sb)


def _rms(x, g):
    ms = jnp.mean(x * x, axis=-1, keepdims=True)
    return x * lax.rsqrt(ms + EPS) * g


def _latent_epilogue(accs, extras):
    acc = accs[0]
    gq, gkv, pos, invf = extras
    c, sa, sb = _rope_tables(pos, invf)
    qn = _rms(acc[:, :Q_LORA], gq)
    kvn = _rms(acc[:, Q_LORA:Q_LORA + KV_LORA], gkv)
    kr = _rope_lanes(acc[:, Q_LORA + KV_LORA:], c, sa, sb)
    return [qn, kvn, kr, c, sa, sb]


def _uq_epilogue(accs, extras, *, scale):
    acc = accs[0]
    c, sa, sb = extras
    tn = acc.shape[1]
    cols = []
    for h in range(tn // HEAD_W):
        base = h * HEAD_W
        cols.append(acc[:, base:base + NOPE] * scale)
        cols.append(_rope_lanes(acc[:, base + NOPE:base + HEAD_W], c, sa, sb) * scale)
    return [jnp.concatenate(cols, axis=1)]


def _attn_kernel(q_ref, kn_ref, v_ref, kr_ref, o_ref, kcat_ref, *, tq, hpb):
    t = q_ref.shape[0]
    for hh in range(hpb):
        kcat_ref[hh, :, :NOPE] = kn_ref[:, hh * NOPE:(hh + 1) * NOPE]
        kcat_ref[hh, :, NOPE:] = kr_ref[...]
    row_chunk = lax.broadcasted_iota(jnp.int32, (tq, tq), 0) // CHUNK
    col_chunk = lax.broadcasted_iota(jnp.int32, (tq, tq), 1) // CHUNK
    visible = col_chunk <= row_chunk
    nt = (((1,), (1,)), ((), ()))
    for jq in range(t // tq):
        s0, s1 = jq * tq, (jq + 1) * tq
        for hh in range(hpb):
            vl = slice(hh * V_DIM, (hh + 1) * V_DIM)
            q = q_ref[s0:s1, hh * HEAD_W:(hh + 1) * HEAD_W]
            sd = lax.dot_general(q, kcat_ref[hh, s0:s1, :], nt, preferred_element_type=F32)
            sd = jnp.where(visible, sd, -jnp.inf)
            m = jnp.max(sd, axis=-1, keepdims=True)
            if jq > 0:
                sp = lax.dot_general(q, kcat_ref[hh, 0:s0, :], nt, preferred_element_type=F32)
                m = jnp.maximum(m, jnp.max(sp, axis=-1, keepdims=True))
            pd = jnp.exp(sd - m)
            l = jnp.sum(pd, axis=-1, keepdims=True)
            o = jnp.dot(pd.astype(BF16), v_ref[s0:s1, vl], preferred_element_type=F32)
            if jq > 0:
                pp = jnp.exp(sp - m)
                l = l + jnp.sum(pp, axis=-1, keepdims=True)
                o = o + jnp.dot(pp.astype(BF16), v_ref[0:s0, vl], preferred_element_type=F32)
            o_ref[s0:s1, vl] = (o / l).astype(o_ref.dtype)


def _attention(qf, kv, kr, batch, seq, heads, hpb=4):
    n = batch * seq
    kern = functools.partial(_attn_kernel, tq=ATT_TQ, hpb=hpb)
    groups = heads // hpb
    return pl.pallas_call(
        kern, grid=(batch, groups), name="mla_attention",
        in_specs=[pl.BlockSpec((seq, hpb * HEAD_W), lambda b, h: (b, h)),
                  pl.BlockSpec((seq, hpb * NOPE), lambda b, h: (b, h)),
                  pl.BlockSpec((seq, hpb * V_DIM), lambda b, h: (b, groups + h)),
                  pl.BlockSpec((seq, LANES), lambda b, h: (b, 0))],
        out_specs=pl.BlockSpec((seq, hpb * V_DIM), lambda b, h: (b, h)),
        out_shape=jax.ShapeDtypeStruct((n, heads * V_DIM), BF16),
        scratch_shapes=[pltpu.VMEM((hpb, seq, HEAD_W), BF16)],
        compiler_params=_cparams(2),
    )(qf, kv, kv, kr)


def _split3(x):
    a = x.astype(BF16)
    r = x - a.astype(F32)
    b = r.astype(BF16)
    c = (r - b.astype(F32)).astype(BF16)
    return a, b, c


def _chunk_rows(n):
    r0 = n * HG_CHUNK
    return pl.ds(r0 if isinstance(n, int) else pl.multiple_of(r0, HG_CHUNK), HG_CHUNK)


def _hgrn_decay_stage(n, hh, refs, masks):
    d = HG_DIM
    lanes = slice(hh * d, (hh + 1) * d)
    f = refs["f"][_chunk_rows(n), lanes]
    g = jnp.log(f) * LOG2_E
    cs = jnp.dot(masks[0], jnp.concatenate(_split3(g), axis=1), preferred_element_type=F32)
    b = cs[:, :d] + cs[:, d:2 * d] + cs[:, 2 * d:]
    refs["b"][n % 2, hh] = b
    refs["c"][n % 2, hh] = b - jnp.log(1.0 - f) * LOG2_E


def _hgrn_score_stage(n, hh, refs, masks):
    c, d, n_sub = HG_CHUNK, HG_DIM, HG_CHUNK // SUB
    m_diag, m_levels = masks[1], masks[2:]
    lanes = slice(hh * d, (hh + 1) * d)
    rows = _chunk_rows(n)
    slot = n % 2
    nt = (((1,), (1,)), ((), ()))

    q = refs["q"][rows, lanes].astype(F32)
    k = 1.0 - refs["f"][rows, lanes]
    b = refs["b"][slot, hh]
    b_last = b[c - 1:c, :]

    q3 = q.reshape(n_sub, SUB, d)
    b3 = b.reshape(n_sub, SUB, d)
    c3 = refs["c"][slot, hh].reshape(n_sub, SUB, d)
    a_ref = refs["a"]
    for s in range(SUB):
        lo = (s // SUBLANES) * SUBLANES
        a = q3[:, lo:, :] * jnp.exp2(jnp.minimum(b3[:, lo:, :] - c3[:, s:s + 1, :], 0.0))
        if lo:
            a = jnp.concatenate([jnp.zeros((n_sub, lo, d), F32), a], axis=1)
        a_ref[hh, :, s * d:(s + 1) * d] = a.reshape(c, d).astype(BF16)
    p_diag = jnp.dot(a_ref[hh], refs["sel"][...], preferred_element_type=F32)

    def level(hs):
        blk = 2 * hs
        qt, kt = [], []
        for i in range(0, c, hs):
            edge = (i // blk) * blk + hs - 1
            e = jnp.exp2(-jnp.abs(b[i:i + hs, :] - b[edge:edge + 1, :]))
            zero = jnp.zeros((hs, d), F32)
            upper = (i % blk) >= hs
            qt.append(q[i:i + hs, :] * e if upper else zero)
            kt.append(zero if upper else k[i:i + hs, :] * e)
        qt = jnp.concatenate(qt, axis=0).astype(BF16)
        kt = jnp.concatenate(kt, axis=0).astype(BF16)
        return lax.dot_general(qt, kt, nt, preferred_element_type=F32)

    p = level(c // 2)
    for idx in reversed(range(len(m_levels))):
        p = jnp.where(m_levels[idx], level(SUB << idx), p)
    p = jnp.where(m_diag, p_diag, p)
    refs["p"][slot, hh] = p.astype(BF16)
    refs["q0"][slot, hh] = (q * jnp.exp2(b)).astype(BF16)
    refs["kl"][slot, hh] = (k * jnp.exp2(b_last - b)).astype(BF16)
    refs["dl"][slot, hh] = jnp.broadcast_to(jnp.exp2(b_last), (SUBLANES, d))


def _hgrn_output_stage(n, hh, refs):
    d = HG_DIM
    lanes = slice(hh * d, (hh + 1) * d)
    rows = _chunk_rows(n)
    slot = n % 2
    nt = (((1,), (1,)), ((), ()))
    v = refs["i"][rows, lanes]
    st = refs["st"][hh]
    o = (jnp.dot(refs["p"][slot, hh], v, preferred_element_type=F32)
         + lax.dot_general(refs["q0"][slot, hh], st.astype(BF16), nt, preferred_element_type=F32))
    vt = v.astype(F32).T.astype(BF16)
    refs["st"][hh] = (refs["dl"][slot, hh][0:1, :] * st
                      + jnp.dot(vt, refs["kl"][slot, hh], preferred_element_type=F32))
    ms = jnp.mean(o * o, axis=-1, keepdims=True)
    ob = o * lax.rsqrt(ms + EPS) * refs["gn"][...] * refs["gs"][rows, lanes].astype(F32)
    refs["o"][rows, lanes] = ob.astype(refs["o"].dtype)


def _hgrn_kernel(q_ref, f_ref, i_ref, gs_ref, gn_ref, sel_ref, o_ref, st_ref, a_ref, b_ref,
                 c_ref, p_ref, q0_ref, kl_ref, dl_ref, *, hpb):
    nc = q_ref.shape[0] // HG_CHUNK
    assert nc >= 3
    c, d = HG_CHUNK, HG_DIM
    row = lax.broadcasted_iota(jnp.int32, (c, d), 0)
    col = lax.broadcasted_iota(jnp.int32, (c, d), 1)
    tri = (col <= row).astype(BF16)
    m_diag = ((row // SUB) == (col // SUB)) & ((col % SUB) <= (row % SUB))
    blks = [2 * SUB << i for i in range(64) if 2 * SUB << i < c]
    masks = (tri, m_diag) + tuple((row // blk) == (col // blk) for blk in blks)
    refs = dict(q=q_ref, f=f_ref, i=i_ref, gs=gs_ref, gn=gn_ref, sel=sel_ref, o=o_ref,
                st=st_ref, a=a_ref, b=b_ref, c=c_ref, p=p_ref, q0=q0_ref, kl=kl_ref, dl=dl_ref)

    st_ref[...] = jnp.zeros_like(st_ref)

    def run(n_out, n_score, n_decay):
        for hh in range(hpb):
            if n_out is not None:
                _hgrn_output_stage(n_out, hh, refs)
            if n_score is not None:
                _hgrn_score_stage(n_score, hh, refs, masks)
            if n_decay is not None:
                _hgrn_decay_stage(n_decay, hh, refs, masks)

    run(None, None, 0)
    run(None, 0, 1)

    def steady(n, carry):
        run(n - 2, n - 1, n)
        return carry

    lax.fori_loop(2, nc, steady, 0)
    run(nc - 2, nc - 1, None)
    run(nc - 1, None, None)


def _hgrn2(qs, f, iv, gs, gnorm, batch, seq, heads, hpb=4):
    n = batch * seq
    d = HG_DIM
    sel = (jnp.arange(SUB * d)[:, None] // d == jnp.arange(d)[None, :] % SUB).astype(BF16)
    spec = pl.BlockSpec((seq, hpb * d), lambda b, h: (b, h))
    return pl.pallas_call(
        functools.partial(_hgrn_kernel, hpb=hpb), grid=(batch, heads // hpb), name="hgrn2",
        in_specs=[spec, spec, spec, spec, pl.BlockSpec((1, d), lambda b, h: (0, 0)),
                  pl.BlockSpec((SUB * d, d), lambda b, h: (0, 0))],
        out_specs=spec,
        out_shape=jax.ShapeDtypeStruct((n, heads * d), BF16),
        scratch_shapes=[pltpu.VMEM((hpb, d, d), F32),
                        pltpu.VMEM((hpb, HG_CHUNK, SUB * d), BF16),
                        pltpu.VMEM((2, hpb, HG_CHUNK, d), F32),
                        pltpu.VMEM((2, hpb, HG_CHUNK, d), F32),
                        pltpu.VMEM((2, hpb, HG_CHUNK, d), BF16),
                        pltpu.VMEM((2, hpb, HG_CHUNK, d), BF16),
                        pltpu.VMEM((2, hpb, HG_CHUNK, d), BF16),
                        pltpu.VMEM((2, hpb, SUBLANES, d), F32)],
        compiler_params=_cparams(2),
    )(qs, f, iv, gs, gnorm.reshape(1, d), sel)


def _ep_silu(accs, extras):
    return [_silu(accs[0])]


def _ep_ident(accs, extras):
    return [accs[0]]


def _ep_forget(accs, extras):
    lbp = extras[0]
    m = jnp.max(lbp, axis=0, keepdims=True)
    e = jnp.exp(lbp - m)
    lb = e[0:1, :] / jnp.sum(e, axis=0, keepdims=True)
    return [lb + (1.0 - lb) * _sigmoid(accs[0])]


def _ep_merge(accs, extras):
    ga, a, gb, b = accs
    return [_sigmoid(ga) * a + _sigmoid(gb) * b]


def _ep_residual(accs, extras):
    y, mu, rstd, g, b = extras
    return [ALPHA * _ln_apply(y, mu[:, :1], rstd[:, :1], g, b) + accs[0]]


def _ep_swiglu(accs, extras):
    return [_silu(accs[0]) * accs[1]]


def _uq_weight_kernel(w_ref, o_ref):
    src = NOPE + ROPE
    pad = jnp.zeros((w_ref.shape[0], HEAD_W - src), o_ref.dtype)
    for h in range(MLA_HEADS):
        o_ref[:, h * HEAD_W:h * HEAD_W + src] = w_ref[:, h * src:(h + 1) * src].astype(o_ref.dtype)
        o_ref[:, h * HEAD_W + src:(h + 1) * HEAD_W] = pad


def _uq_weight(w_uq, tr=256):
    r = w_uq.shape[0]
    return pl.pallas_call(
        _uq_weight_kernel, grid=(r // tr,), name="q_up_weight_layout",
        in_specs=[pl.BlockSpec((tr, w_uq.shape[1]), lambda i: (i, 0))],
        out_specs=pl.BlockSpec((tr, MLA_HEADS * HEAD_W), lambda i: (i, 0)),
        out_shape=jax.ShapeDtypeStruct((r, MLA_HEADS * HEAD_W), BF16),
        compiler_params=_cparams(1),
    )(w_uq)


def _ukv_weight(w_ukv):
    r = w_ukv.shape[0]
    w = w_ukv.reshape(r, MLA_HEADS, NOPE + V_DIM)
    return jnp.concatenate([w[:, :, :NOPE].reshape(r, -1), w[:, :, NOPE:].reshape(r, -1)],
                           axis=1).astype(BF16)


def kernel(x, positions, ln_in_g, ln_in_b, w_in, q_norm_g, w_uq, kv_norm_g, w_ukv, hg_lb,
           hg_norm_g, w_branch_a, w_branch_b, w_out, ln1_g, ln1_b, w_gate, w_up, w_down,
           ln2_g, ln2_b):
    batch, seq, d = x.shape
    n = batch * seq
    assert w_in.shape[0] == DEPTH == 1
    l = 0
    wt = w_in.reshape(d, w_in.shape[-1]).T

    o_kr = Q_LORA + KV_LORA
    o_hq = o_kr + ROPE
    o_hf, o_hi, o_hg = o_hq + HG_WIDTH, o_hq + 2 * HG_WIDTH, o_hq + 3 * HG_WIDTH
    o_ga = o_hq + 4 * HG_WIDTH
    o_gb = o_ga + D_MODEL

    x2 = x.reshape(n, d)
    h16, h_mu, h_rstd = _layer_norm("ln_in", x2, ln_in_g, ln_in_b, BF16, True)

    def residual_specs(y, mu, rstd, g, b, tm, tn):
        return [(y, (tm, tn), _tile_ij), (mu, (tm, LANES), _row_i), (rstd, (tm, LANES), _row_i),
                (g.reshape(1, d), (1, tn), _col_j), (b.reshape(1, d), (1, tn), _col_j)]

    lat_w = o_kr + LANES
    w_lat = jnp.concatenate([wt[:o_hq], jnp.zeros((LANES - ROPE, d), wt.dtype)],
                            axis=0).astype(BF16)
    tm = 1024
    table = ((n, LANES), F32, (tm, LANES), _row_i)
    qn, kvn, kr, rc, rsa, rsb = _matmul(
        "latent_proj", [h16], [(0, w_lat, "nk", 0, lat_w)], _latent_epilogue,
        [((n, Q_LORA), BF16, (tm, Q_LORA), _row_i),
         ((n, KV_LORA), BF16, (tm, KV_LORA), _row_i),
         ((n, LANES), BF16, (tm, LANES), _row_i), table, table, table],
        tm=tm, tn=lat_w,
        extras=[(q_norm_g[l].reshape(1, -1), (1, Q_LORA), lambda i, j: (0, 0)),
                (kv_norm_g[l].reshape(1, -1), (1, KV_LORA), lambda i, j: (0, 0)),
                (positions.reshape(n, 1), (tm, 1), _row_i),
                (_rope_frequencies(), (1, LANES), lambda i, j: (0, 0))])
    rope_specs = [(rc, (tm, LANES), _row_i), (rsa, (tm, LANES), _row_i), (rsb, (tm, LANES), _row_i)]

    scale = (NOPE + ROPE) ** -0.5
    q_w = MLA_HEADS * HEAD_W
    (qf,) = _matmul("q_up_proj", [qn], [(0, _uq_weight(w_uq[l]), "kn", 0, q_w)],
                    functools.partial(_uq_epilogue, scale=scale),
                    [((n, q_w), BF16, (tm, 2048), _tile_ij)],
                    tm=tm, tn=2048, extras=rope_specs)
    kv_w = MLA_HEADS * (NOPE + V_DIM)
    (kv,) = _matmul("kv_up_proj", [kvn], [(0, _ukv_weight(w_ukv[l]), "kn", 0, kv_w)], _ep_ident,
                    [((n, kv_w), BF16, (2048, 2048), _tile_ij)], tm=2048, tn=2048)
    o_a = _attention(qf, kv, kr, batch, seq, MLA_HEADS)

    tm, tn = 1024, 1024

    def proj(name, col0, ep, dtype, extras=()):
        (r,) = _matmul(name, [h16], [(0, wt, "nk", col0, HG_WIDTH)], ep,
                       [((n, HG_WIDTH), dtype, (tm, tn), _tile_ij)],
                       tm=tm, tn=tn, extras=list(extras))
        return r

    qs = proj("hg_q_proj", o_hq, _ep_silu, BF16)
    fg = proj("hg_f_proj", o_hf, _ep_forget, F32, [(hg_lb, (hg_lb.shape[0], tn), _col_j)])
    iv = proj("hg_i_proj", o_hi, _ep_ident, BF16)
    gs = proj("hg_g_proj", o_hg, _ep_silu, BF16)
    o_b = _hgrn2(qs, fg, iv, gs, hg_norm_g[l], batch, seq, HG_HEADS)

    tm, tn = 512, 512
    (merged,) = _matmul(
        "gated_merge", [h16, o_a, o_b],
        [(0, wt, "nk", o_ga, d), (1, w_branch_a[l], "kn", 0, d),
         (0, wt, "nk", o_gb, d), (2, w_branch_b[l], "kn", 0, d)],
        _ep_merge, [((n, d), BF16, (tm, tn), _tile_ij)], tm=tm, tn=tn)

    tm, tn = 512, 1024
    (y1,) = _matmul("out_proj", [merged], [(0, w_out[l], "kn", 0, d)], _ep_residual,
                    [((n, d), F32, (tm, tn), _tile_ij)], tm=tm, tn=tn,
                    extras=residual_specs(x2, h_mu, h_rstd, ln_in_g, ln_in_b, tm, tn))
    h1_16, h1_mu, h1_rstd = _layer_norm("ln1", y1, ln1_g[l], ln1_b[l], BF16, True)

    tm, tn = 2048, 256
    (act,) = _matmul("swiglu_up", [h1_16], [(0, w_gate[l], "kn", 0, D_FF), (0, w_up[l], "kn", 0, D_FF)],
                     _ep_swiglu, [((n, D_FF), BF16, (tm, tn), _tile_ij)], tm=tm, tn=tn)
    tm, tn = 512, 512
    (y2,) = _matmul("swiglu_down", [act], [(0, w_down[l], "kn", 0, d)], _ep_residual,
                    [((n, d), F32, (tm, tn), _tile_ij)], tm=tm, tn=tn,
                    extras=residual_specs(y1, h1_mu, h1_rstd, ln1_g[l], ln1_b[l], tm, tn))
    (out,) = _layer_norm("ln2", y2, ln2_g[l], ln2_b[l], F32, False)
    return out.reshape(batch, seq, d)
```

```python
import functools

import jax
import jax.numpy as jnp
from jax import lax
from jax.experimental import pallas as pl
from jax.experimental.pallas import tpu as pltpu

F32 = jnp.float32
BF16 = jnp.bfloat16

D_MODEL = 4096
CHUNK = 64
MLA_HEADS = 16
NOPE = 128
ROPE = 64
V_DIM = 128
Q_LORA = 1024
KV_LORA = 512
ROPE_THETA = 10000.0
HG_HEADS = 16
HG_DIM = 128
HG_WIDTH = HG_HEADS * HG_DIM
D_FF = 11008
DEPTH = 1
ALPHA = (2 * DEPTH) ** 0.25
EPS = 1e-5
LOG2_E = 1.4426950408889634

LANES = 128
SUBLANES = 8
VMEM_LIMIT = 60 * 1024 * 1024

HEAD_W = 2 * LANES
KV_W = NOPE + V_DIM
SUB = 8
HG_CHUNK = 128
ATT_TQ = 512
FIRST_TILE_DEPTH = 4


def _cparams(n_grid, vmem=VMEM_LIMIT):
    return pltpu.CompilerParams(dimension_semantics=("arbitrary",) * n_grid,
                                vmem_limit_bytes=vmem)


def _weight_chunk_copy(w_hbm, stage, sem, spec, jj, ci):
    kind, (col0, tn, cr) = spec
    if kind == "kn":
        src = w_hbm.at[pl.ds(pl.multiple_of(ci * cr, SUBLANES), cr),
                       pl.ds(pl.multiple_of(col0 + jj * tn, LANES), tn)]
    else:
        src = w_hbm.at[pl.ds(pl.multiple_of(col0 + jj * tn + ci * cr, SUBLANES), cr), :]
    return pltpu.make_async_copy(src, stage, sem)


def _mm_kernel(*refs, pair_x, w_specs, n_x, n_extra, n_out, epilogue):
    n_pairs = len(pair_x)
    xs = refs[:n_x]
    w_refs = refs[n_x:n_x + n_pairs]
    extras = refs[n_x + n_pairs:n_x + n_pairs + n_extra]
    outs = refs[n_x + n_pairs + n_extra:n_x + n_pairs + n_extra + n_out]
    scratch = list(refs[n_x + n_pairs + n_extra + n_out:])
    j, i = pl.program_id(0), pl.program_id(1)
    nj, ni = pl.num_programs(0), pl.num_programs(1)
    slot = j % 2

    streamed = []
    tiles = []
    for wr, spec in zip(w_refs, w_specs):
        if spec[1] is None:
            tiles.append(wr)
        else:
            tile16, stages, sems = scratch[:3]
            del scratch[:3]
            streamed.append((wr, spec, tile16, stages, sems))
            tiles.append(tile16.at[slot])

    def chunk_rows(spec, ci):
        cr = spec[1][2]
        return pl.ds(pl.multiple_of(ci * cr, SUBLANES), cr)

    @pl.when((j == 0) & (i == 0))
    def _():
        depth = FIRST_TILE_DEPTH

        def first(k, ci):
            wr, spec, _, stages, sems = streamed[k]
            b = ci % depth
            return _weight_chunk_copy(wr, stages.at[b], sems.at[b], spec, 0, ci)

        for ci in range(depth - 1):
            @pl.when(ci < ni)
            def _():
                for k in range(len(streamed)):
                    first(k, ci).start()

        def load(ci, carry):
            for k, (_, spec, tile16, stages, _) in enumerate(streamed):
                @pl.when(ci + depth - 1 < ni)
                def _():
                    first(k, ci + depth - 1).start()

                first(k, ci).wait()
                tile16[0, chunk_rows(spec, ci), :] = stages[ci % depth].astype(BF16)
            return carry
        lax.fori_loop(0, ni, load, 0)

        for wr, spec, _, stages, sems in streamed:
            @pl.when(nj > 1)
            def _():
                _weight_chunk_copy(wr, stages.at[0], sems.at[0], spec, 1, 0).start()

    accs = []
    for xi, w, spec in zip(pair_x, tiles, w_specs):
        if spec[0] == "kn":
            accs.append(jnp.dot(xs[xi][...], w[...], preferred_element_type=F32))
        else:
            accs.append(lax.dot_general(xs[xi][...], w[...], (((1,), (1,)), ((), ())),
                                        preferred_element_type=F32))
    res = epilogue(accs, [e[...] for e in extras])
    for o, r in zip(outs, res):
        o[...] = r.astype(o.dtype)

    for wr, spec, tile16, stages, sems in streamed:
        stage, sem = stages.at[0], sems.at[0]

        @pl.when(j + 1 < nj)
        def _():
            _weight_chunk_copy(wr, stage, sem, spec, j + 1, i).wait()
            tile16[1 - slot, chunk_rows(spec, i), :] = stage[...].astype(BF16)

        @pl.when((j + 1 < nj) & (i + 1 < ni))
        def _():
            _weight_chunk_copy(wr, stage, sem, spec, j + 1, i + 1).start()

        @pl.when((i + 1 == ni) & (j + 2 < nj))
        def _():
            _weight_chunk_copy(wr, stage, sem, spec, j + 2, 0).start()


def _matmul(name, xs, pairs, epilogue, outs, *, tm, tn, extras=()):
    m = xs[0].shape[0]
    n = pairs[0][4]
    assert m % tm == 0 and n % tn == 0
    nj, ni = n // tn, m // tm
    grid = (nj, ni)

    in_specs = [pl.BlockSpec((tm, x.shape[1]), lambda j, i: (i, 0)) for x in xs]
    scratch, w_specs = [], []
    for _, w, kind, col0, ncols in pairs:
        assert ncols == n
        if w.dtype == BF16:
            assert col0 % tn == 0
            if kind == "kn":
                in_specs.append(pl.BlockSpec((w.shape[0], tn),
                                             lambda j, i, c0=col0 // tn: (0, c0 + j)))
            else:
                in_specs.append(pl.BlockSpec((tn, w.shape[1]),
                                             lambda j, i, c0=col0 // tn: (c0 + j, 0)))
            w_specs.append((kind, None))
            continue
        rows = w.shape[0] if kind == "kn" else tn
        lanes = tn if kind == "kn" else w.shape[1]
        assert rows % (ni * SUBLANES) == 0 and col0 % SUBLANES == 0
        if kind == "kn":
            assert col0 % LANES == 0 and tn % LANES == 0
        cr = rows // ni
        in_specs.append(pl.BlockSpec(memory_space=pl.ANY))
        scratch += [pltpu.VMEM((2, rows, lanes), BF16),
                    pltpu.VMEM((FIRST_TILE_DEPTH, cr, lanes), F32),
                    pltpu.SemaphoreType.DMA((FIRST_TILE_DEPTH,))]
        w_specs.append((kind, (col0, tn, cr)))
    for _, bs, im in extras:
        in_specs.append(pl.BlockSpec(bs, functools.partial(_swap_ji, im)))
    out_specs = [pl.BlockSpec(bs, functools.partial(_swap_ji, im)) for _, _, bs, im in outs]
    out_shape = [jax.ShapeDtypeStruct(s, d) for s, d, _, _ in outs]

    kern = functools.partial(_mm_kernel, pair_x=tuple(p[0] for p in pairs),
                             w_specs=tuple(w_specs), n_x=len(xs), n_extra=len(extras),
                             n_out=len(outs), epilogue=epilogue)
    return pl.pallas_call(
        kern, grid=grid, in_specs=in_specs, out_specs=out_specs, out_shape=out_shape,
        scratch_shapes=scratch, compiler_params=_cparams(2), name=name,
    )(*xs, *[p[1] for p in pairs], *[e[0] for e in extras])


def _swap_ji(im, j, i):
    return im(i, j)


def _tile_ij(i, j):
    return (i, j)


def _row_i(i, j):
    return (i, 0)


def _col_j(i, j):
    return (0, j)


def _sigmoid(x):
    return 1.0 / (1.0 + jnp.exp(-x))


def _silu(x):
    return x * _sigmoid(x)


def _ln_apply(y, mu, rstd, g, b):
    return (y - mu) * rstd * g + b


def _ln_kernel(y_ref, g_ref, b_ref, o_ref, *stat_refs):
    y = y_ref[...]
    mu = jnp.mean(y, axis=-1, keepdims=True)
    d = y - mu
    rstd = lax.rsqrt(jnp.mean(d * d, axis=-1, keepdims=True) + EPS)
    o_ref[...] = _ln_apply(y, mu, rstd, g_ref[...], b_ref[...]).astype(o_ref.dtype)
    if stat_refs:
        mu_ref, rstd_ref = stat_refs
        mu_ref[...] = jnp.broadcast_to(mu, mu_ref.shape)
        rstd_ref[...] = jnp.broadcast_to(rstd, rstd_ref.shape)


def _layer_norm(name, y, g, b, out_dtype, with_stats, tm=512):
    m, d = y.shape
    out_specs = [pl.BlockSpec((tm, d), lambda i: (i, 0))]
    out_shape = [jax.ShapeDtypeStruct((m, d), out_dtype)]
    if with_stats:
        out_specs += [pl.BlockSpec((tm, LANES), lambda i: (i, 0))] * 2
        out_shape += [jax.ShapeDtypeStruct((m, LANES), F32)] * 2
    return pl.pallas_call(
        _ln_kernel, grid=(m // tm,), name=name,
        in_specs=[pl.BlockSpec((tm, d), lambda i: (i, 0)),
                  pl.BlockSpec((1, d), lambda i: (0, 0)),
                  pl.BlockSpec((1, d), lambda i: (0, 0))],
        out_specs=out_specs, out_shape=out_shape,
        compiler_params=_cparams(1),
    )(y, g.reshape(1, d), b.reshape(1, d))


def _rope_tables(pos, invf):
    ang = pos.astype(F32) * invf
    lane = lax.broadcasted_iota(jnp.int32, ang.shape, 1)
    cos = jnp.cos(ang)
    sin = jnp.sin(ang)
    half = ROPE // 2
    return (cos, jnp.where(lane < half, -sin, 0.0),
            jnp.where((lane >= half) & (lane < ROPE), sin, 0.0))


def _rope_frequencies():
    half = ROPE // 2
    inv_freq = ROPE_THETA ** (-jnp.arange(half, dtype=F32) / half)
    return jnp.concatenate([inv_freq, inv_freq, jnp.zeros((LANES - ROPE,), F32)]).reshape(1, LANES)


def _rope_lanes(x, c, sa, sb):
    half = ROPE // 2
    return (x * c + pltpu.roll(x, LANES - half, axis=1) * sa
            + pltpu.roll(x, half, axis=1) * sb)


def _rms(x, g):
    ms = jnp.mean(x * x, axis=-1, keepdims=True)
    return x * lax.rsqrt(ms + EPS) * g


def _latent_epilogue(accs, extras):
    acc = accs[0]
    gq, gkv, pos, invf = extras
    c, sa, sb = _rope_tables(pos, invf)
    qn = _rms(acc[:, :Q_LORA], gq)
    kvn = _rms(acc[:, Q_LORA:Q_LORA + KV_LORA], gkv)
    kr = _rope_lanes(acc[:, Q_LORA + KV_LORA:], c, sa, sb)
    return [qn, kvn, kr, c, sa, sb]


def _uq_epilogue(accs, extras, *, scale):
    acc = accs[0]
    c, sa, sb = extras
    tn = acc.shape[1]
    cols = []
    for h in range(tn // HEAD_W):
        base = h * HEAD_W
        cols.append(acc[:, base:base + NOPE] * scale)
        cols.append(_rope_lanes(acc[:, base + NOPE:base + HEAD_W], c, sa, sb) * scale)
    return [jnp.concatenate(cols, axis=1)]


def _attn_kernel(q_ref, kv_ref, kr_ref, o_ref, kcat_ref, *, tq, hpb):
    t = q_ref.shape[0]
    for hh in range(hpb):
        kcat_ref[hh, :, :NOPE] = kv_ref[:, hh * KV_W:hh * KV_W + NOPE]
        kcat_ref[hh, :, NOPE:] = kr_ref[...]
    row_chunk = lax.broadcasted_iota(jnp.int32, (tq, tq), 0) // CHUNK
    col_chunk = lax.broadcasted_iota(jnp.int32, (tq, tq), 1) // CHUNK
    visible = col_chunk <= row_chunk
    nt = (((1,), (1,)), ((), ()))
    for jq in range(t // tq):
        s0, s1 = jq * tq, (jq + 1) * tq
        for hh in range(hpb):
            ol = slice(hh * V_DIM, (hh + 1) * V_DIM)
            vl = slice(hh * KV_W + NOPE, (hh + 1) * KV_W)
            q = q_ref[s0:s1, hh * HEAD_W:(hh + 1) * HEAD_W]
            sd = lax.dot_general(q, kcat_ref[hh, s0:s1, :], nt, preferred_element_type=F32)
            sd = jnp.where(visible, sd, -jnp.inf)
            m = jnp.max(sd, axis=-1, keepdims=True)
            if jq > 0:
                sp = lax.dot_general(q, kcat_ref[hh, 0:s0, :], nt, preferred_element_type=F32)
                m = jnp.maximum(m, jnp.max(sp, axis=-1, keepdims=True))
            pd = jnp.exp(sd - m)
            l = jnp.sum(pd, axis=-1, keepdims=True)
            o = jnp.dot(pd.astype(BF16), kv_ref[s0:s1, vl], preferred_element_type=F32)
            if jq > 0:
                pp = jnp.exp(sp - m)
                l = l + jnp.sum(pp, axis=-1, keepdims=True)
                o = o + jnp.dot(pp.astype(BF16), kv_ref[0:s0, vl], preferred_element_type=F32)
            o_ref[s0:s1, ol] = (o / l).astype(o_ref.dtype)


def _attention(qf, kv, kr, batch, seq, heads, hpb=4):
    n = batch * seq
    kern = functools.partial(_attn_kernel, tq=ATT_TQ, hpb=hpb)
    groups = heads // hpb
    return pl.pallas_call(
        kern, grid=(batch, groups), name="mla_attention",
        in_specs=[pl.BlockSpec((seq, hpb * HEAD_W), lambda b, h: (b, h)),
                  pl.BlockSpec((seq, hpb * KV_W), lambda b, h: (b, h)),
                  pl.BlockSpec((seq, LANES), lambda b, h: (b, 0))],
        out_specs=pl.BlockSpec((seq, hpb * V_DIM), lambda b, h: (b, h)),
        out_shape=jax.ShapeDtypeStruct((n, heads * V_DIM), BF16),
        scratch_shapes=[pltpu.VMEM((hpb, seq, HEAD_W), BF16)],
        compiler_params=_cparams(2),
    )(qf, kv, kr)


def _split3(x):
    a = x.astype(BF16)
    r = x - a.astype(F32)
    b = r.astype(BF16)
    c = (r - b.astype(F32)).astype(BF16)
    return a, b, c


def _chunk_rows(n):
    r0 = n * HG_CHUNK
    return pl.ds(r0 if isinstance(n, int) else pl.multiple_of(r0, HG_CHUNK), HG_CHUNK)


def _hgrn_decay_stage(n, hh, refs, masks):
    d = HG_DIM
    lanes = slice(hh * d, (hh + 1) * d)
    f = refs["f"][_chunk_rows(n), lanes]
    g = jnp.log(f) * LOG2_E
    cs = jnp.dot(masks[0], jnp.concatenate(_split3(g), axis=1), preferred_element_type=F32)
    b = cs[:, :d] + cs[:, d:2 * d] + cs[:, 2 * d:]
    refs["b"][n % 2, hh] = b
    refs["c"][n % 2, hh] = b - jnp.log(1.0 - f) * LOG2_E


def _hgrn_score_stage(n, hh, refs, masks):
    c, d, n_sub = HG_CHUNK, HG_DIM, HG_CHUNK // SUB
    m_diag, m_levels = masks[1], masks[2:]
    lanes = slice(hh * d, (hh + 1) * d)
    rows = _chunk_rows(n)
    slot = n % 2
    nt = (((1,), (1,)), ((), ()))

    q = refs["q"][rows, lanes].astype(F32)
    k = 1.0 - refs["f"][rows, lanes]
    b = refs["b"][slot, hh]
    b_last = b[c - 1:c, :]

    q3 = q.reshape(n_sub, SUB, d)
    b3 = b.reshape(n_sub, SUB, d)
    c3 = refs["c"][slot, hh].reshape(n_sub, SUB, d)
    a_ref = refs["a"]
    for s in range(SUB):
        lo = (s // SUBLANES) * SUBLANES
        a = q3[:, lo:, :] * jnp.exp2(jnp.minimum(b3[:, lo:, :] - c3[:, s:s + 1, :], 0.0))
        if lo:
            a = jnp.concatenate([jnp.zeros((n_sub, lo, d), F32), a], axis=1)
        a_ref[hh, :, s * d:(s + 1) * d] = a.reshape(c, d).astype(BF16)
    p_diag = jnp.dot(a_ref[hh], refs["sel"][...], preferred_element_type=F32)

    def level(hs):
        blk = 2 * hs
        qt, kt = [], []
        for i in range(0, c, hs):
            edge = (i // blk) * blk + hs - 1
            e = jnp.exp2(-jnp.abs(b[i:i + hs, :] - b[edge:edge + 1, :]))
            zero = jnp.zeros((hs, d), F32)
            upper = (i % blk) >= hs
            qt.append(q[i:i + hs, :] * e if upper else zero)
            kt.append(zero if upper else k[i:i + hs, :] * e)
        qt = jnp.concatenate(qt, axis=0).astype(BF16)
        kt = jnp.concatenate(kt, axis=0).astype(BF16)
        return lax.dot_general(qt, kt, nt, preferred_element_type=F32)

    p = level(c // 2)
    for idx in reversed(range(len(m_levels))):
        p = jnp.where(m_levels[idx], level(SUB << idx), p)
    p = jnp.where(m_diag, p_diag, p)
    refs["p"][slot, hh] = p.astype(BF16)
    refs["q0"][slot, hh] = (q * jnp.exp2(b)).astype(BF16)
    refs["kl"][slot, hh] = (k * jnp.exp2(b_last - b)).astype(BF16)
    refs["dl"][slot, hh] = jnp.broadcast_to(jnp.exp2(b_last), (SUBLANES, d))


def _hgrn_output_stage(n, hh, refs):
    d = HG_DIM
    lanes = slice(hh * d, (hh + 1) * d)
    rows = _chunk_rows(n)
    slot = n % 2
    nt = (((1,), (1,)), ((), ()))
    v = refs["i"][rows, lanes]
    st = refs["st"][hh]
    o = (jnp.dot(refs["p"][slot, hh], v, preferred_element_type=F32)
         + lax.dot_general(refs["q0"][slot, hh], st.astype(BF16), nt, preferred_element_type=F32))
    vt = v.astype(F32).T.astype(BF16)
    refs["st"][hh] = (refs["dl"][slot, hh][0:1, :] * st
                      + jnp.dot(vt, refs["kl"][slot, hh], preferred_element_type=F32))
    ms = jnp.mean(o * o, axis=-1, keepdims=True)
    ob = o * lax.rsqrt(ms + EPS) * refs["gn"][...] * refs["gs"][rows, lanes].astype(F32)
    refs["o"][rows, lanes] = ob.astype(refs["o"].dtype)


def _hgrn_kernel(q_ref, f_ref, i_ref, gs_ref, gn_ref, sel_ref, o_ref, st_ref, a_ref, b_ref,
                 c_ref, p_ref, q0_ref, kl_ref, dl_ref, *, hpb):
    nc = q_ref.shape[0] // HG_CHUNK
    assert nc >= 3
    c, d = HG_CHUNK, HG_DIM
    row = lax.broadcasted_iota(jnp.int32, (c, d), 0)
    col = lax.broadcasted_iota(jnp.int32, (c, d), 1)
    tri = (col <= row).astype(BF16)
    m_diag = ((row // SUB) == (col // SUB)) & ((col % SUB) <= (row % SUB))
    blks = [2 * SUB << i for i in range(64) if 2 * SUB << i < c]
    masks = (tri, m_diag) + tuple((row // blk) == (col // blk) for blk in blks)
    refs = dict(q=q_ref, f=f_ref, i=i_ref, gs=gs_ref, gn=gn_ref, sel=sel_ref, o=o_ref,
                st=st_ref, a=a_ref, b=b_ref, c=c_ref, p=p_ref, q0=q0_ref, kl=kl_ref, dl=dl_ref)

    st_ref[...] = jnp.zeros_like(st_ref)

    def run(n_out, n_score, n_decay):
        for hh in range(hpb):
            if n_out is not None:
                _hgrn_output_stage(n_out, hh, refs)
            if n_score is not None:
                _hgrn_score_stage(n_score, hh, refs, masks)
            if n_decay is not None:
                _hgrn_decay_stage(n_decay, hh, refs, masks)

    run(None, None, 0)
    run(None, 0, 1)

    def steady(n, carry):
        run(n - 2, n - 1, n)
        return carry

    lax.fori_loop(2, nc, steady, 0)
    run(nc - 2, nc - 1, None)
    run(nc - 1, None, None)


def _hgrn2(qs, f, iv, gs, gnorm, batch, seq, heads, hpb=4):
    n = batch * seq
    d = HG_DIM
    sel = (jnp.arange(SUB * d)[:, None] // d == jnp.arange(d)[None, :] % SUB).astype(BF16)
    spec = pl.BlockSpec((seq, hpb * d), lambda b, h: (b, h))
    return pl.pallas_call(
        functools.partial(_hgrn_kernel, hpb=hpb), grid=(batch, heads // hpb), name="hgrn2",
        in_specs=[spec, spec, spec, spec, pl.BlockSpec((1, d), lambda b, h: (0, 0)),
                  pl.BlockSpec((SUB * d, d), lambda b, h: (0, 0))],
        out_specs=spec,
        out_shape=jax.ShapeDtypeStruct((n, heads * d), BF16),
        scratch_shapes=[pltpu.VMEM((hpb, d, d), F32),
                        pltpu.VMEM((hpb, HG_CHUNK, SUB * d), BF16),
                        pltpu.VMEM((2, hpb, HG_CHUNK, d), F32),
                        pltpu.VMEM((2, hpb, HG_CHUNK, d), F32),
                        pltpu.VMEM((2, hpb, HG_CHUNK, d), BF16),
                        pltpu.VMEM((2, hpb, HG_CHUNK, d), BF16),
                        pltpu.VMEM((2, hpb, HG_CHUNK, d), BF16),
                        pltpu.VMEM((2, hpb, SUBLANES, d), F32)],
        compiler_params=_cparams(2),
    )(qs, f, iv, gs, gnorm.reshape(1, d), sel)


def _ep_silu(accs, extras):
    return [_silu(accs[0])]


def _ep_ident(accs, extras):
    return [accs[0]]


def _ep_forget(accs, extras):
    lbp = extras[0]
    m = jnp.max(lbp, axis=0, keepdims=True)
    e = jnp.exp(lbp - m)
    lb = e[0:1, :] / jnp.sum(e, axis=0, keepdims=True)
    return [lb + (1.0 - lb) * _sigmoid(accs[0])]


def _ep_merge(accs, extras):
    ga, a, gb, b = accs
    return [_sigmoid(ga) * a + _sigmoid(gb) * b]


def _ep_residual(accs, extras):
    y, mu, rstd, g, b = extras
    return [ALPHA * _ln_apply(y, mu[:, :1], rstd[:, :1], g, b) + accs[0]]


def _ep_swiglu(accs, extras):
    return [_silu(accs[0]) * accs[1]]


def _uq_weight_kernel(w_ref, o_ref):
    src = NOPE + ROPE
    pad = jnp.zeros((w_ref.shape[0], HEAD_W - src), o_ref.dtype)
    for h in range(MLA_HEADS):
        o_ref[:, h * HEAD_W:h * HEAD_W + src] = w_ref[:, h * src:(h + 1) * src].astype(o_ref.dtype)
        o_ref[:, h * HEAD_W + src:(h + 1) * HEAD_W] = pad


def _uq_weight(w_uq, tr=256):
    r = w_uq.shape[0]
    return pl.pallas_call(
        _uq_weight_kernel, grid=(r // tr,), name="q_up_weight_layout",
        in_specs=[pl.BlockSpec((tr, w_uq.shape[1]), lambda i: (i, 0))],
        out_specs=pl.BlockSpec((tr, MLA_HEADS * HEAD_W), lambda i: (i, 0)),
        out_shape=jax.ShapeDtypeStruct((r, MLA_HEADS * HEAD_W), BF16),
        compiler_params=_cparams(1),
    )(w_uq)


def kernel(x, positions, ln_in_g, ln_in_b, w_in, q_norm_g, w_uq, kv_norm_g, w_ukv, hg_lb,
           hg_norm_g, w_branch_a, w_branch_b, w_out, ln1_g, ln1_b, w_gate, w_up, w_down,
           ln2_g, ln2_b):
    batch, seq, d = x.shape
    n = batch * seq
    assert w_in.shape[0] == DEPTH == 1
    l = 0
    wt = w_in.reshape(d, w_in.shape[-1]).T

    o_kr = Q_LORA + KV_LORA
    o_hq = o_kr + ROPE
    o_hf, o_hi, o_hg = o_hq + HG_WIDTH, o_hq + 2 * HG_WIDTH, o_hq + 3 * HG_WIDTH
    o_ga = o_hq + 4 * HG_WIDTH
    o_gb = o_ga + D_MODEL

    x2 = x.reshape(n, d)
    h16, h_mu, h_rstd = _layer_norm("ln_in", x2, ln_in_g, ln_in_b, BF16, True)

    def residual_specs(y, mu, rstd, g, b, tm, tn):
        return [(y, (tm, tn), _tile_ij), (mu, (tm, LANES), _row_i), (rstd, (tm, LANES), _row_i),
                (g.reshape(1, d), (1, tn), _col_j), (b.reshape(1, d), (1, tn), _col_j)]

    lat_w = o_kr + LANES
    w_lat = jnp.concatenate([wt[:o_hq], jnp.zeros((LANES - ROPE, d), wt.dtype)],
                            axis=0).astype(BF16)
    tm = 1024
    table = ((n, LANES), F32, (tm, LANES), _row_i)
    qn, kvn, kr, rc, rsa, rsb = _matmul(
        "latent_proj", [h16], [(0, w_lat, "nk", 0, lat_w)], _latent_epilogue,
        [((n, Q_LORA), BF16, (tm, Q_LORA), _row_i),
         ((n, KV_LORA), BF16, (tm, KV_LORA), _row_i),
         ((n, LANES), BF16, (tm, LANES), _row_i), table, table, table],
        tm=tm, tn=lat_w,
        extras=[(q_norm_g[l].reshape(1, -1), (1, Q_LORA), lambda i, j: (0, 0)),
                (kv_norm_g[l].reshape(1, -1), (1, KV_LORA), lambda i, j: (0, 0)),
                (positions.reshape(n, 1), (tm, 1), _row_i),
                (_rope_frequencies(), (1, LANES), lambda i, j: (0, 0))])
    rope_specs = [(rc, (tm, LANES), _row_i), (rsa, (tm, LANES), _row_i), (rsb, (tm, LANES), _row_i)]

    scale = (NOPE + ROPE) ** -0.5
    q_w = MLA_HEADS * HEAD_W
    (qf,) = _matmul("q_up_proj", [qn], [(0, _uq_weight(w_uq[l]), "kn", 0, q_w)],
                    functools.partial(_uq_epilogue, scale=scale),
                    [((n, q_w), BF16, (tm, 2048), _tile_ij)],
                    tm=tm, tn=2048, extras=rope_specs)
    kv_w = MLA_HEADS * KV_W
    (kv,) = _matmul("kv_up_proj", [kvn], [(0, w_ukv[l], "kn", 0, kv_w)], _ep_ident,
                    [((n, kv_w), BF16, (2048, 2048), _tile_ij)], tm=2048, tn=2048)
    o_a = _attention(qf, kv, kr, batch, seq, MLA_HEADS)

    tm, tn = 1024, 1024

    def proj(name, col0, ep, dtype, extras=()):
        (r,) = _matmul(name, [h16], [(0, wt, "nk", col0, HG_WIDTH)], ep,
                       [((n, HG_WIDTH), dtype, (tm, tn), _tile_ij)],
                       tm=tm, tn=tn, extras=list(extras))
        return r

    qs = proj("hg_q_proj", o_hq, _ep_silu, BF16)
    fg = proj("hg_f_proj", o_hf, _ep_forget, F32, [(hg_lb, (hg_lb.shape[0], tn), _col_j)])
    iv = proj("hg_i_proj", o_hi, _ep_ident, BF16)
    gs = proj("hg_g_proj", o_hg, _ep_silu, BF16)
    o_b = _hgrn2(qs, fg, iv, gs, hg_norm_g[l], batch, seq, HG_HEADS)

    tm, tn = 512, 512
    (merged,) = _matmul(
        "gated_merge", [h16, o_a, o_b],
        [(0, wt, "nk", o_ga, d), (1, w_branch_a[l], "kn", 0, d),
         (0, wt, "nk", o_gb, d), (2, w_branch_b[l], "kn", 0, d)],
        _ep_merge, [((n, d), BF16, (tm, tn), _tile_ij)], tm=tm, tn=tn)

    tm, tn = 512, 1024
    (y1,) = _matmul("out_proj", [merged], [(0, w_out[l], "kn", 0, d)], _ep_residual,
                    [((n, d), F32, (tm, tn), _tile_ij)], tm=tm, tn=tn,
                    extras=residual_specs(x2, h_mu, h_rstd, ln_in_g, ln_in_b, tm, tn))
    h1_16, h1_mu, h1_rstd = _layer_norm("ln1", y1, ln1_g[l], ln1_b[l], BF16, True)

    tm, tn = 2048, 256
    (act,) = _matmul("swiglu_up", [h1_16], [(0, w_gate[l], "kn", 0, D_FF), (0, w_up[l], "kn", 0, D_FF)],
                     _ep_swiglu, [((n, D_FF), BF16, (tm, tn), _tile_ij)], tm=tm, tn=tn)
    tm, tn = 512, 512
    (y2,) = _matmul("swiglu_down", [act], [(0, w_down[l], "kn", 0, d)], _ep_residual,
                    [((n, d), F32, (tm, tn), _tile_ij)], tm=tm, tn=tn,
                    extras=residual_specs(y1, h1_mu, h1_rstd, ln1_g[l], ln1_b[l], tm, tn))
    (out,) = _layer_norm("ln2", y2, ln2_g[l], ln2_b[l], F32, False)
    return out.reshape(batch, seq, d)
```

```python
import functools

import jax
import jax.numpy as jnp
from jax import lax
from jax.experimental import pallas as pl
from jax.experimental.pallas import tpu as pltpu

F32 = jnp.float32
BF16 = jnp.bfloat16

D_MODEL = 4096
CHUNK = 64
MLA_HEADS = 16
NOPE = 128
ROPE = 64
V_DIM = 128
Q_LORA = 1024
KV_LORA = 512
ROPE_THETA = 10000.0
HG_HEADS = 16
HG_DIM = 128
HG_WIDTH = HG_HEADS * HG_DIM
D_FF = 11008
DEPTH = 1
ALPHA = (2 * DEPTH) ** 0.25
EPS = 1e-5
LOG2_E = 1.4426950408889634

LANES = 128
SUBLANES = 8
VMEM_LIMIT = 60 * 1024 * 1024

HEAD_W = 2 * LANES
KV_W = NOPE + V_DIM
SUB = 8
HG_CHUNK = 128
ATT_TQ = 512
FIRST_TILE_DEPTH = 4


def _cparams(n_grid, vmem=VMEM_LIMIT):
    return pltpu.CompilerParams(dimension_semantics=("arbitrary",) * n_grid,
                                vmem_limit_bytes=vmem)


def _weight_chunk_copy(w_hbm, stage, sem, spec, jj, ci):
    kind, (col0, tn, cr) = spec
    if kind == "kn":
        src = w_hbm.at[pl.ds(pl.multiple_of(ci * cr, SUBLANES), cr),
                       pl.ds(pl.multiple_of(col0 + jj * tn, LANES), tn)]
    else:
        src = w_hbm.at[pl.ds(pl.multiple_of(col0 + jj * tn + ci * cr, SUBLANES), cr), :]
    return pltpu.make_async_copy(src, stage, sem)


def _mm_kernel(*refs, pair_x, w_specs, n_x, n_extra, n_out, epilogue):
    n_pairs = len(pair_x)
    xs = refs[:n_x]
    w_refs = refs[n_x:n_x + n_pairs]
    extras = refs[n_x + n_pairs:n_x + n_pairs + n_extra]
    outs = refs[n_x + n_pairs + n_extra:n_x + n_pairs + n_extra + n_out]
    scratch = list(refs[n_x + n_pairs + n_extra + n_out:])
    j, i = pl.program_id(0), pl.program_id(1)
    nj, ni = pl.num_programs(0), pl.num_programs(1)
    slot = j % 2

    streamed = []
    tiles = []
    for wr, spec in zip(w_refs, w_specs):
        if spec[1] is None:
            tiles.append(wr)
        else:
            tile16, stages, sems = scratch[:3]
            del scratch[:3]
            streamed.append((wr, spec, tile16, stages, sems))
            tiles.append(tile16.at[slot])

    def chunk_rows(spec, ci):
        cr = spec[1][2]
        return pl.ds(pl.multiple_of(ci * cr, SUBLANES), cr)

    @pl.when((j == 0) & (i == 0))
    def _():
        depth = FIRST_TILE_DEPTH

        def first(k, ci):
            wr, spec, _, stages, sems = streamed[k]
            b = ci % depth
            return _weight_chunk_copy(wr, stages.at[b], sems.at[b], spec, 0, ci)

        for ci in range(depth - 1):
            @pl.when(ci < ni)
            def _():
                for k in range(len(streamed)):
                    first(k, ci).start()

        def load(ci, carry):
            for k, (_, spec, tile16, stages, _) in enumerate(streamed):
                @pl.when(ci + depth - 1 < ni)
                def _():
                    first(k, ci + depth - 1).start()

                first(k, ci).wait()
                tile16[0, chunk_rows(spec, ci), :] = stages[ci % depth].astype(BF16)
            return carry
        lax.fori_loop(0, ni, load, 0)

        for wr, spec, _, stages, sems in streamed:
            @pl.when(nj > 1)
            def _():
                _weight_chunk_copy(wr, stages.at[0], sems.at[0], spec, 1, 0).start()

    accs = []
    for xi, w, spec in zip(pair_x, tiles, w_specs):
        if spec[0] == "kn":
            accs.append(jnp.dot(xs[xi][...], w[...], preferred_element_type=F32))
        else:
            accs.append(lax.dot_general(xs[xi][...], w[...], (((1,), (1,)), ((), ())),
                                        preferred_element_type=F32))
    res = epilogue(accs, [e[...] for e in extras])
    for o, r in zip(outs, res):
        o[...] = r.astype(o.dtype)

    for wr, spec, tile16, stages, sems in streamed:
        stage, sem = stages.at[0], sems.at[0]

        @pl.when(j + 1 < nj)
        def _():
            _weight_chunk_copy(wr, stage, sem, spec, j + 1, i).wait()
            tile16[1 - slot, chunk_rows(spec, i), :] = stage[...].astype(BF16)

        @pl.when((j + 1 < nj) & (i + 1 < ni))
        def _():
            _weight_chunk_copy(wr, stage, sem, spec, j + 1, i + 1).start()

        @pl.when((i + 1 == ni) & (j + 2 < nj))
        def _():
            _weight_chunk_copy(wr, stage, sem, spec, j + 2, 0).start()


def _matmul(name, xs, pairs, epilogue, outs, *, tm, tn, extras=()):
    m = xs[0].shape[0]
    n = pairs[0][4]
    assert m % tm == 0 and n % tn == 0
    nj, ni = n // tn, m // tm
    grid = (nj, ni)

    in_specs = [pl.BlockSpec((tm, x.shape[1]), lambda j, i: (i, 0)) for x in xs]
    scratch, w_specs = [], []
    for _, w, kind, col0, ncols in pairs:
        assert ncols == n
        if w.dtype == BF16:
            assert col0 % tn == 0
            if kind == "kn":
                in_specs.append(pl.BlockSpec((w.shape[0], tn),
                                             lambda j, i, c0=col0 // tn: (0, c0 + j)))
            else:
                in_specs.append(pl.BlockSpec((tn, w.shape[1]),
                                             lambda j, i, c0=col0 // tn: (c0 + j, 0)))
            w_specs.append((kind, None))
            continue
        rows = w.shape[0] if kind == "kn" else tn
        lanes = tn if kind == "kn" else w.shape[1]
        assert rows % (ni * SUBLANES) == 0 and col0 % SUBLANES == 0
        if kind == "kn":
            assert col0 % LANES == 0 and tn % LANES == 0
        cr = rows // ni
        in_specs.append(pl.BlockSpec(memory_space=pl.ANY))
        scratch += [pltpu.VMEM((2, rows, lanes), BF16),
                    pltpu.VMEM((FIRST_TILE_DEPTH, cr, lanes), F32),
                    pltpu.SemaphoreType.DMA((FIRST_TILE_DEPTH,))]
        w_specs.append((kind, (col0, tn, cr)))
    for _, bs, im in extras:
        in_specs.append(pl.BlockSpec(bs, functools.partial(_swap_ji, im)))
    out_specs = [pl.BlockSpec(bs, functools.partial(_swap_ji, im)) for _, _, bs, im in outs]
    out_shape = [jax.ShapeDtypeStruct(s, d) for s, d, _, _ in outs]

    kern = functools.partial(_mm_kernel, pair_x=tuple(p[0] for p in pairs),
                             w_specs=tuple(w_specs), n_x=len(xs), n_extra=len(extras),
                             n_out=len(outs), epilogue=epilogue)
    return pl.pallas_call(
        kern, grid=grid, in_specs=in_specs, out_specs=out_specs, out_shape=out_shape,
        scratch_shapes=scratch, compiler_params=_cparams(2), name=name,
    )(*xs, *[p[1] for p in pairs], *[e[0] for e in extras])


def _swap_ji(im, j, i):
    return im(i, j)


def _tile_ij(i, j):
    return (i, j)


def _row_i(i, j):
    return (i, 0)


def _col_j(i, j):
    return (0, j)


def _sigmoid(x):
    return 0.5 * jnp.tanh(0.5 * x) + 0.5


def _silu(x):
    return x * _sigmoid(x)


def _ln_apply(y, mu, rstd, g, b):
    return (y - mu) * rstd * g + b


def _ln_kernel(y_ref, g_ref, b_ref, o_ref, *stat_refs):
    y = y_ref[...]
    mu = jnp.mean(y, axis=-1, keepdims=True)
    d = y - mu
    rstd = lax.rsqrt(jnp.mean(d * d, axis=-1, keepdims=True) + EPS)
    o_ref[...] = _ln_apply(y, mu, rstd, g_ref[...], b_ref[...]).astype(o_ref.dtype)
    if stat_refs:
        mu_ref, rstd_ref = stat_refs
        mu_ref[...] = jnp.broadcast_to(mu, mu_ref.shape)
        rstd_ref[...] = jnp.broadcast_to(rstd, rstd_ref.shape)


def _layer_norm(name, y, g, b, out_dtype, with_stats, tm=512):
    m, d = y.shape
    out_specs = [pl.BlockSpec((tm, d), lambda i: (i, 0))]
    out_shape = [jax.ShapeDtypeStruct((m, d), out_dtype)]
    if with_stats:
        out_specs += [pl.BlockSpec((tm, LANES), lambda i: (i, 0))] * 2
        out_shape += [jax.ShapeDtypeStruct((m, LANES), F32)] * 2
    return pl.pallas_call(
        _ln_kernel, grid=(m // tm,), name=name,
        in_specs=[pl.BlockSpec((tm, d), lambda i: (i, 0)),
                  pl.BlockSpec((1, d), lambda i: (0, 0)),
                  pl.BlockSpec((1, d), lambda i: (0, 0))],
        out_specs=out_specs, out_shape=out_shape,
        compiler_params=_cparams(1),
    )(y, g.reshape(1, d), b.reshape(1, d))


def _rope_tables(pos, invf):
    ang = pos.astype(F32) * invf
    lane = lax.broadcasted_iota(jnp.int32, ang.shape, 1)
    cos = jnp.cos(ang)
    sin = jnp.sin(ang)
    half = ROPE // 2
    return (cos, jnp.where(lane < half, -sin, 0.0),
            jnp.where((lane >= half) & (lane < ROPE), sin, 0.0))


def _rope_frequencies():
    half = ROPE // 2
    inv_freq = ROPE_THETA ** (-jnp.arange(half, dtype=F32) / half)
    return jnp.concatenate([inv_freq, inv_freq, jnp.zeros((LANES - ROPE,), F32)]).reshape(1, LANES)


def _rope_lanes(x, c, sa, sb):
    half = ROPE // 2
    return (x * c + pltpu.roll(x, LANES - half, axis=1) * sa
            + pltpu.roll(x, half, axis=1) * sb)


def _rms(x, g):
    ms = jnp.mean(x * x, axis=-1, keepdims=True)
    return x * lax.rsqrt(ms + EPS) * g


def _latent_epilogue(accs, extras):
    acc = accs[0]
    gq, gkv, pos, invf = extras
    c, sa, sb = _rope_tables(pos, invf)
    qn = _rms(acc[:, :Q_LORA], gq)
    kvn = _rms(acc[:, Q_LORA:Q_LORA + KV_LORA], gkv)
    kr = _rope_lanes(acc[:, Q_LORA + KV_LORA:], c, sa, sb)
    return [qn, kvn, kr, c, sa, sb]


def _uq_epilogue(accs, extras, *, scale):
    acc = accs[0]
    c, sa, sb = extras
    tn = acc.shape[1]
    cols = []
    for h in range(tn // HEAD_W):
        base = h * HEAD_W
        cols.append(acc[:, base:base + NOPE] * scale)
        cols.append(_rope_lanes(acc[:, base + NOPE:base + HEAD_W], c, sa, sb) * scale)
    return [jnp.concatenate(cols, axis=1)]


def _attn_kernel(q_ref, kv_ref, kr_ref, o_ref, kcat_ref, *, tq, hpb):
    t = q_ref.shape[0]
    for hh in range(hpb):
        kcat_ref[hh, :, :NOPE] = kv_ref[:, hh * KV_W:hh * KV_W + NOPE]
        kcat_ref[hh, :, NOPE:] = kr_ref[...]
    row_chunk = lax.broadcasted_iota(jnp.int32, (tq, tq), 0) // CHUNK
    col_chunk = lax.broadcasted_iota(jnp.int32, (tq, tq), 1) // CHUNK
    visible = col_chunk <= row_chunk
    nt = (((1,), (1,)), ((), ()))
    for jq in range(t // tq):
        s0, s1 = jq * tq, (jq + 1) * tq
        for hh in range(hpb):
            ol = slice(hh * V_DIM, (hh + 1) * V_DIM)
            vl = slice(hh * KV_W + NOPE, (hh + 1) * KV_W)
            q = q_ref[s0:s1, hh * HEAD_W:(hh + 1) * HEAD_W]
            sd = lax.dot_general(q, kcat_ref[hh, s0:s1, :], nt, preferred_element_type=F32)
            sd = jnp.where(visible, sd, -jnp.inf)
            m = jnp.max(sd, axis=-1, keepdims=True)
            if jq > 0:
                sp = lax.dot_general(q, kcat_ref[hh, 0:s0, :], nt, preferred_element_type=F32)
                m = jnp.maximum(m, jnp.max(sp, axis=-1, keepdims=True))
            pd = jnp.exp(sd - m)
            l = jnp.sum(pd, axis=-1, keepdims=True)
            o = jnp.dot(pd.astype(BF16), kv_ref[s0:s1, vl], preferred_element_type=F32)
            if jq > 0:
                pp = jnp.exp(sp - m)
                l = l + jnp.sum(pp, axis=-1, keepdims=True)
                o = o + jnp.dot(pp.astype(BF16), kv_ref[0:s0, vl], preferred_element_type=F32)
            o_ref[s0:s1, ol] = (o / l).astype(o_ref.dtype)


def _attention(qf, kv, kr, batch, seq, heads, hpb=4):
    n = batch * seq
    kern = functools.partial(_attn_kernel, tq=ATT_TQ, hpb=hpb)
    groups = heads // hpb
    return pl.pallas_call(
        kern, grid=(batch, groups), name="mla_attention",
        in_specs=[pl.BlockSpec((seq, hpb * HEAD_W), lambda b, h: (b, h)),
                  pl.BlockSpec((seq, hpb * KV_W), lambda b, h: (b, h)),
                  pl.BlockSpec((seq, LANES), lambda b, h: (b, 0))],
        out_specs=pl.BlockSpec((seq, hpb * V_DIM), lambda b, h: (b, h)),
        out_shape=jax.ShapeDtypeStruct((n, heads * V_DIM), BF16),
        scratch_shapes=[pltpu.VMEM((hpb, seq, HEAD_W), BF16)],
        compiler_params=_cparams(2),
    )(qf, kv, kr)


def _split3(x):
    a = x.astype(BF16)
    r = x - a.astype(F32)
    b = r.astype(BF16)
    c = (r - b.astype(F32)).astype(BF16)
    return a, b, c


def _chunk_rows(n):
    r0 = n * HG_CHUNK
    return pl.ds(r0 if isinstance(n, int) else pl.multiple_of(r0, HG_CHUNK), HG_CHUNK)


def _hgrn_decay_stage(n, hh, refs, masks):
    d = HG_DIM
    lanes = slice(hh * d, (hh + 1) * d)
    f = refs["f"][_chunk_rows(n), lanes]
    g = jnp.log(f) * LOG2_E
    cs = jnp.dot(masks[0], jnp.concatenate(_split3(g), axis=1), preferred_element_type=F32)
    b = cs[:, :d] + cs[:, d:2 * d] + cs[:, 2 * d:]
    refs["b"][n % 2, hh] = b
    refs["c"][n % 2, hh] = b - jnp.log(1.0 - f) * LOG2_E


def _hgrn_score_stage(n, hh, refs, masks):
    c, d, n_sub = HG_CHUNK, HG_DIM, HG_CHUNK // SUB
    m_diag, m_levels = masks[1], masks[2:]
    lanes = slice(hh * d, (hh + 1) * d)
    rows = _chunk_rows(n)
    slot = n % 2
    nt = (((1,), (1,)), ((), ()))

    q = refs["q"][rows, lanes].astype(F32)
    k = 1.0 - refs["f"][rows, lanes]
    b = refs["b"][slot, hh]
    b_last = b[c - 1:c, :]

    q3 = q.reshape(n_sub, SUB, d)
    b3 = b.reshape(n_sub, SUB, d)
    c3 = refs["c"][slot, hh].reshape(n_sub, SUB, d)
    a_ref = refs["a"]
    for s in range(SUB):
        lo = (s // SUBLANES) * SUBLANES
        a = q3[:, lo:, :] * jnp.exp2(jnp.minimum(b3[:, lo:, :] - c3[:, s:s + 1, :], 0.0))
        if lo:
            a = jnp.concatenate([jnp.zeros((n_sub, lo, d), F32), a], axis=1)
        a_ref[hh, :, s * d:(s + 1) * d] = a.reshape(c, d).astype(BF16)
    p_diag = jnp.dot(a_ref[hh], refs["sel"][...], preferred_element_type=F32)

    def level(hs):
        blk = 2 * hs
        qt, kt = [], []
        for i in range(0, c, hs):
            edge = (i // blk) * blk + hs - 1
            e = jnp.exp2(-jnp.abs(b[i:i + hs, :] - b[edge:edge + 1, :]))
            zero = jnp.zeros((hs, d), F32)
            upper = (i % blk) >= hs
            qt.append(q[i:i + hs, :] * e if upper else zero)
            kt.append(zero if upper else k[i:i + hs, :] * e)
        qt = jnp.concatenate(qt, axis=0).astype(BF16)
        kt = jnp.concatenate(kt, axis=0).astype(BF16)
        return lax.dot_general(qt, kt, nt, preferred_element_type=F32)

    p = level(c // 2)
    for idx in reversed(range(len(m_levels))):
        p = jnp.where(m_levels[idx], level(SUB << idx), p)
    p = jnp.where(m_diag, p_diag, p)
    refs["p"][slot, hh] = p.astype(BF16)
    refs["q0"][slot, hh] = (q * jnp.exp2(b)).astype(BF16)
    refs["kl"][slot, hh] = (k * jnp.exp2(b_last - b)).astype(BF16)
    refs["dl"][slot, hh] = jnp.broadcast_to(jnp.exp2(b_last), (SUBLANES, d))


def _hgrn_output_stage(n, hh, refs):
    d = HG_DIM
    lanes = slice(hh * d, (hh + 1) * d)
    rows = _chunk_rows(n)
    slot = n % 2
    nt = (((1,), (1,)), ((), ()))
    v = refs["i"][rows, lanes]
    st = refs["st"][hh]
    o = (jnp.dot(refs["p"][slot, hh], v, preferred_element_type=F32)
         + lax.dot_general(refs["q0"][slot, hh], st.astype(BF16), nt, preferred_element_type=F32))
    vt = v.astype(F32).T.astype(BF16)
    refs["st"][hh] = (refs["dl"][slot, hh][0:1, :] * st
                      + jnp.dot(vt, refs["kl"][slot, hh], preferred_element_type=F32))
    ms = jnp.mean(o * o, axis=-1, keepdims=True)
    ob = o * lax.rsqrt(ms + EPS) * refs["gn"][...] * refs["gs"][rows, lanes].astype(F32)
    refs["o"][rows, lanes] = ob.astype(refs["o"].dtype)


def _hgrn_kernel(q_ref, f_ref, i_ref, gs_ref, gn_ref, sel_ref, o_ref, st_ref, a_ref, b_ref,
                 c_ref, p_ref, q0_ref, kl_ref, dl_ref, *, hpb):
    nc = q_ref.shape[0] // HG_CHUNK
    assert nc >= 3
    c, d = HG_CHUNK, HG_DIM
    row = lax.broadcasted_iota(jnp.int32, (c, d), 0)
    col = lax.broadcasted_iota(jnp.int32, (c, d), 1)
    tri = (col <= row).astype(BF16)
    m_diag = ((row // SUB) == (col // SUB)) & ((col % SUB) <= (row % SUB))
    blks = [2 * SUB << i for i in range(64) if 2 * SUB << i < c]
    masks = (tri, m_diag) + tuple((row // blk) == (col // blk) for blk in blks)
    refs = dict(q=q_ref, f=f_ref, i=i_ref, gs=gs_ref, gn=gn_ref, sel=sel_ref, o=o_ref,
                st=st_ref, a=a_ref, b=b_ref, c=c_ref, p=p_ref, q0=q0_ref, kl=kl_ref, dl=dl_ref)

    st_ref[...] = jnp.zeros_like(st_ref)

    def run(n_out, n_score, n_decay):
        for hh in range(hpb):
            if n_out is not None:
                _hgrn_output_stage(n_out, hh, refs)
            if n_score is not None:
                _hgrn_score_stage(n_score, hh, refs, masks)
            if n_decay is not None:
                _hgrn_decay_stage(n_decay, hh, refs, masks)

    run(None, None, 0)
    run(None, 0, 1)

    def steady(n, carry):
        run(n - 2, n - 1, n)
        return carry

    lax.fori_loop(2, nc, steady, 0)
    run(nc - 2, nc - 1, None)
    run(nc - 1, None, None)


def _hgrn2(qs, f, iv, gs, gnorm, batch, seq, heads, hpb=4):
    n = batch * seq
    d = HG_DIM
    sel = (jnp.arange(SUB * d)[:, None] // d == jnp.arange(d)[None, :] % SUB).astype(BF16)
    spec = pl.BlockSpec((seq, hpb * d), lambda b, h: (b, h))
    return pl.pallas_call(
        functools.partial(_hgrn_kernel, hpb=hpb), grid=(batch, heads // hpb), name="hgrn2",
        in_specs=[spec, spec, spec, spec, pl.BlockSpec((1, d), lambda b, h: (0, 0)),
                  pl.BlockSpec((SUB * d, d), lambda b, h: (0, 0))],
        out_specs=spec,
        out_shape=jax.ShapeDtypeStruct((n, heads * d), BF16),
        scratch_shapes=[pltpu.VMEM((hpb, d, d), F32),
                        pltpu.VMEM((hpb, HG_CHUNK, SUB * d), BF16),
                        pltpu.VMEM((2, hpb, HG_CHUNK, d), F32),
                        pltpu.VMEM((2, hpb, HG_CHUNK, d), F32),
                        pltpu.VMEM((2, hpb, HG_CHUNK, d), BF16),
                        pltpu.VMEM((2, hpb, HG_CHUNK, d), BF16),
                        pltpu.VMEM((2, hpb, HG_CHUNK, d), BF16),
                        pltpu.VMEM((2, hpb, SUBLANES, d), F32)],
        compiler_params=_cparams(2),
    )(qs, f, iv, gs, gnorm.reshape(1, d), sel)


def _ep_silu(accs, extras):
    return [_silu(accs[0])]


def _ep_ident(accs, extras):
    return [accs[0]]


def _ep_forget(accs, extras):
    lbp = extras[0]
    m = jnp.max(lbp, axis=0, keepdims=True)
    e = jnp.exp(lbp - m)
    lb = e[0:1, :] / jnp.sum(e, axis=0, keepdims=True)
    return [lb + (1.0 - lb) * _sigmoid(accs[0])]


def _ep_merge(accs, extras):
    ga, a, gb, b = accs
    return [_sigmoid(ga) * a + _sigmoid(gb) * b]


def _ep_residual(accs, extras):
    y, mu, rstd, g, b = extras
    return [ALPHA * _ln_apply(y, mu[:, :1], rstd[:, :1], g, b) + accs[0]]


def _ep_swiglu(accs, extras):
    return [_silu(accs[0]) * accs[1]]


def _uq_weight_kernel(w_ref, o_ref):
    src = NOPE + ROPE
    pad = jnp.zeros((w_ref.shape[0], HEAD_W - src), o_ref.dtype)
    for h in range(MLA_HEADS):
        o_ref[:, h * HEAD_W:h * HEAD_W + src] = w_ref[:, h * src:(h + 1) * src].astype(o_ref.dtype)
        o_ref[:, h * HEAD_W + src:(h + 1) * HEAD_W] = pad


def _uq_weight(w_uq, tr=256):
    r = w_uq.shape[0]
    return pl.pallas_call(
        _uq_weight_kernel, grid=(r // tr,), name="q_up_weight_layout",
        in_specs=[pl.BlockSpec((tr, w_uq.shape[1]), lambda i: (i, 0))],
        out_specs=pl.BlockSpec((tr, MLA_HEADS * HEAD_W), lambda i: (i, 0)),
        out_shape=jax.ShapeDtypeStruct((r, MLA_HEADS * HEAD_W), BF16),
        compiler_params=_cparams(1),
    )(w_uq)


def kernel(x, positions, ln_in_g, ln_in_b, w_in, q_norm_g, w_uq, kv_norm_g, w_ukv, hg_lb,
           hg_norm_g, w_branch_a, w_branch_b, w_out, ln1_g, ln1_b, w_gate, w_up, w_down,
           ln2_g, ln2_b):
    batch, seq, d = x.shape
    n = batch * seq
    assert w_in.shape[0] == DEPTH == 1
    l = 0
    wt = w_in.reshape(d, w_in.shape[-1]).T

    o_kr = Q_LORA + KV_LORA
    o_hq = o_kr + ROPE
    o_hf, o_hi, o_hg = o_hq + HG_WIDTH, o_hq + 2 * HG_WIDTH, o_hq + 3 * HG_WIDTH
    o_ga = o_hq + 4 * HG_WIDTH
    o_gb = o_ga + D_MODEL

    x2 = x.reshape(n, d)
    h16, h_mu, h_rstd = _layer_norm("ln_in", x2, ln_in_g, ln_in_b, BF16, True)

    def residual_specs(y, mu, rstd, g, b, tm, tn):
        return [(y, (tm, tn), _tile_ij), (mu, (tm, LANES), _row_i), (rstd, (tm, LANES), _row_i),
                (g.reshape(1, d), (1, tn), _col_j), (b.reshape(1, d), (1, tn), _col_j)]

    lat_w = o_kr + LANES
    w_lat = jnp.concatenate([wt[:o_hq], jnp.zeros((LANES - ROPE, d), wt.dtype)],
                            axis=0).astype(BF16)
    tm = 1024
    table = ((n, LANES), F32, (tm, LANES), _row_i)
    qn, kvn, kr, rc, rsa, rsb = _matmul(
        "latent_proj", [h16], [(0, w_lat, "nk", 0, lat_w)], _latent_epilogue,
        [((n, Q_LORA), BF16, (tm, Q_LORA), _row_i),
         ((n, KV_LORA), BF16, (tm, KV_LORA), _row_i),
         ((n, LANES), BF16, (tm, LANES), _row_i), table, table, table],
        tm=tm, tn=lat_w,
        extras=[(q_norm_g[l].reshape(1, -1), (1, Q_LORA), lambda i, j: (0, 0)),
                (kv_norm_g[l].reshape(1, -1), (1, KV_LORA), lambda i, j: (0, 0)),
                (positions.reshape(n, 1), (tm, 1), _row_i),
                (_rope_frequencies(), (1, LANES), lambda i, j: (0, 0))])
    rope_specs = [(rc, (tm, LANES), _row_i), (rsa, (tm, LANES), _row_i), (rsb, (tm, LANES), _row_i)]

    scale = (NOPE + ROPE) ** -0.5
    q_w = MLA_HEADS * HEAD_W
    (qf,) = _matmul("q_up_proj", [qn], [(0, _uq_weight(w_uq[l]), "kn", 0, q_w)],
                    functools.partial(_uq_epilogue, scale=scale),
                    [((n, q_w), BF16, (tm, 2048), _tile_ij)],
                    tm=tm, tn=2048, extras=rope_specs)
    kv_w = MLA_HEADS * KV_W
    (kv,) = _matmul("kv_up_proj", [kvn], [(0, w_ukv[l], "kn", 0, kv_w)], _ep_ident,
                    [((n, kv_w), BF16, (2048, 2048), _tile_ij)], tm=2048, tn=2048)
    o_a = _attention(qf, kv, kr, batch, seq, MLA_HEADS)

    tm, tn = 1024, 1024

    def proj(name, col0, ep, dtype, extras=()):
        (r,) = _matmul(name, [h16], [(0, wt, "nk", col0, HG_WIDTH)], ep,
                       [((n, HG_WIDTH), dtype, (tm, tn), _tile_ij)],
                       tm=tm, tn=tn, extras=list(extras))
        return r

    qs = proj("hg_q_proj", o_hq, _ep_silu, BF16)
    fg = proj("hg_f_proj", o_hf, _ep_forget, F32, [(hg_lb, (hg_lb.shape[0], tn), _col_j)])
    iv = proj("hg_i_proj", o_hi, _ep_ident, BF16)
    gs = proj("hg_g_proj", o_hg, _ep_silu, BF16)
    o_b = _hgrn2(qs, fg, iv, gs, hg_norm_g[l], batch, seq, HG_HEADS)

    tm, tn = 512, 512
    (merged,) = _matmul(
        "gated_merge", [h16, o_a, o_b],
        [(0, wt, "nk", o_ga, d), (1, w_branch_a[l], "kn", 0, d),
         (0, wt, "nk", o_gb, d), (2, w_branch_b[l], "kn", 0, d)],
        _ep_merge, [((n, d), BF16, (tm, tn), _tile_ij)], tm=tm, tn=tn)

    tm, tn = 512, 1024
    (y1,) = _matmul("out_proj", [merged], [(0, w_out[l], "kn", 0, d)], _ep_residual,
                    [((n, d), F32, (tm, tn), _tile_ij)], tm=tm, tn=tn,
                    extras=residual_specs(x2, h_mu, h_rstd, ln_in_g, ln_in_b, tm, tn))
    h1_16, h1_mu, h1_rstd = _layer_norm("ln1", y1, ln1_g[l], ln1_b[l], BF16, True)

    tm, tn = 2048, 256
    (act,) = _matmul("swiglu_up", [h1_16], [(0, w_gate[l], "kn", 0, D_FF), (0, w_up[l], "kn", 0, D_FF)],
                     _ep_swiglu, [((n, D_FF), BF16, (tm, tn), _tile_ij)], tm=tm, tn=tn)
    tm, tn = 512, 512
    (y2,) = _matmul("swiglu_down", [act], [(0, w_down[l], "kn", 0, d)], _ep_residual,
                    [((n, d), F32, (tm, tn), _tile_ij)], tm=tm, tn=tn,
                    extras=residual_specs(y1, h1_mu, h1_rstd, ln1_g[l], ln1_b[l], tm, tn))
    (out,) = _layer_norm("ln2", y2, ln2_g[l], ln2_b[l], F32, False)
    return out.reshape(batch, seq, d)
```

```python
import functools

import jax
import jax.numpy as jnp
from jax import lax
from jax.experimental import pallas as pl
from jax.experimental.pallas import tpu as pltpu

F32 = jnp.float32
BF16 = jnp.bfloat16

D_MODEL = 4096
CHUNK = 64
MLA_HEADS = 16
NOPE = 128
ROPE = 64
V_DIM = 128
Q_LORA = 1024
KV_LORA = 512
ROPE_THETA = 10000.0
HG_HEADS = 16
HG_DIM = 128
HG_WIDTH = HG_HEADS * HG_DIM
D_FF = 11008
DEPTH = 1
ALPHA = (2 * DEPTH) ** 0.25
EPS = 1e-5
LOG2_E = 1.4426950408889634

LANES = 128
SUBLANES = 8
VMEM_LIMIT = 60 * 1024 * 1024

HEAD_W = 2 * LANES
KV_W = NOPE + V_DIM
SUB = 8
HG_CHUNK = 128
ATT_TQ = 512
FIRST_TILE_DEPTH = 4


def _cparams(n_grid, vmem=VMEM_LIMIT):
    return pltpu.CompilerParams(dimension_semantics=("arbitrary",) * n_grid,
                                vmem_limit_bytes=vmem)


def _weight_chunk_copy(w_hbm, stage, sem, spec, jj, ci):
    kind, (col0, tn, cr) = spec
    if kind == "kn":
        src = w_hbm.at[pl.ds(pl.multiple_of(ci * cr, SUBLANES), cr),
                       pl.ds(pl.multiple_of(col0 + jj * tn, LANES), tn)]
    else:
        src = w_hbm.at[pl.ds(pl.multiple_of(col0 + jj * tn + ci * cr, SUBLANES), cr), :]
    return pltpu.make_async_copy(src, stage, sem)


def _mm_kernel(*refs, pair_x, w_specs, n_x, n_extra, n_out, epilogue):
    n_pairs = len(pair_x)
    xs = refs[:n_x]
    w_refs = refs[n_x:n_x + n_pairs]
    extras = refs[n_x + n_pairs:n_x + n_pairs + n_extra]
    outs = refs[n_x + n_pairs + n_extra:n_x + n_pairs + n_extra + n_out]
    scratch = list(refs[n_x + n_pairs + n_extra + n_out:])
    j, i = pl.program_id(0), pl.program_id(1)
    nj, ni = pl.num_programs(0), pl.num_programs(1)
    slot = j % 2

    streamed = []
    tiles = []
    for wr, spec in zip(w_refs, w_specs):
        if spec[1] is None:
            tiles.append(wr)
        else:
            tile16, stages, sems = scratch[:3]
            del scratch[:3]
            streamed.append((wr, spec, tile16, stages, sems))
            tiles.append(tile16.at[slot])

    def chunk_rows(spec, ci):
        cr = spec[1][2]
        return pl.ds(pl.multiple_of(ci * cr, SUBLANES), cr)

    @pl.when((j == 0) & (i == 0))
    def _():
        depth = FIRST_TILE_DEPTH

        def first(k, ci):
            wr, spec, _, stages, sems = streamed[k]
            b = ci % depth
            return _weight_chunk_copy(wr, stages.at[b], sems.at[b], spec, 0, ci)

        for ci in range(depth - 1):
            @pl.when(ci < ni)
            def _():
                for k in range(len(streamed)):
                    first(k, ci).start()

        def load(ci, carry):
            for k, (_, spec, tile16, stages, _) in enumerate(streamed):
                @pl.when(ci + depth - 1 < ni)
                def _():
                    first(k, ci + depth - 1).start()

                first(k, ci).wait()
                tile16[0, chunk_rows(spec, ci), :] = stages[ci % depth].astype(BF16)
            return carry
        lax.fori_loop(0, ni, load, 0)

        for wr, spec, _, stages, sems in streamed:
            @pl.when(nj > 1)
            def _():
                _weight_chunk_copy(wr, stages.at[0], sems.at[0], spec, 1, 0).start()

    accs = []
    for xi, w, spec in zip(pair_x, tiles, w_specs):
        if spec[0] == "kn":
            accs.append(jnp.dot(xs[xi][...], w[...], preferred_element_type=F32))
        else:
            accs.append(lax.dot_general(xs[xi][...], w[...], (((1,), (1,)), ((), ())),
                                        preferred_element_type=F32))
    res = epilogue(accs, [e[...] for e in extras])
    for o, r in zip(outs, res):
        o[...] = r.astype(o.dtype)

    for wr, spec, tile16, stages, sems in streamed:
        stage, sem = stages.at[0], sems.at[0]

        @pl.when(j + 1 < nj)
        def _():
            _weight_chunk_copy(wr, stage, sem, spec, j + 1, i).wait()
            tile16[1 - slot, chunk_rows(spec, i), :] = stage[...].astype(BF16)

        @pl.when((j + 1 < nj) & (i + 1 < ni))
        def _():
            _weight_chunk_copy(wr, stage, sem, spec, j + 1, i + 1).start()

        @pl.when((i + 1 == ni) & (j + 2 < nj))
        def _():
            _weight_chunk_copy(wr, stage, sem, spec, j + 2, 0).start()


def _matmul(name, xs, pairs, epilogue, outs, *, tm, tn, extras=()):
    m = xs[0].shape[0]
    n = pairs[0][4]
    assert m % tm == 0 and n % tn == 0
    nj, ni = n // tn, m // tm
    grid = (nj, ni)

    in_specs = [pl.BlockSpec((tm, x.shape[1]), lambda j, i: (i, 0)) for x in xs]
    scratch, w_specs = [], []
    for _, w, kind, col0, ncols in pairs:
        assert ncols == n
        if w.dtype == BF16:
            assert col0 % tn == 0
            if kind == "kn":
                in_specs.append(pl.BlockSpec((w.shape[0], tn),
                                             lambda j, i, c0=col0 // tn: (0, c0 + j)))
            else:
                in_specs.append(pl.BlockSpec((tn, w.shape[1]),
                                             lambda j, i, c0=col0 // tn: (c0 + j, 0)))
            w_specs.append((kind, None))
            continue
        rows = w.shape[0] if kind == "kn" else tn
        lanes = tn if kind == "kn" else w.shape[1]
        assert rows % (ni * SUBLANES) == 0 and col0 % SUBLANES == 0
        if kind == "kn":
            assert col0 % LANES == 0 and tn % LANES == 0
        cr = rows // ni
        in_specs.append(pl.BlockSpec(memory_space=pl.ANY))
        scratch += [pltpu.VMEM((2, rows, lanes), BF16),
                    pltpu.VMEM((FIRST_TILE_DEPTH, cr, lanes), F32),
                    pltpu.SemaphoreType.DMA((FIRST_TILE_DEPTH,))]
        w_specs.append((kind, (col0, tn, cr)))
    for _, bs, im in extras:
        in_specs.append(pl.BlockSpec(bs, functools.partial(_swap_ji, im)))
    out_specs = [pl.BlockSpec(bs, functools.partial(_swap_ji, im)) for _, _, bs, im in outs]
    out_shape = [jax.ShapeDtypeStruct(s, d) for s, d, _, _ in outs]

    kern = functools.partial(_mm_kernel, pair_x=tuple(p[0] for p in pairs),
                             w_specs=tuple(w_specs), n_x=len(xs), n_extra=len(extras),
                             n_out=len(outs), epilogue=epilogue)
    return pl.pallas_call(
        kern, grid=grid, in_specs=in_specs, out_specs=out_specs, out_shape=out_shape,
        scratch_shapes=scratch, compiler_params=_cparams(2), name=name,
    )(*xs, *[p[1] for p in pairs], *[e[0] for e in extras])


def _swap_ji(im, j, i):
    return im(i, j)


def _tile_ij(i, j):
    return (i, j)


def _row_i(i, j):
    return (i, 0)


def _col_j(i, j):
    return (0, j)


def _sigmoid(x):
    return 0.5 * jnp.tanh(0.5 * x) + 0.5


def _silu(x):
    h = 0.5 * x
    return h + h * jnp.tanh(h)


def _ln_apply(y, mu, rstd, g, b):
    return (y - mu) * rstd * g + b


def _ln_kernel(y_ref, g_ref, b_ref, o_ref, *stat_refs):
    y = y_ref[...]
    mu = jnp.mean(y, axis=-1, keepdims=True)
    d = y - mu
    rstd = lax.rsqrt(jnp.mean(d * d, axis=-1, keepdims=True) + EPS)
    o_ref[...] = _ln_apply(y, mu, rstd, g_ref[...], b_ref[...]).astype(o_ref.dtype)
    if stat_refs:
        mu_ref, rstd_ref = stat_refs
        mu_ref[...] = jnp.broadcast_to(mu, mu_ref.shape)
        rstd_ref[...] = jnp.broadcast_to(rstd, rstd_ref.shape)


def _layer_norm(name, y, g, b, out_dtype, with_stats, tm=512):
    m, d = y.shape
    out_specs = [pl.BlockSpec((tm, d), lambda i: (i, 0))]
    out_shape = [jax.ShapeDtypeStruct((m, d), out_dtype)]
    if with_stats:
        out_specs += [pl.BlockSpec((tm, LANES), lambda i: (i, 0))] * 2
        out_shape += [jax.ShapeDtypeStruct((m, LANES), F32)] * 2
    return pl.pallas_call(
        _ln_kernel, grid=(m // tm,), name=name,
        in_specs=[pl.BlockSpec((tm, d), lambda i: (i, 0)),
                  pl.BlockSpec((1, d), lambda i: (0, 0)),
                  pl.BlockSpec((1, d), lambda i: (0, 0))],
        out_specs=out_specs, out_shape=out_shape,
        compiler_params=_cparams(1),
    )(y, g.reshape(1, d), b.reshape(1, d))


def _rope_tables(pos, invf):
    ang = pos.astype(F32) * invf
    lane = lax.broadcasted_iota(jnp.int32, ang.shape, 1)
    cos = jnp.cos(ang)
    sin = jnp.sin(ang)
    half = ROPE // 2
    return (cos, jnp.where(lane < half, -sin, 0.0),
            jnp.where((lane >= half) & (lane < ROPE), sin, 0.0))


def _rope_frequencies():
    half = ROPE // 2
    inv_freq = ROPE_THETA ** (-jnp.arange(half, dtype=F32) / half)
    return jnp.concatenate([inv_freq, inv_freq, jnp.zeros((LANES - ROPE,), F32)]).reshape(1, LANES)


def _rope_lanes(x, c, sa, sb):
    half = ROPE // 2
    return (x * c + pltpu.roll(x, LANES - half, axis=1) * sa
            + pltpu.roll(x, half, axis=1) * sb)


def _rms(x, g):
    ms = jnp.mean(x * x, axis=-1, keepdims=True)
    return x * lax.rsqrt(ms + EPS) * g


def _latent_epilogue(accs, extras):
    acc = accs[0]
    gq, gkv, pos, invf = extras
    c, sa, sb = _rope_tables(pos, invf)
    qn = _rms(acc[:, :Q_LORA], gq)
    kvn = _rms(acc[:, Q_LORA:Q_LORA + KV_LORA], gkv)
    kr = _rope_lanes(acc[:, Q_LORA + KV_LORA:], c, sa, sb)
    return [qn, kvn, kr, c, sa, sb]


def _uq_epilogue(accs, extras, *, scale):
    acc = accs[0]
    c, sa, sb = extras
    tn = acc.shape[1]
    cols = []
    for h in range(tn // HEAD_W):
        base = h * HEAD_W
        cols.append(acc[:, base:base + NOPE] * scale)
        cols.append(_rope_lanes(acc[:, base + NOPE:base + HEAD_W], c, sa, sb) * scale)
    return [jnp.concatenate(cols, axis=1)]


def _attn_kernel(q_ref, kv_ref, kr_ref, o_ref, kcat_ref, *, tq, hpb):
    t = q_ref.shape[0]
    for hh in range(hpb):
        kcat_ref[hh, :, :NOPE] = kv_ref[:, hh * KV_W:hh * KV_W + NOPE]
        kcat_ref[hh, :, NOPE:] = kr_ref[...]
    row_chunk = lax.broadcasted_iota(jnp.int32, (tq, tq), 0) // CHUNK
    col_chunk = lax.broadcasted_iota(jnp.int32, (tq, tq), 1) // CHUNK
    visible = col_chunk <= row_chunk
    nt = (((1,), (1,)), ((), ()))
    for jq in range(t // tq):
        s0, s1 = jq * tq, (jq + 1) * tq
        for hh in range(hpb):
            ol = slice(hh * V_DIM, (hh + 1) * V_DIM)
            vl = slice(hh * KV_W + NOPE, (hh + 1) * KV_W)
            q = q_ref[s0:s1, hh * HEAD_W:(hh + 1) * HEAD_W]
            sd = lax.dot_general(q, kcat_ref[hh, s0:s1, :], nt, preferred_element_type=F32)
            sd = jnp.where(visible, sd, -jnp.inf)
            m = jnp.max(sd, axis=-1, keepdims=True)
            if jq > 0:
                sp = lax.dot_general(q, kcat_ref[hh, 0:s0, :], nt, preferred_element_type=F32)
                m = jnp.maximum(m, jnp.max(sp, axis=-1, keepdims=True))
            pd = jnp.exp(sd - m)
            l = jnp.sum(pd, axis=-1, keepdims=True)
            o = jnp.dot(pd.astype(BF16), kv_ref[s0:s1, vl], preferred_element_type=F32)
            if jq > 0:
                pp = jnp.exp(sp - m)
                l = l + jnp.sum(pp, axis=-1, keepdims=True)
                o = o + jnp.dot(pp.astype(BF16), kv_ref[0:s0, vl], preferred_element_type=F32)
            o_ref[s0:s1, ol] = (o / l).astype(o_ref.dtype)


def _attention(qf, kv, kr, batch, seq, heads, hpb=4):
    n = batch * seq
    kern = functools.partial(_attn_kernel, tq=ATT_TQ, hpb=hpb)
    groups = heads // hpb
    return pl.pallas_call(
        kern, grid=(batch, groups), name="mla_attention",
        in_specs=[pl.BlockSpec((seq, hpb * HEAD_W), lambda b, h: (b, h)),
                  pl.BlockSpec((seq, hpb * KV_W), lambda b, h: (b, h)),
                  pl.BlockSpec((seq, LANES), lambda b, h: (b, 0))],
        out_specs=pl.BlockSpec((seq, hpb * V_DIM), lambda b, h: (b, h)),
        out_shape=jax.ShapeDtypeStruct((n, heads * V_DIM), BF16),
        scratch_shapes=[pltpu.VMEM((hpb, seq, HEAD_W), BF16)],
        compiler_params=_cparams(2),
    )(qf, kv, kr)


def _split3(x):
    a = x.astype(BF16)
    r = x - a.astype(F32)
    b = r.astype(BF16)
    c = (r - b.astype(F32)).astype(BF16)
    return a, b, c


def _chunk_rows(n):
    r0 = n * HG_CHUNK
    return pl.ds(r0 if isinstance(n, int) else pl.multiple_of(r0, HG_CHUNK), HG_CHUNK)


def _hgrn_decay_stage(n, hh, refs, masks):
    d = HG_DIM
    lanes = slice(hh * d, (hh + 1) * d)
    f = refs["f"][_chunk_rows(n), lanes]
    g = jnp.log(f) * LOG2_E
    cs = jnp.dot(masks[0], jnp.concatenate(_split3(g), axis=1), preferred_element_type=F32)
    b = cs[:, :d] + cs[:, d:2 * d] + cs[:, 2 * d:]
    refs["b"][n % 2, hh] = b
    refs["c"][n % 2, hh] = b - jnp.log(1.0 - f) * LOG2_E


def _hgrn_score_stage(n, hh, refs, masks):
    c, d, n_sub = HG_CHUNK, HG_DIM, HG_CHUNK // SUB
    m_diag, m_levels = masks[1], masks[2:]
    lanes = slice(hh * d, (hh + 1) * d)
    rows = _chunk_rows(n)
    slot = n % 2
    nt = (((1,), (1,)), ((), ()))

    q = refs["q"][rows, lanes].astype(F32)
    k = 1.0 - refs["f"][rows, lanes]
    b = refs["b"][slot, hh]
    b_last = b[c - 1:c, :]

    q3 = q.reshape(n_sub, SUB, d)
    b3 = b.reshape(n_sub, SUB, d)
    c3 = refs["c"][slot, hh].reshape(n_sub, SUB, d)
    a_ref = refs["a"]
    for s in range(SUB):
        lo = (s // SUBLANES) * SUBLANES
        a = q3[:, lo:, :] * jnp.exp2(jnp.minimum(b3[:, lo:, :] - c3[:, s:s + 1, :], 0.0))
        if lo:
            a = jnp.concatenate([jnp.zeros((n_sub, lo, d), F32), a], axis=1)
        a_ref[hh, :, s * d:(s + 1) * d] = a.reshape(c, d).astype(BF16)
    p_diag = jnp.dot(a_ref[hh], refs["sel"][...], preferred_element_type=F32)

    def level(hs):
        blk = 2 * hs
        qt, kt = [], []
        for i in range(0, c, hs):
            edge = (i // blk) * blk + hs - 1
            e = jnp.exp2(-jnp.abs(b[i:i + hs, :] - b[edge:edge + 1, :]))
            zero = jnp.zeros((hs, d), F32)
            upper = (i % blk) >= hs
            qt.append(q[i:i + hs, :] * e if upper else zero)
            kt.append(zero if upper else k[i:i + hs, :] * e)
        qt = jnp.concatenate(qt, axis=0).astype(BF16)
        kt = jnp.concatenate(kt, axis=0).astype(BF16)
        return lax.dot_general(qt, kt, nt, preferred_element_type=F32)

    p = level(c // 2)
    for idx in reversed(range(len(m_levels))):
        p = jnp.where(m_levels[idx], level(SUB << idx), p)
    p = jnp.where(m_diag, p_diag, p)
    refs["p"][slot, hh] = p.astype(BF16)
    refs["q0"][slot, hh] = (q * jnp.exp2(b)).astype(BF16)
    refs["kl"][slot, hh] = (k * jnp.exp2(b_last - b)).astype(BF16)
    refs["dl"][slot, hh] = jnp.broadcast_to(jnp.exp2(b_last), (SUBLANES, d))


def _hgrn_output_stage(n, hh, refs):
    d = HG_DIM
    lanes = slice(hh * d, (hh + 1) * d)
    rows = _chunk_rows(n)
    slot = n % 2
    nt = (((1,), (1,)), ((), ()))
    v = refs["i"][rows, lanes]
    st = refs["st"][hh]
    o = (jnp.dot(refs["p"][slot, hh], v, preferred_element_type=F32)
         + lax.dot_general(refs["q0"][slot, hh], st.astype(BF16), nt, preferred_element_type=F32))
    vt = v.astype(F32).T.astype(BF16)
    refs["st"][hh] = (refs["dl"][slot, hh][0:1, :] * st
                      + jnp.dot(vt, refs["kl"][slot, hh], preferred_element_type=F32))
    ms = jnp.mean(o * o, axis=-1, keepdims=True)
    ob = o * lax.rsqrt(ms + EPS) * refs["gn"][...] * refs["gs"][rows, lanes].astype(F32)
    refs["o"][rows, lanes] = ob.astype(refs["o"].dtype)


def _hgrn_kernel(q_ref, f_ref, i_ref, gs_ref, gn_ref, sel_ref, o_ref, st_ref, a_ref, b_ref,
                 c_ref, p_ref, q0_ref, kl_ref, dl_ref, *, hpb):
    nc = q_ref.shape[0] // HG_CHUNK
    assert nc >= 3
    c, d = HG_CHUNK, HG_DIM
    row = lax.broadcasted_iota(jnp.int32, (c, d), 0)
    col = lax.broadcasted_iota(jnp.int32, (c, d), 1)
    tri = (col <= row).astype(BF16)
    m_diag = ((row // SUB) == (col // SUB)) & ((col % SUB) <= (row % SUB))
    blks = [2 * SUB << i for i in range(64) if 2 * SUB << i < c]
    masks = (tri, m_diag) + tuple((row // blk) == (col // blk) for blk in blks)
    refs = dict(q=q_ref, f=f_ref, i=i_ref, gs=gs_ref, gn=gn_ref, sel=sel_ref, o=o_ref,
                st=st_ref, a=a_ref, b=b_ref, c=c_ref, p=p_ref, q0=q0_ref, kl=kl_ref, dl=dl_ref)

    st_ref[...] = jnp.zeros_like(st_ref)

    def run(n_out, n_score, n_decay):
        for hh in range(hpb):
            if n_out is not None:
                _hgrn_output_stage(n_out, hh, refs)
            if n_score is not None:
                _hgrn_score_stage(n_score, hh, refs, masks)
            if n_decay is not None:
                _hgrn_decay_stage(n_decay, hh, refs, masks)

    run(None, None, 0)
    run(None, 0, 1)

    def steady(n, carry):
        run(n - 2, n - 1, n)
        return carry

    lax.fori_loop(2, nc, steady, 0)
    run(nc - 2, nc - 1, None)
    run(nc - 1, None, None)


def _hgrn2(qs, f, iv, gs, gnorm, batch, seq, heads, hpb=4):
    n = batch * seq
    d = HG_DIM
    sel = (jnp.arange(SUB * d)[:, None] // d == jnp.arange(d)[None, :] % SUB).astype(BF16)
    spec = pl.BlockSpec((seq, hpb * d), lambda b, h: (b, h))
    return pl.pallas_call(
        functools.partial(_hgrn_kernel, hpb=hpb), grid=(batch, heads // hpb), name="hgrn2",
        in_specs=[spec, spec, spec, spec, pl.BlockSpec((1, d), lambda b, h: (0, 0)),
                  pl.BlockSpec((SUB * d, d), lambda b, h: (0, 0))],
        out_specs=spec,
        out_shape=jax.ShapeDtypeStruct((n, heads * d), BF16),
        scratch_shapes=[pltpu.VMEM((hpb, d, d), F32),
                        pltpu.VMEM((hpb, HG_CHUNK, SUB * d), BF16),
                        pltpu.VMEM((2, hpb, HG_CHUNK, d), F32),
                        pltpu.VMEM((2, hpb, HG_CHUNK, d), F32),
                        pltpu.VMEM((2, hpb, HG_CHUNK, d), BF16),
                        pltpu.VMEM((2, hpb, HG_CHUNK, d), BF16),
                        pltpu.VMEM((2, hpb, HG_CHUNK, d), BF16),
                        pltpu.VMEM((2, hpb, SUBLANES, d), F32)],
        compiler_params=_cparams(2),
    )(qs, f, iv, gs, gnorm.reshape(1, d), sel)


def _ep_silu(accs, extras):
    return [_silu(accs[0])]


def _ep_ident(accs, extras):
    return [accs[0]]


def _ep_forget(accs, extras):
    lbp = extras[0]
    m = jnp.max(lbp, axis=0, keepdims=True)
    e = jnp.exp(lbp - m)
    lb = e[0:1, :] / jnp.sum(e, axis=0, keepdims=True)
    return [lb + (1.0 - lb) * _sigmoid(accs[0])]


def _ep_merge(accs, extras):
    ga, a, gb, b = accs
    return [_sigmoid(ga) * a + _sigmoid(gb) * b]


def _ep_residual(accs, extras):
    y, mu, rstd, g, b = extras
    return [ALPHA * _ln_apply(y, mu[:, :1], rstd[:, :1], g, b) + accs[0]]


def _ep_swiglu(accs, extras):
    return [_silu(accs[0]) * accs[1]]


def _uq_weight_kernel(w_ref, o_ref):
    src = NOPE + ROPE
    pad = jnp.zeros((w_ref.shape[0], HEAD_W - src), o_ref.dtype)
    for h in range(MLA_HEADS):
        o_ref[:, h * HEAD_W:h * HEAD_W + src] = w_ref[:, h * src:(h + 1) * src].astype(o_ref.dtype)
        o_ref[:, h * HEAD_W + src:(h + 1) * HEAD_W] = pad


def _uq_weight(w_uq, tr=256):
    r = w_uq.shape[0]
    return pl.pallas_call(
        _uq_weight_kernel, grid=(r // tr,), name="q_up_weight_layout",
        in_specs=[pl.BlockSpec((tr, w_uq.shape[1]), lambda i: (i, 0))],
        out_specs=pl.BlockSpec((tr, MLA_HEADS * HEAD_W), lambda i: (i, 0)),
        out_shape=jax.ShapeDtypeStruct((r, MLA_HEADS * HEAD_W), BF16),
        compiler_params=_cparams(1),
    )(w_uq)


def kernel(x, positions, ln_in_g, ln_in_b, w_in, q_norm_g, w_uq, kv_norm_g, w_ukv, hg_lb,
           hg_norm_g, w_branch_a, w_branch_b, w_out, ln1_g, ln1_b, w_gate, w_up, w_down,
           ln2_g, ln2_b):
    batch, seq, d = x.shape
    n = batch * seq
    assert w_in.shape[0] == DEPTH == 1
    l = 0
    wt = w_in.reshape(d, w_in.shape[-1]).T

    o_kr = Q_LORA + KV_LORA
    o_hq = o_kr + ROPE
    o_hf, o_hi, o_hg = o_hq + HG_WIDTH, o_hq + 2 * HG_WIDTH, o_hq + 3 * HG_WIDTH
    o_ga = o_hq + 4 * HG_WIDTH
    o_gb = o_ga + D_MODEL

    x2 = x.reshape(n, d)
    h16, h_mu, h_rstd = _layer_norm("ln_in", x2, ln_in_g, ln_in_b, BF16, True)

    def residual_specs(y, mu, rstd, g, b, tm, tn):
        return [(y, (tm, tn), _tile_ij), (mu, (tm, LANES), _row_i), (rstd, (tm, LANES), _row_i),
                (g.reshape(1, d), (1, tn), _col_j), (b.reshape(1, d), (1, tn), _col_j)]

    lat_w = o_kr + LANES
    w_lat = jnp.concatenate([wt[:o_hq], jnp.zeros((LANES - ROPE, d), wt.dtype)],
                            axis=0).astype(BF16)
    tm = 1024
    table = ((n, LANES), F32, (tm, LANES), _row_i)
    qn, kvn, kr, rc, rsa, rsb = _matmul(
        "latent_proj", [h16], [(0, w_lat, "nk", 0, lat_w)], _latent_epilogue,
        [((n, Q_LORA), BF16, (tm, Q_LORA), _row_i),
         ((n, KV_LORA), BF16, (tm, KV_LORA), _row_i),
         ((n, LANES), BF16, (tm, LANES), _row_i), table, table, table],
        tm=tm, tn=lat_w,
        extras=[(q_norm_g[l].reshape(1, -1), (1, Q_LORA), lambda i, j: (0, 0)),
                (kv_norm_g[l].reshape(1, -1), (1, KV_LORA), lambda i, j: (0, 0)),
                (positions.reshape(n, 1), (tm, 1), _row_i),
                (_rope_frequencies(), (1, LANES), lambda i, j: (0, 0))])
    rope_specs = [(rc, (tm, LANES), _row_i), (rsa, (tm, LANES), _row_i), (rsb, (tm, LANES), _row_i)]

    scale = (NOPE + ROPE) ** -0.5
    q_w = MLA_HEADS * HEAD_W
    (qf,) = _matmul("q_up_proj", [qn], [(0, _uq_weight(w_uq[l]), "kn", 0, q_w)],
                    functools.partial(_uq_epilogue, scale=scale),
                    [((n, q_w), BF16, (tm, 2048), _tile_ij)],
                    tm=tm, tn=2048, extras=rope_specs)
    kv_w = MLA_HEADS * KV_W
    (kv,) = _matmul("kv_up_proj", [kvn], [(0, w_ukv[l], "kn", 0, kv_w)], _ep_ident,
                    [((n, kv_w), BF16, (2048, 2048), _tile_ij)], tm=2048, tn=2048)
    o_a = _attention(qf, kv, kr, batch, seq, MLA_HEADS)

    tm, tn = 1024, 1024

    def proj(name, col0, ep, dtype, extras=()):
        (r,) = _matmul(name, [h16], [(0, wt, "nk", col0, HG_WIDTH)], ep,
                       [((n, HG_WIDTH), dtype, (tm, tn), _tile_ij)],
                       tm=tm, tn=tn, extras=list(extras))
        return r

    qs = proj("hg_q_proj", o_hq, _ep_silu, BF16)
    fg = proj("hg_f_proj", o_hf, _ep_forget, F32, [(hg_lb, (hg_lb.shape[0], tn), _col_j)])
    iv = proj("hg_i_proj", o_hi, _ep_ident, BF16)
    gs = proj("hg_g_proj", o_hg, _ep_silu, BF16)
    o_b = _hgrn2(qs, fg, iv, gs, hg_norm_g[l], batch, seq, HG_HEADS)

    tm, tn = 512, 512
    (merged,) = _matmul(
        "gated_merge", [h16, o_a, o_b],
        [(0, wt, "nk", o_ga, d), (1, w_branch_a[l], "kn", 0, d),
         (0, wt, "nk", o_gb, d), (2, w_branch_b[l], "kn", 0, d)],
        _ep_merge, [((n, d), BF16, (tm, tn), _tile_ij)], tm=tm, tn=tn)

    tm, tn = 512, 1024
    (y1,) = _matmul("out_proj", [merged], [(0, w_out[l], "kn", 0, d)], _ep_residual,
                    [((n, d), F32, (tm, tn), _tile_ij)], tm=tm, tn=tn,
                    extras=residual_specs(x2, h_mu, h_rstd, ln_in_g, ln_in_b, tm, tn))
    h1_16, h1_mu, h1_rstd = _layer_norm("ln1", y1, ln1_g[l], ln1_b[l], BF16, True)

    tm, tn = 2048, 256
    (act,) = _matmul("swiglu_up", [h1_16], [(0, w_gate[l], "kn", 0, D_FF), (0, w_up[l], "kn", 0, D_FF)],
                     _ep_swiglu, [((n, D_FF), BF16, (tm, tn), _tile_ij)], tm=tm, tn=tn)
    tm, tn = 512, 512
    (y2,) = _matmul("swiglu_down", [act], [(0, w_down[l], "kn", 0, d)], _ep_residual,
                    [((n, d), F32, (tm, tn), _tile_ij)], tm=tm, tn=tn,
                    extras=residual_specs(y1, h1_mu, h1_rstd, ln1_g[l], ln1_b[l], tm, tn))
    (out,) = _layer_norm("ln2", y2, ln2_g[l], ln2_b[l], F32, False)
    return out.reshape(batch, seq, d)
```

```python
import functools

import jax
import jax.numpy as jnp
from jax import lax
from jax.experimental import pallas as pl
from jax.experimental.pallas import tpu as pltpu

F32 = jnp.float32
BF16 = jnp.bfloat16

D_MODEL = 4096
CHUNK = 64
MLA_HEADS = 16
NOPE = 128
ROPE = 64
V_DIM = 128
Q_LORA = 1024
KV_LORA = 512
ROPE_THETA = 10000.0
HG_HEADS = 16
HG_DIM = 128
HG_WIDTH = HG_HEADS * HG_DIM
D_FF = 11008
DEPTH = 1
ALPHA = (2 * DEPTH) ** 0.25
EPS = 1e-5
LOG2_E = 1.4426950408889634

LANES = 128
SUBLANES = 8
VMEM_LIMIT = 60 * 1024 * 1024

HEAD_W = 2 * LANES
KV_W = NOPE + V_DIM
SUB = 8
HG_CHUNK = 128
ATT_TQ = 512
FIRST_TILE_DEPTH = 4


def _cparams(n_grid, vmem=VMEM_LIMIT):
    return pltpu.CompilerParams(dimension_semantics=("arbitrary",) * n_grid,
                                vmem_limit_bytes=vmem)


def _weight_chunk_copy(w_hbm, stage, sem, spec, jj, ci):
    kind, (col0, tn, cr) = spec
    if kind == "kn":
        src = w_hbm.at[pl.ds(pl.multiple_of(ci * cr, SUBLANES), cr),
                       pl.ds(pl.multiple_of(col0 + jj * tn, LANES), tn)]
    else:
        src = w_hbm.at[pl.ds(pl.multiple_of(col0 + jj * tn + ci * cr, SUBLANES), cr), :]
    return pltpu.make_async_copy(src, stage, sem)


def _mm_kernel(*refs, pair_x, w_specs, n_x, n_extra, n_out, epilogue):
    n_pairs = len(pair_x)
    xs = refs[:n_x]
    w_refs = refs[n_x:n_x + n_pairs]
    extras = refs[n_x + n_pairs:n_x + n_pairs + n_extra]
    outs = refs[n_x + n_pairs + n_extra:n_x + n_pairs + n_extra + n_out]
    scratch = list(refs[n_x + n_pairs + n_extra + n_out:])
    j, i = pl.program_id(0), pl.program_id(1)
    nj, ni = pl.num_programs(0), pl.num_programs(1)
    slot = j % 2

    streamed = []
    tiles = []
    for wr, spec in zip(w_refs, w_specs):
        if spec[1] is None:
            tiles.append(wr)
        else:
            tile16, stages, sems = scratch[:3]
            del scratch[:3]
            streamed.append((wr, spec, tile16, stages, sems))
            tiles.append(tile16.at[slot])

    def chunk_rows(spec, ci):
        cr = spec[1][2]
        return pl.ds(pl.multiple_of(ci * cr, SUBLANES), cr)

    @pl.when((j == 0) & (i == 0))
    def _():
        depth = FIRST_TILE_DEPTH

        def first(k, ci):
            wr, spec, _, stages, sems = streamed[k]
            b = ci % depth
            return _weight_chunk_copy(wr, stages.at[b], sems.at[b], spec, 0, ci)

        for ci in range(depth - 1):
            @pl.when(ci < ni)
            def _():
                for k in range(len(streamed)):
                    first(k, ci).start()

        def load(ci, carry):
            for k, (_, spec, tile16, stages, _) in enumerate(streamed):
                @pl.when(ci + depth - 1 < ni)
                def _():
                    first(k, ci + depth - 1).start()

                first(k, ci).wait()
                tile16[0, chunk_rows(spec, ci), :] = stages[ci % depth].astype(BF16)
            return carry
        lax.fori_loop(0, ni, load, 0)

        for wr, spec, _, stages, sems in streamed:
            @pl.when(nj > 1)
            def _():
                _weight_chunk_copy(wr, stages.at[0], sems.at[0], spec, 1, 0).start()

    accs = []
    for xi, w, spec in zip(pair_x, tiles, w_specs):
        if spec[0] == "kn":
            accs.append(jnp.dot(xs[xi][...], w[...], preferred_element_type=F32))
        else:
            accs.append(lax.dot_general(xs[xi][...], w[...], (((1,), (1,)), ((), ())),
                                        preferred_element_type=F32))
    res = epilogue(accs, [e[...] for e in extras])
    for o, r in zip(outs, res):
        o[...] = r.astype(o.dtype)

    for wr, spec, tile16, stages, sems in streamed:
        stage, sem = stages.at[0], sems.at[0]

        @pl.when(j + 1 < nj)
        def _():
            _weight_chunk_copy(wr, stage, sem, spec, j + 1, i).wait()
            tile16[1 - slot, chunk_rows(spec, i), :] = stage[...].astype(BF16)

        @pl.when((j + 1 < nj) & (i + 1 < ni))
        def _():
            _weight_chunk_copy(wr, stage, sem, spec, j + 1, i + 1).start()

        @pl.when((i + 1 == ni) & (j + 2 < nj))
        def _():
            _weight_chunk_copy(wr, stage, sem, spec, j + 2, 0).start()


def _matmul(name, xs, pairs, epilogue, outs, *, tm, tn, extras=()):
    m = xs[0].shape[0]
    n = pairs[0][4]
    assert m % tm == 0 and n % tn == 0
    nj, ni = n // tn, m // tm
    grid = (nj, ni)

    in_specs = [pl.BlockSpec((tm, x.shape[1]), lambda j, i: (i, 0)) for x in xs]
    scratch, w_specs = [], []
    for _, w, kind, col0, ncols in pairs:
        assert ncols == n
        if w.dtype == BF16:
            assert col0 % tn == 0
            if kind == "kn":
                in_specs.append(pl.BlockSpec((w.shape[0], tn),
                                             lambda j, i, c0=col0 // tn: (0, c0 + j)))
            else:
                in_specs.append(pl.BlockSpec((tn, w.shape[1]),
                                             lambda j, i, c0=col0 // tn: (c0 + j, 0)))
            w_specs.append((kind, None))
            continue
        rows = w.shape[0] if kind == "kn" else tn
        lanes = tn if kind == "kn" else w.shape[1]
        assert rows % (ni * SUBLANES) == 0 and col0 % SUBLANES == 0
        if kind == "kn":
            assert col0 % LANES == 0 and tn % LANES == 0
        cr = rows // ni
        in_specs.append(pl.BlockSpec(memory_space=pl.ANY))
        scratch += [pltpu.VMEM((2, rows, lanes), BF16),
                    pltpu.VMEM((FIRST_TILE_DEPTH, cr, lanes), F32),
                    pltpu.SemaphoreType.DMA((FIRST_TILE_DEPTH,))]
        w_specs.append((kind, (col0, tn, cr)))
    for _, bs, im in extras:
        in_specs.append(pl.BlockSpec(bs, functools.partial(_swap_ji, im)))
    out_specs = [pl.BlockSpec(bs, functools.partial(_swap_ji, im)) for _, _, bs, im in outs]
    out_shape = [jax.ShapeDtypeStruct(s, d) for s, d, _, _ in outs]

    kern = functools.partial(_mm_kernel, pair_x=tuple(p[0] for p in pairs),
                             w_specs=tuple(w_specs), n_x=len(xs), n_extra=len(extras),
                             n_out=len(outs), epilogue=epilogue)
    return pl.pallas_call(
        kern, grid=grid, in_specs=in_specs, out_specs=out_specs, out_shape=out_shape,
        scratch_shapes=scratch, compiler_params=_cparams(2), name=name,
    )(*xs, *[p[1] for p in pairs], *[e[0] for e in extras])


def _swap_ji(im, j, i):
    return im(i, j)


def _tile_ij(i, j):
    return (i, j)


def _row_i(i, j):
    return (i, 0)


def _col_j(i, j):
    return (0, j)


def _sigmoid(x):
    return 0.5 * jnp.tanh(0.5 * x) + 0.5


def _silu(x):
    h = 0.5 * x
    return h + h * jnp.tanh(h)


def _ln_apply(y, mu, rstd, g, b):
    return (y - mu) * rstd * g + b


def _ln_kernel(y_ref, g_ref, b_ref, o_ref, *stat_refs):
    y = y_ref[...]
    mu = jnp.mean(y, axis=-1, keepdims=True)
    d = y - mu
    rstd = lax.rsqrt(jnp.mean(d * d, axis=-1, keepdims=True) + EPS)
    o_ref[...] = _ln_apply(y, mu, rstd, g_ref[...], b_ref[...]).astype(o_ref.dtype)
    if stat_refs:
        mu_ref, rstd_ref = stat_refs
        mu_ref[...] = jnp.broadcast_to(mu, mu_ref.shape)
        rstd_ref[...] = jnp.broadcast_to(rstd, rstd_ref.shape)


def _layer_norm(name, y, g, b, out_dtype, with_stats, tm=512):
    m, d = y.shape
    out_specs = [pl.BlockSpec((tm, d), lambda i: (i, 0))]
    out_shape = [jax.ShapeDtypeStruct((m, d), out_dtype)]
    if with_stats:
        out_specs += [pl.BlockSpec((tm, LANES), lambda i: (i, 0))] * 2
        out_shape += [jax.ShapeDtypeStruct((m, LANES), F32)] * 2
    return pl.pallas_call(
        _ln_kernel, grid=(m // tm,), name=name,
        in_specs=[pl.BlockSpec((tm, d), lambda i: (i, 0)),
                  pl.BlockSpec((1, d), lambda i: (0, 0)),
                  pl.BlockSpec((1, d), lambda i: (0, 0))],
        out_specs=out_specs, out_shape=out_shape,
        compiler_params=_cparams(1),
    )(y, g.reshape(1, d), b.reshape(1, d))


def _rope_tables(pos, invf):
    ang = pos.astype(F32) * invf
    lane = lax.broadcasted_iota(jnp.int32, ang.shape, 1)
    cos = jnp.cos(ang)
    sin = jnp.sin(ang)
    half = ROPE // 2
    return (cos, jnp.where(lane < half, -sin, 0.0),
            jnp.where((lane >= half) & (lane < ROPE), sin, 0.0))


def _rope_frequencies():
    half = ROPE // 2
    inv_freq = ROPE_THETA ** (-jnp.arange(half, dtype=F32) / half)
    return jnp.concatenate([inv_freq, inv_freq, jnp.zeros((LANES - ROPE,), F32)]).reshape(1, LANES)


def _rope_lanes(x, c, sa, sb):
    half = ROPE // 2
    return (x * c + pltpu.roll(x, LANES - half, axis=1) * sa
            + pltpu.roll(x, half, axis=1) * sb)


def _rms(x, g):
    ms = jnp.mean(x * x, axis=-1, keepdims=True)
    return x * lax.rsqrt(ms + EPS) * g


def _latent_epilogue(accs, extras):
    acc = accs[0]
    gq, gkv, pos, invf = extras
    c, sa, sb = _rope_tables(pos, invf)
    qn = _rms(acc[:, :Q_LORA], gq)
    kvn = _rms(acc[:, Q_LORA:Q_LORA + KV_LORA], gkv)
    kr = _rope_lanes(acc[:, Q_LORA + KV_LORA:], c, sa, sb)
    return [qn, kvn, kr, c, sa, sb]


def _uq_epilogue(accs, extras, *, scale):
    acc = accs[0]
    c, sa, sb = extras
    tn = acc.shape[1]
    cols = []
    for h in range(tn // HEAD_W):
        base = h * HEAD_W
        cols.append(acc[:, base:base + NOPE] * scale)
        cols.append(_rope_lanes(acc[:, base + NOPE:base + HEAD_W], c, sa, sb) * scale)
    return [jnp.concatenate(cols, axis=1), accs[1]]


def _attn_kernel(q_ref, kv_ref, kr_ref, o_ref, kcat_ref, *, tq, hpb):
    t = q_ref.shape[0]
    for hh in range(hpb):
        kcat_ref[hh, :, :NOPE] = kv_ref[:, hh * KV_W:hh * KV_W + NOPE]
        kcat_ref[hh, :, NOPE:] = kr_ref[...]
    row_chunk = lax.broadcasted_iota(jnp.int32, (tq, tq), 0) // CHUNK
    col_chunk = lax.broadcasted_iota(jnp.int32, (tq, tq), 1) // CHUNK
    visible = col_chunk <= row_chunk
    nt = (((1,), (1,)), ((), ()))
    for jq in range(t // tq):
        s0, s1 = jq * tq, (jq + 1) * tq
        for hh in range(hpb):
            ol = slice(hh * V_DIM, (hh + 1) * V_DIM)
            vl = slice(hh * KV_W + NOPE, (hh + 1) * KV_W)
            q = q_ref[s0:s1, hh * HEAD_W:(hh + 1) * HEAD_W]
            sd = lax.dot_general(q, kcat_ref[hh, s0:s1, :], nt, preferred_element_type=F32)
            sd = jnp.where(visible, sd, -jnp.inf)
            m = jnp.max(sd, axis=-1, keepdims=True)
            if jq > 0:
                sp = lax.dot_general(q, kcat_ref[hh, 0:s0, :], nt, preferred_element_type=F32)
                m = jnp.maximum(m, jnp.max(sp, axis=-1, keepdims=True))
            pd = jnp.exp(sd - m)
            l = jnp.sum(pd, axis=-1, keepdims=True)
            o = jnp.dot(pd.astype(BF16), kv_ref[s0:s1, vl], preferred_element_type=F32)
            if jq > 0:
                pp = jnp.exp(sp - m)
                l = l + jnp.sum(pp, axis=-1, keepdims=True)
                o = o + jnp.dot(pp.astype(BF16), kv_ref[0:s0, vl], preferred_element_type=F32)
            o_ref[s0:s1, ol] = (o / l).astype(o_ref.dtype)


def _attention(qf, kv, kr, batch, seq, heads, hpb=4):
    n = batch * seq
    kern = functools.partial(_attn_kernel, tq=ATT_TQ, hpb=hpb)
    groups = heads // hpb
    return pl.pallas_call(
        kern, grid=(batch, groups), name="mla_attention",
        in_specs=[pl.BlockSpec((seq, hpb * HEAD_W), lambda b, h: (b, h)),
                  pl.BlockSpec((seq, hpb * KV_W), lambda b, h: (b, h)),
                  pl.BlockSpec((seq, LANES), lambda b, h: (b, 0))],
        out_specs=pl.BlockSpec((seq, hpb * V_DIM), lambda b, h: (b, h)),
        out_shape=jax.ShapeDtypeStruct((n, heads * V_DIM), BF16),
        scratch_shapes=[pltpu.VMEM((hpb, seq, HEAD_W), BF16)],
        compiler_params=_cparams(2),
    )(qf, kv, kr)


def _split3(x):
    a = x.astype(BF16)
    r = x - a.astype(F32)
    b = r.astype(BF16)
    c = (r - b.astype(F32)).astype(BF16)
    return a, b, c


def _chunk_rows(n):
    r0 = n * HG_CHUNK
    return pl.ds(r0 if isinstance(n, int) else pl.multiple_of(r0, HG_CHUNK), HG_CHUNK)


def _hgrn_decay_stage(n, hh, refs, masks):
    d = HG_DIM
    lanes = slice(hh * d, (hh + 1) * d)
    f = refs["f"][_chunk_rows(n), lanes]
    g = jnp.log(f) * LOG2_E
    cs = jnp.dot(masks[0], jnp.concatenate(_split3(g), axis=1), preferred_element_type=F32)
    b = cs[:, :d] + cs[:, d:2 * d] + cs[:, 2 * d:]
    refs["b"][n % 2, hh] = b
    refs["c"][n % 2, hh] = b - jnp.log(1.0 - f) * LOG2_E


def _hgrn_score_stage(n, hh, refs, masks):
    c, d, n_sub = HG_CHUNK, HG_DIM, HG_CHUNK // SUB
    m_diag, m_levels = masks[1], masks[2:]
    lanes = slice(hh * d, (hh + 1) * d)
    rows = _chunk_rows(n)
    slot = n % 2
    nt = (((1,), (1,)), ((), ()))

    q = refs["q"][rows, lanes].astype(F32)
    k = 1.0 - refs["f"][rows, lanes]
    b = refs["b"][slot, hh]
    b_last = b[c - 1:c, :]

    q3 = q.reshape(n_sub, SUB, d)
    b3 = b.reshape(n_sub, SUB, d)
    c3 = refs["c"][slot, hh].reshape(n_sub, SUB, d)
    a_ref = refs["a"]
    for s in range(SUB):
        lo = (s // SUBLANES) * SUBLANES
        a = q3[:, lo:, :] * jnp.exp2(jnp.minimum(b3[:, lo:, :] - c3[:, s:s + 1, :], 0.0))
        if lo:
            a = jnp.concatenate([jnp.zeros((n_sub, lo, d), F32), a], axis=1)
        a_ref[hh, :, s * d:(s + 1) * d] = a.reshape(c, d).astype(BF16)
    p_diag = jnp.dot(a_ref[hh], refs["sel"][...], preferred_element_type=F32)

    def level(hs):
        blk = 2 * hs
        qt, kt = [], []
        for i in range(0, c, hs):
            edge = (i // blk) * blk + hs - 1
            e = jnp.exp2(-jnp.abs(b[i:i + hs, :] - b[edge:edge + 1, :]))
            zero = jnp.zeros((hs, d), F32)
            upper = (i % blk) >= hs
            qt.append(q[i:i + hs, :] * e if upper else zero)
            kt.append(zero if upper else k[i:i + hs, :] * e)
        qt = jnp.concatenate(qt, axis=0).astype(BF16)
        kt = jnp.concatenate(kt, axis=0).astype(BF16)
        return lax.dot_general(qt, kt, nt, preferred_element_type=F32)

    p = level(c // 2)
    for idx in reversed(range(len(m_levels))):
        p = jnp.where(m_levels[idx], level(SUB << idx), p)
    p = jnp.where(m_diag, p_diag, p)
    refs["p"][slot, hh] = p.astype(BF16)
    refs["q0"][slot, hh] = (q * jnp.exp2(b)).astype(BF16)
    refs["kl"][slot, hh] = (k * jnp.exp2(b_last - b)).astype(BF16)
    refs["dl"][slot, hh] = jnp.broadcast_to(jnp.exp2(b_last), (SUBLANES, d))


def _hgrn_output_stage(n, hh, refs):
    d = HG_DIM
    lanes = slice(hh * d, (hh + 1) * d)
    rows = _chunk_rows(n)
    slot = n % 2
    nt = (((1,), (1,)), ((), ()))
    v = refs["i"][rows, lanes]
    st = refs["st"][hh]
    o = (jnp.dot(refs["p"][slot, hh], v, preferred_element_type=F32)
         + lax.dot_general(refs["q0"][slot, hh], st.astype(BF16), nt, preferred_element_type=F32))
    vt = v.astype(F32).T.astype(BF16)
    refs["st"][hh] = (refs["dl"][slot, hh][0:1, :] * st
                      + jnp.dot(vt, refs["kl"][slot, hh], preferred_element_type=F32))
    ms = jnp.mean(o * o, axis=-1, keepdims=True)
    ob = o * lax.rsqrt(ms + EPS) * refs["gn"][...] * refs["gs"][rows, lanes].astype(F32)
    refs["o"][rows, lanes] = ob.astype(refs["o"].dtype)


def _hgrn_kernel(q_ref, f_ref, i_ref, gs_ref, gn_ref, sel_ref, o_ref, st_ref, a_ref, b_ref,
                 c_ref, p_ref, q0_ref, kl_ref, dl_ref, *, hpb):
    nc = q_ref.shape[0] // HG_CHUNK
    assert nc >= 3
    c, d = HG_CHUNK, HG_DIM
    row = lax.broadcasted_iota(jnp.int32, (c, d), 0)
    col = lax.broadcasted_iota(jnp.int32, (c, d), 1)
    tri = (col <= row).astype(BF16)
    m_diag = ((row // SUB) == (col // SUB)) & ((col % SUB) <= (row % SUB))
    blks = [2 * SUB << i for i in range(64) if 2 * SUB << i < c]
    masks = (tri, m_diag) + tuple((row // blk) == (col // blk) for blk in blks)
    refs = dict(q=q_ref, f=f_ref, i=i_ref, gs=gs_ref, gn=gn_ref, sel=sel_ref, o=o_ref,
                st=st_ref, a=a_ref, b=b_ref, c=c_ref, p=p_ref, q0=q0_ref, kl=kl_ref, dl=dl_ref)

    st_ref[...] = jnp.zeros_like(st_ref)

    def run(n_out, n_score, n_decay):
        for hh in range(hpb):
            if n_out is not None:
                _hgrn_output_stage(n_out, hh, refs)
            if n_score is not None:
                _hgrn_score_stage(n_score, hh, refs, masks)
            if n_decay is not None:
                _hgrn_decay_stage(n_decay, hh, refs, masks)

    run(None, None, 0)
    run(None, 0, 1)

    def steady(n, carry):
        run(n - 2, n - 1, n)
        return carry

    lax.fori_loop(2, nc, steady, 0)
    run(nc - 2, nc - 1, None)
    run(nc - 1, None, None)


def _hgrn2(qs, f, iv, gs, gnorm, batch, seq, heads, hpb=4):
    n = batch * seq
    d = HG_DIM
    sel = (jnp.arange(SUB * d)[:, None] // d == jnp.arange(d)[None, :] % SUB).astype(BF16)
    spec = pl.BlockSpec((seq, hpb * d), lambda b, h: (b, h))
    return pl.pallas_call(
        functools.partial(_hgrn_kernel, hpb=hpb), grid=(batch, heads // hpb), name="hgrn2",
        in_specs=[spec, spec, spec, spec, pl.BlockSpec((1, d), lambda b, h: (0, 0)),
                  pl.BlockSpec((SUB * d, d), lambda b, h: (0, 0))],
        out_specs=spec,
        out_shape=jax.ShapeDtypeStruct((n, heads * d), BF16),
        scratch_shapes=[pltpu.VMEM((hpb, d, d), F32),
                        pltpu.VMEM((hpb, HG_CHUNK, SUB * d), BF16),
                        pltpu.VMEM((2, hpb, HG_CHUNK, d), F32),
                        pltpu.VMEM((2, hpb, HG_CHUNK, d), F32),
                        pltpu.VMEM((2, hpb, HG_CHUNK, d), BF16),
                        pltpu.VMEM((2, hpb, HG_CHUNK, d), BF16),
                        pltpu.VMEM((2, hpb, HG_CHUNK, d), BF16),
                        pltpu.VMEM((2, hpb, SUBLANES, d), F32)],
        compiler_params=_cparams(2),
    )(qs, f, iv, gs, gnorm.reshape(1, d), sel)


def _ep_silu(accs, extras):
    return [_silu(accs[0])]


def _ep_ident(accs, extras):
    return [accs[0]]


def _ep_forget(accs, extras):
    lbp = extras[0]
    m = jnp.max(lbp, axis=0, keepdims=True)
    e = jnp.exp(lbp - m)
    lb = e[0:1, :] / jnp.sum(e, axis=0, keepdims=True)
    return [lb + (1.0 - lb) * _sigmoid(accs[0])]


def _ep_merge(accs, extras):
    ga, a, gb, b = accs
    return [_sigmoid(ga) * a + _sigmoid(gb) * b]


def _ep_residual(accs, extras):
    y, mu, rstd, g, b = extras
    return [ALPHA * _ln_apply(y, mu[:, :1], rstd[:, :1], g, b) + accs[0]]


def _ep_swiglu(accs, extras):
    return [_silu(accs[0]) * accs[1]]


def _uq_weight_kernel(w_ref, o_ref):
    src = NOPE + ROPE
    pad = jnp.zeros((w_ref.shape[0], HEAD_W - src), o_ref.dtype)
    for h in range(MLA_HEADS):
        o_ref[:, h * HEAD_W:h * HEAD_W + src] = w_ref[:, h * src:(h + 1) * src].astype(o_ref.dtype)
        o_ref[:, h * HEAD_W + src:(h + 1) * HEAD_W] = pad


def _uq_weight(w_uq, tr=256):
    r = w_uq.shape[0]
    return pl.pallas_call(
        _uq_weight_kernel, grid=(r // tr,), name="q_up_weight_layout",
        in_specs=[pl.BlockSpec((tr, w_uq.shape[1]), lambda i: (i, 0))],
        out_specs=pl.BlockSpec((tr, MLA_HEADS * HEAD_W), lambda i: (i, 0)),
        out_shape=jax.ShapeDtypeStruct((r, MLA_HEADS * HEAD_W), BF16),
        compiler_params=_cparams(1),
    )(w_uq)


def kernel(x, positions, ln_in_g, ln_in_b, w_in, q_norm_g, w_uq, kv_norm_g, w_ukv, hg_lb,
           hg_norm_g, w_branch_a, w_branch_b, w_out, ln1_g, ln1_b, w_gate, w_up, w_down,
           ln2_g, ln2_b):
    batch, seq, d = x.shape
    n = batch * seq
    assert w_in.shape[0] == DEPTH == 1
    l = 0
    wt = w_in.reshape(d, w_in.shape[-1]).T

    o_kr = Q_LORA + KV_LORA
    o_hq = o_kr + ROPE
    o_hf, o_hi, o_hg = o_hq + HG_WIDTH, o_hq + 2 * HG_WIDTH, o_hq + 3 * HG_WIDTH
    o_ga = o_hq + 4 * HG_WIDTH
    o_gb = o_ga + D_MODEL

    x2 = x.reshape(n, d)
    h16, h_mu, h_rstd = _layer_norm("ln_in", x2, ln_in_g, ln_in_b, BF16, True)

    def residual_specs(y, mu, rstd, g, b, tm, tn):
        return [(y, (tm, tn), _tile_ij), (mu, (tm, LANES), _row_i), (rstd, (tm, LANES), _row_i),
                (g.reshape(1, d), (1, tn), _col_j), (b.reshape(1, d), (1, tn), _col_j)]

    lat_w = o_kr + LANES
    w_lat = jnp.concatenate([wt[:o_hq], jnp.zeros((LANES - ROPE, d), wt.dtype)],
                            axis=0).astype(BF16)
    tm = 1024
    table = ((n, LANES), F32, (tm, LANES), _row_i)
    qn, kvn, kr, rc, rsa, rsb = _matmul(
        "latent_proj", [h16], [(0, w_lat, "nk", 0, lat_w)], _latent_epilogue,
        [((n, Q_LORA), BF16, (tm, Q_LORA), _row_i),
         ((n, KV_LORA), BF16, (tm, KV_LORA), _row_i),
         ((n, LANES), BF16, (tm, LANES), _row_i), table, table, table],
        tm=tm, tn=lat_w,
        extras=[(q_norm_g[l].reshape(1, -1), (1, Q_LORA), lambda i, j: (0, 0)),
                (kv_norm_g[l].reshape(1, -1), (1, KV_LORA), lambda i, j: (0, 0)),
                (positions.reshape(n, 1), (tm, 1), _row_i),
                (_rope_frequencies(), (1, LANES), lambda i, j: (0, 0))])
    rope_specs = [(rc, (tm, LANES), _row_i), (rsa, (tm, LANES), _row_i), (rsb, (tm, LANES), _row_i)]

    scale = (NOPE + ROPE) ** -0.5
    q_w = MLA_HEADS * HEAD_W
    kv_w = MLA_HEADS * KV_W
    assert q_w == kv_w
    qf, kv = _matmul("mla_up_proj", [qn, kvn],
                     [(0, _uq_weight(w_uq[l]), "kn", 0, q_w), (1, w_ukv[l], "kn", 0, kv_w)],
                     functools.partial(_uq_epilogue, scale=scale),
                     [((n, q_w), BF16, (tm, 2048), _tile_ij),
                      ((n, kv_w), BF16, (tm, 2048), _tile_ij)],
                     tm=tm, tn=2048, extras=rope_specs)
    o_a = _attention(qf, kv, kr, batch, seq, MLA_HEADS)

    tm, tn = 1024, 1024

    def proj(name, col0, ep, dtype, extras=()):
        (r,) = _matmul(name, [h16], [(0, wt, "nk", col0, HG_WIDTH)], ep,
                       [((n, HG_WIDTH), dtype, (tm, tn), _tile_ij)],
                       tm=tm, tn=tn, extras=list(extras))
        return r

    qs = proj("hg_q_proj", o_hq, _ep_silu, BF16)
    fg = proj("hg_f_proj", o_hf, _ep_forget, F32, [(hg_lb, (hg_lb.shape[0], tn), _col_j)])
    iv = proj("hg_i_proj", o_hi, _ep_ident, BF16)
    gs = proj("hg_g_proj", o_hg, _ep_silu, BF16)
    o_b = _hgrn2(qs, fg, iv, gs, hg_norm_g[l], batch, seq, HG_HEADS)

    tm, tn = 512, 512
    (merged,) = _matmul(
        "gated_merge", [h16, o_a, o_b],
        [(0, wt, "nk", o_ga, d), (1, w_branch_a[l], "kn", 0, d),
         (0, wt, "nk", o_gb, d), (2, w_branch_b[l], "kn", 0, d)],
        _ep_merge, [((n, d), BF16, (tm, tn), _tile_ij)], tm=tm, tn=tn)

    tm, tn = 512, 1024
    (y1,) = _matmul("out_proj", [merged], [(0, w_out[l], "kn", 0, d)], _ep_residual,
                    [((n, d), F32, (tm, tn), _tile_ij)], tm=tm, tn=tn,
                    extras=residual_specs(x2, h_mu, h_rstd, ln_in_g, ln_in_b, tm, tn))
    h1_16, h1_mu, h1_rstd = _layer_norm("ln1", y1, ln1_g[l], ln1_b[l], BF16, True)

    tm, tn = 2048, 256
    (act,) = _matmul("swiglu_up", [h1_16], [(0, w_gate[l], "kn", 0, D_FF), (0, w_up[l], "kn", 0, D_FF)],
                     _ep_swiglu, [((n, D_FF), BF16, (tm, tn), _tile_ij)], tm=tm, tn=tn)
    tm, tn = 512, 512
    (y2,) = _matmul("swiglu_down", [act], [(0, w_down[l], "kn", 0, d)], _ep_residual,
                    [((n, d), F32, (tm, tn), _tile_ij)], tm=tm, tn=tn,
                    extras=residual_specs(y1, h1_mu, h1_rstd, ln1_g[l], ln1_b[l], tm, tn))
    (out,) = _layer_norm("ln2", y2, ln2_g[l], ln2_b[l], F32, False)
    return out.reshape(batch, seq, d)
```

```python
import functools

import jax
import jax.numpy as jnp
from jax import lax
from jax.experimental import pallas as pl
from jax.experimental.pallas import tpu as pltpu

F32 = jnp.float32
BF16 = jnp.bfloat16

D_MODEL = 4096
CHUNK = 64
MLA_HEADS = 16
NOPE = 128
ROPE = 64
V_DIM = 128
Q_LORA = 1024
KV_LORA = 512
ROPE_THETA = 10000.0
HG_HEADS = 16
HG_DIM = 128
HG_WIDTH = HG_HEADS * HG_DIM
D_FF = 11008
DEPTH = 1
ALPHA = (2 * DEPTH) ** 0.25
EPS = 1e-5
LOG2_E = 1.4426950408889634

LANES = 128
SUBLANES = 8
VMEM_LIMIT = 60 * 1024 * 1024

HEAD_W = 2 * LANES
KV_W = NOPE + V_DIM
SUB = 8
HG_CHUNK = 128
ATT_TQ = 512
FIRST_TILE_DEPTH = 4
WEIGHT_DMA_PRIORITY = 1


def _cparams(n_grid, vmem=VMEM_LIMIT):
    return pltpu.CompilerParams(dimension_semantics=("arbitrary",) * n_grid,
                                vmem_limit_bytes=vmem)


def _weight_chunk_copy(w_hbm, stage, sem, spec, jj, ci):
    kind, (col0, tn, cr) = spec
    if kind == "kn":
        src = w_hbm.at[pl.ds(pl.multiple_of(ci * cr, SUBLANES), cr),
                       pl.ds(pl.multiple_of(col0 + jj * tn, LANES), tn)]
    else:
        src = w_hbm.at[pl.ds(pl.multiple_of(col0 + jj * tn + ci * cr, SUBLANES), cr), :]
    return pltpu.make_async_copy(src, stage, sem)


def _mm_kernel(*refs, pair_x, w_specs, n_x, n_extra, n_out, epilogue):
    n_pairs = len(pair_x)
    xs = refs[:n_x]
    w_refs = refs[n_x:n_x + n_pairs]
    extras = refs[n_x + n_pairs:n_x + n_pairs + n_extra]
    outs = refs[n_x + n_pairs + n_extra:n_x + n_pairs + n_extra + n_out]
    scratch = list(refs[n_x + n_pairs + n_extra + n_out:])
    j, i = pl.program_id(0), pl.program_id(1)
    nj, ni = pl.num_programs(0), pl.num_programs(1)
    slot = j % 2

    streamed = []
    tiles = []
    for wr, spec in zip(w_refs, w_specs):
        if spec[1] is None:
            tiles.append(wr)
        else:
            tile16, stages, sems = scratch[:3]
            del scratch[:3]
            streamed.append((wr, spec, tile16, stages, sems))
            tiles.append(tile16.at[slot])

    def chunk_rows(spec, ci):
        cr = spec[1][2]
        return pl.ds(pl.multiple_of(ci * cr, SUBLANES), cr)

    @pl.when((j == 0) & (i == 0))
    def _():
        depth = FIRST_TILE_DEPTH

        def first(k, ci):
            wr, spec, _, stages, sems = streamed[k]
            b = ci % depth
            return _weight_chunk_copy(wr, stages.at[b], sems.at[b], spec, 0, ci)

        for ci in range(depth - 1):
            @pl.when(ci < ni)
            def _():
                for k in range(len(streamed)):
                    first(k, ci).start(priority=WEIGHT_DMA_PRIORITY)

        def load(ci, carry):
            for k, (_, spec, tile16, stages, _) in enumerate(streamed):
                @pl.when(ci + depth - 1 < ni)
                def _():
                    first(k, ci + depth - 1).start(priority=WEIGHT_DMA_PRIORITY)

                first(k, ci).wait()
                tile16[0, chunk_rows(spec, ci), :] = stages[ci % depth].astype(BF16)
            return carry
        lax.fori_loop(0, ni, load, 0)

        for wr, spec, _, stages, sems in streamed:
            @pl.when(nj > 1)
            def _():
                _weight_chunk_copy(wr, stages.at[0], sems.at[0], spec, 1, 0).start(priority=WEIGHT_DMA_PRIORITY)

    accs = []
    for xi, w, spec in zip(pair_x, tiles, w_specs):
        if spec[0] == "kn":
            accs.append(jnp.dot(xs[xi][...], w[...], preferred_element_type=F32))
        else:
            accs.append(lax.dot_general(xs[xi][...], w[...], (((1,), (1,)), ((), ())),
                                        preferred_element_type=F32))
    res = epilogue(accs, [e[...] for e in extras])
    for o, r in zip(outs, res):
        o[...] = r.astype(o.dtype)

    for wr, spec, tile16, stages, sems in streamed:
        stage, sem = stages.at[0], sems.at[0]

        @pl.when(j + 1 < nj)
        def _():
            _weight_chunk_copy(wr, stage, sem, spec, j + 1, i).wait()
            tile16[1 - slot, chunk_rows(spec, i), :] = stage[...].astype(BF16)

        @pl.when((j + 1 < nj) & (i + 1 < ni))
        def _():
            _weight_chunk_copy(wr, stage, sem, spec, j + 1, i + 1).start(priority=WEIGHT_DMA_PRIORITY)

        @pl.when((i + 1 == ni) & (j + 2 < nj))
        def _():
            _weight_chunk_copy(wr, stage, sem, spec, j + 2, 0).start(priority=WEIGHT_DMA_PRIORITY)


def _matmul(name, xs, pairs, epilogue, outs, *, tm, tn, extras=()):
    m = xs[0].shape[0]
    n = pairs[0][4]
    assert m % tm == 0 and n % tn == 0
    nj, ni = n // tn, m // tm
    grid = (nj, ni)

    in_specs = [pl.BlockSpec((tm, x.shape[1]), lambda j, i: (i, 0)) for x in xs]
    scratch, w_specs = [], []
    for _, w, kind, col0, ncols in pairs:
        assert ncols == n
        if w.dtype == BF16:
            assert col0 % tn == 0
            if kind == "kn":
                in_specs.append(pl.BlockSpec((w.shape[0], tn),
                                             lambda j, i, c0=col0 // tn: (0, c0 + j)))
            else:
                in_specs.append(pl.BlockSpec((tn, w.shape[1]),
                                             lambda j, i, c0=col0 // tn: (c0 + j, 0)))
            w_specs.append((kind, None))
            continue
        rows = w.shape[0] if kind == "kn" else tn
        lanes = tn if kind == "kn" else w.shape[1]
        assert rows % (ni * SUBLANES) == 0 and col0 % SUBLANES == 0
        if kind == "kn":
            assert col0 % LANES == 0 and tn % LANES == 0
        cr = rows // ni
        in_specs.append(pl.BlockSpec(memory_space=pl.ANY))
        scratch += [pltpu.VMEM((2, rows, lanes), BF16),
                    pltpu.VMEM((FIRST_TILE_DEPTH, cr, lanes), F32),
                    pltpu.SemaphoreType.DMA((FIRST_TILE_DEPTH,))]
        w_specs.append((kind, (col0, tn, cr)))
    for _, bs, im in extras:
        in_specs.append(pl.BlockSpec(bs, functools.partial(_swap_ji, im)))
    out_specs = [pl.BlockSpec(bs, functools.partial(_swap_ji, im)) for _, _, bs, im in outs]
    out_shape = [jax.ShapeDtypeStruct(s, d) for s, d, _, _ in outs]

    kern = functools.partial(_mm_kernel, pair_x=tuple(p[0] for p in pairs),
                             w_specs=tuple(w_specs), n_x=len(xs), n_extra=len(extras),
                             n_out=len(outs), epilogue=epilogue)
    return pl.pallas_call(
        kern, grid=grid, in_specs=in_specs, out_specs=out_specs, out_shape=out_shape,
        scratch_shapes=scratch, compiler_params=_cparams(2), name=name,
    )(*xs, *[p[1] for p in pairs], *[e[0] for e in extras])


def _swap_ji(im, j, i):
    return im(i, j)


def _tile_ij(i, j):
    return (i, j)


def _row_i(i, j):
    return (i, 0)


def _col_j(i, j):
    return (0, j)


def _sigmoid(x):
    return 0.5 * jnp.tanh(0.5 * x) + 0.5


def _silu(x):
    h = 0.5 * x
    return h + h * jnp.tanh(h)


def _ln_apply(y, mu, rstd, g, b):
    return (y - mu) * rstd * g + b


def _ln_kernel(y_ref, g_ref, b_ref, o_ref, *stat_refs):
    y = y_ref[...]
    mu = jnp.mean(y, axis=-1, keepdims=True)
    d = y - mu
    rstd = lax.rsqrt(jnp.mean(d * d, axis=-1, keepdims=True) + EPS)
    o_ref[...] = _ln_apply(y, mu, rstd, g_ref[...], b_ref[...]).astype(o_ref.dtype)
    if stat_refs:
        mu_ref, rstd_ref = stat_refs
        mu_ref[...] = jnp.broadcast_to(mu, mu_ref.shape)
        rstd_ref[...] = jnp.broadcast_to(rstd, rstd_ref.shape)


def _layer_norm(name, y, g, b, out_dtype, with_stats, tm=512):
    m, d = y.shape
    out_specs = [pl.BlockSpec((tm, d), lambda i: (i, 0))]
    out_shape = [jax.ShapeDtypeStruct((m, d), out_dtype)]
    if with_stats:
        out_specs += [pl.BlockSpec((tm, LANES), lambda i: (i, 0))] * 2
        out_shape += [jax.ShapeDtypeStruct((m, LANES), F32)] * 2
    return pl.pallas_call(
        _ln_kernel, grid=(m // tm,), name=name,
        in_specs=[pl.BlockSpec((tm, d), lambda i: (i, 0)),
                  pl.BlockSpec((1, d), lambda i: (0, 0)),
                  pl.BlockSpec((1, d), lambda i: (0, 0))],
        out_specs=out_specs, out_shape=out_shape,
        compiler_params=_cparams(1),
    )(y, g.reshape(1, d), b.reshape(1, d))


def _rope_tables(pos, invf):
    ang = pos.astype(F32) * invf
    lane = lax.broadcasted_iota(jnp.int32, ang.shape, 1)
    cos = jnp.cos(ang)
    sin = jnp.sin(ang)
    half = ROPE // 2
    return (cos, jnp.where(lane < half, -sin, 0.0),
            jnp.where((lane >= half) & (lane < ROPE), sin, 0.0))


def _rope_frequencies():
    half = ROPE // 2
    inv_freq = ROPE_THETA ** (-jnp.arange(half, dtype=F32) / half)
    return jnp.concatenate([inv_freq, inv_freq, jnp.zeros((LANES - ROPE,), F32)]).reshape(1, LANES)


def _rope_lanes(x, c, sa, sb):
    half = ROPE // 2
    return (x * c + pltpu.roll(x, LANES - half, axis=1) * sa
            + pltpu.roll(x, half, axis=1) * sb)


def _rms(x, g):
    ms = jnp.mean(x * x, axis=-1, keepdims=True)
    return x * lax.rsqrt(ms + EPS) * g


def _latent_epilogue(accs, extras):
    acc = accs[0]
    gq, gkv, pos, invf = extras
    c, sa, sb = _rope_tables(pos, invf)
    qn = _rms(acc[:, :Q_LORA], gq)
    kvn = _rms(acc[:, Q_LORA:Q_LORA + KV_LORA], gkv)
    kr = _rope_lanes(acc[:, Q_LORA + KV_LORA:], c, sa, sb)
    return [qn, kvn, kr, c, sa, sb]


def _uq_epilogue(accs, extras, *, scale):
    acc = accs[0]
    c, sa, sb = extras
    tn = acc.shape[1]
    cols = []
    for h in range(tn // HEAD_W):
        base = h * HEAD_W
        cols.append(acc[:, base:base + NOPE] * scale)
        cols.append(_rope_lanes(acc[:, base + NOPE:base + HEAD_W], c, sa, sb) * scale)
    return [jnp.concatenate(cols, axis=1)]


def _attn_kernel(q_ref, kv_ref, kr_ref, o_ref, kcat_ref, *, tq, hpb):
    t = q_ref.shape[0]
    for hh in range(hpb):
        kcat_ref[hh, :, :NOPE] = kv_ref[:, hh * KV_W:hh * KV_W + NOPE]
        kcat_ref[hh, :, NOPE:] = kr_ref[...]
    row_chunk = lax.broadcasted_iota(jnp.int32, (tq, tq), 0) // CHUNK
    col_chunk = lax.broadcasted_iota(jnp.int32, (tq, tq), 1) // CHUNK
    visible = col_chunk <= row_chunk
    nt = (((1,), (1,)), ((), ()))
    for jq in range(t // tq):
        s0, s1 = jq * tq, (jq + 1) * tq
        for hh in range(hpb):
            ol = slice(hh * V_DIM, (hh + 1) * V_DIM)
            vl = slice(hh * KV_W + NOPE, (hh + 1) * KV_W)
            q = q_ref[s0:s1, hh * HEAD_W:(hh + 1) * HEAD_W]
            sd = lax.dot_general(q, kcat_ref[hh, s0:s1, :], nt, preferred_element_type=F32)
            sd = jnp.where(visible, sd, -jnp.inf)
            m = jnp.max(sd, axis=-1, keepdims=True)
            if jq > 0:
                sp = lax.dot_general(q, kcat_ref[hh, 0:s0, :], nt, preferred_element_type=F32)
                m = jnp.maximum(m, jnp.max(sp, axis=-1, keepdims=True))
            pd = jnp.exp(sd - m)
            l = jnp.sum(pd, axis=-1, keepdims=True)
            o = jnp.dot(pd.astype(BF16), kv_ref[s0:s1, vl], preferred_element_type=F32)
            if jq > 0:
                pp = jnp.exp(sp - m)
                l = l + jnp.sum(pp, axis=-1, keepdims=True)
                o = o + jnp.dot(pp.astype(BF16), kv_ref[0:s0, vl], preferred_element_type=F32)
            o_ref[s0:s1, ol] = (o / l).astype(o_ref.dtype)


def _attention(qf, kv, kr, batch, seq, heads, hpb=4):
    n = batch * seq
    kern = functools.partial(_attn_kernel, tq=ATT_TQ, hpb=hpb)
    groups = heads // hpb
    return pl.pallas_call(
        kern, grid=(batch, groups), name="mla_attention",
        in_specs=[pl.BlockSpec((seq, hpb * HEAD_W), lambda b, h: (b, h)),
                  pl.BlockSpec((seq, hpb * KV_W), lambda b, h: (b, h)),
                  pl.BlockSpec((seq, LANES), lambda b, h: (b, 0))],
        out_specs=pl.BlockSpec((seq, hpb * V_DIM), lambda b, h: (b, h)),
        out_shape=jax.ShapeDtypeStruct((n, heads * V_DIM), BF16),
        scratch_shapes=[pltpu.VMEM((hpb, seq, HEAD_W), BF16)],
        compiler_params=_cparams(2),
    )(qf, kv, kr)


def _split3(x):
    a = x.astype(BF16)
    r = x - a.astype(F32)
    b = r.astype(BF16)
    c = (r - b.astype(F32)).astype(BF16)
    return a, b, c


def _chunk_rows(n):
    r0 = n * HG_CHUNK
    return pl.ds(r0 if isinstance(n, int) else pl.multiple_of(r0, HG_CHUNK), HG_CHUNK)


def _hgrn_decay_stage(n, hh, refs, masks):
    d = HG_DIM
    lanes = slice(hh * d, (hh + 1) * d)
    f = refs["f"][_chunk_rows(n), lanes]
    g = jnp.log(f) * LOG2_E
    cs = jnp.dot(masks[0], jnp.concatenate(_split3(g), axis=1), preferred_element_type=F32)
    b = cs[:, :d] + cs[:, d:2 * d] + cs[:, 2 * d:]
    refs["b"][n % 2, hh] = b
    refs["c"][n % 2, hh] = b - jnp.log(1.0 - f) * LOG2_E


def _hgrn_score_stage(n, hh, refs, masks):
    c, d, n_sub = HG_CHUNK, HG_DIM, HG_CHUNK // SUB
    m_diag, m_levels = masks[1], masks[2:]
    lanes = slice(hh * d, (hh + 1) * d)
    rows = _chunk_rows(n)
    slot = n % 2
    nt = (((1,), (1,)), ((), ()))

    q = refs["q"][rows, lanes].astype(F32)
    k = 1.0 - refs["f"][rows, lanes]
    b = refs["b"][slot, hh]
    b_last = b[c - 1:c, :]

    q3 = q.reshape(n_sub, SUB, d)
    b3 = b.reshape(n_sub, SUB, d)
    c3 = refs["c"][slot, hh].reshape(n_sub, SUB, d)
    a_ref = refs["a"]
    for s in range(SUB):
        lo = (s // SUBLANES) * SUBLANES
        a = q3[:, lo:, :] * jnp.exp2(jnp.minimum(b3[:, lo:, :] - c3[:, s:s + 1, :], 0.0))
        if lo:
            a = jnp.concatenate([jnp.zeros((n_sub, lo, d), F32), a], axis=1)
        a_ref[hh, :, s * d:(s + 1) * d] = a.reshape(c, d).astype(BF16)
    p_diag = jnp.dot(a_ref[hh], refs["sel"][...], preferred_element_type=F32)

    def level(hs):
        blk = 2 * hs
        qt, kt = [], []
        for i in range(0, c, hs):
            edge = (i // blk) * blk + hs - 1
            e = jnp.exp2(-jnp.abs(b[i:i + hs, :] - b[edge:edge + 1, :]))
            zero = jnp.zeros((hs, d), F32)
            upper = (i % blk) >= hs
            qt.append(q[i:i + hs, :] * e if upper else zero)
            kt.append(zero if upper else k[i:i + hs, :] * e)
        qt = jnp.concatenate(qt, axis=0).astype(BF16)
        kt = jnp.concatenate(kt, axis=0).astype(BF16)
        return lax.dot_general(qt, kt, nt, preferred_element_type=F32)

    p = level(c // 2)
    for idx in reversed(range(len(m_levels))):
        p = jnp.where(m_levels[idx], level(SUB << idx), p)
    p = jnp.where(m_diag, p_diag, p)
    refs["p"][slot, hh] = p.astype(BF16)
    refs["q0"][slot, hh] = (q * jnp.exp2(b)).astype(BF16)
    refs["kl"][slot, hh] = (k * jnp.exp2(b_last - b)).astype(BF16)
    refs["dl"][slot, hh] = jnp.broadcast_to(jnp.exp2(b_last), (SUBLANES, d))


def _hgrn_output_stage(n, hh, refs):
    d = HG_DIM
    lanes = slice(hh * d, (hh + 1) * d)
    rows = _chunk_rows(n)
    slot = n % 2
    nt = (((1,), (1,)), ((), ()))
    v = refs["i"][rows, lanes]
    st = refs["st"][hh]
    o = (jnp.dot(refs["p"][slot, hh], v, preferred_element_type=F32)
         + lax.dot_general(refs["q0"][slot, hh], st.astype(BF16), nt, preferred_element_type=F32))
    vt = v.astype(F32).T.astype(BF16)
    refs["st"][hh] = (refs["dl"][slot, hh][0:1, :] * st
                      + jnp.dot(vt, refs["kl"][slot, hh], preferred_element_type=F32))
    ms = jnp.mean(o * o, axis=-1, keepdims=True)
    ob = o * lax.rsqrt(ms + EPS) * refs["gn"][...] * refs["gs"][rows, lanes].astype(F32)
    refs["o"][rows, lanes] = ob.astype(refs["o"].dtype)


def _hgrn_kernel(q_ref, f_ref, i_ref, gs_ref, gn_ref, sel_ref, o_ref, st_ref, a_ref, b_ref,
                 c_ref, p_ref, q0_ref, kl_ref, dl_ref, *, hpb):
    nc = q_ref.shape[0] // HG_CHUNK
    assert nc >= 3
    c, d = HG_CHUNK, HG_DIM
    row = lax.broadcasted_iota(jnp.int32, (c, d), 0)
    col = lax.broadcasted_iota(jnp.int32, (c, d), 1)
    tri = (col <= row).astype(BF16)
    m_diag = ((row // SUB) == (col // SUB)) & ((col % SUB) <= (row % SUB))
    blks = [2 * SUB << i for i in range(64) if 2 * SUB << i < c]
    masks = (tri, m_diag) + tuple((row // blk) == (col // blk) for blk in blks)
    refs = dict(q=q_ref, f=f_ref, i=i_ref, gs=gs_ref, gn=gn_ref, sel=sel_ref, o=o_ref,
                st=st_ref, a=a_ref, b=b_ref, c=c_ref, p=p_ref, q0=q0_ref, kl=kl_ref, dl=dl_ref)

    st_ref[...] = jnp.zeros_like(st_ref)

    def run(n_out, n_score, n_decay):
        for hh in range(hpb):
            if n_out is not None:
                _hgrn_output_stage(n_out, hh, refs)
            if n_score is not None:
                _hgrn_score_stage(n_score, hh, refs, masks)
            if n_decay is not None:
                _hgrn_decay_stage(n_decay, hh, refs, masks)

    run(None, None, 0)
    run(None, 0, 1)

    def steady(n, carry):
        run(n - 2, n - 1, n)
        return carry

    lax.fori_loop(2, nc, steady, 0)
    run(nc - 2, nc - 1, None)
    run(nc - 1, None, None)


def _hgrn2(qs, f, iv, gs, gnorm, batch, seq, heads, hpb=4):
    n = batch * seq
    d = HG_DIM
    sel = (jnp.arange(SUB * d)[:, None] // d == jnp.arange(d)[None, :] % SUB).astype(BF16)
    spec = pl.BlockSpec((seq, hpb * d), lambda b, h: (b, h))
    return pl.pallas_call(
        functools.partial(_hgrn_kernel, hpb=hpb), grid=(batch, heads // hpb), name="hgrn2",
        in_specs=[spec, spec, spec, spec, pl.BlockSpec((1, d), lambda b, h: (0, 0)),
                  pl.BlockSpec((SUB * d, d), lambda b, h: (0, 0))],
        out_specs=spec,
        out_shape=jax.ShapeDtypeStruct((n, heads * d), BF16),
        scratch_shapes=[pltpu.VMEM((hpb, d, d), F32),
                        pltpu.VMEM((hpb, HG_CHUNK, SUB * d), BF16),
                        pltpu.VMEM((2, hpb, HG_CHUNK, d), F32),
                        pltpu.VMEM((2, hpb, HG_CHUNK, d), F32),
                        pltpu.VMEM((2, hpb, HG_CHUNK, d), BF16),
                        pltpu.VMEM((2, hpb, HG_CHUNK, d), BF16),
                        pltpu.VMEM((2, hpb, HG_CHUNK, d), BF16),
                        pltpu.VMEM((2, hpb, SUBLANES, d), F32)],
        compiler_params=_cparams(2),
    )(qs, f, iv, gs, gnorm.reshape(1, d), sel)


def _ep_silu(accs, extras):
    return [_silu(accs[0])]


def _ep_ident(accs, extras):
    return [accs[0]]


def _ep_forget(accs, extras):
    lbp = extras[0]
    m = jnp.max(lbp, axis=0, keepdims=True)
    e = jnp.exp(lbp - m)
    lb = e[0:1, :] / jnp.sum(e, axis=0, keepdims=True)
    return [lb + (1.0 - lb) * _sigmoid(accs[0])]


def _ep_merge(accs, extras):
    ga, a, gb, b = accs
    return [_sigmoid(ga) * a + _sigmoid(gb) * b]


def _ep_residual(accs, extras):
    y, mu, rstd, g, b = extras
    return [ALPHA * _ln_apply(y, mu[:, :1], rstd[:, :1], g, b) + accs[0]]


def _ep_swiglu(accs, extras):
    return [_silu(accs[0]) * accs[1]]


def _uq_weight_kernel(w_ref, o_ref):
    src = NOPE + ROPE
    pad = jnp.zeros((w_ref.shape[0], HEAD_W - src), o_ref.dtype)
    for h in range(MLA_HEADS):
        o_ref[:, h * HEAD_W:h * HEAD_W + src] = w_ref[:, h * src:(h + 1) * src].astype(o_ref.dtype)
        o_ref[:, h * HEAD_W + src:(h + 1) * HEAD_W] = pad


def _uq_weight(w_uq, tr=256):
    r = w_uq.shape[0]
    return pl.pallas_call(
        _uq_weight_kernel, grid=(r // tr,), name="q_up_weight_layout",
        in_specs=[pl.BlockSpec((tr, w_uq.shape[1]), lambda i: (i, 0))],
        out_specs=pl.BlockSpec((tr, MLA_HEADS * HEAD_W), lambda i: (i, 0)),
        out_shape=jax.ShapeDtypeStruct((r, MLA_HEADS * HEAD_W), BF16),
        compiler_params=_cparams(1),
    )(w_uq)


def kernel(x, positions, ln_in_g, ln_in_b, w_in, q_norm_g, w_uq, kv_norm_g, w_ukv, hg_lb,
           hg_norm_g, w_branch_a, w_branch_b, w_out, ln1_g, ln1_b, w_gate, w_up, w_down,
           ln2_g, ln2_b):
    batch, seq, d = x.shape
    n = batch * seq
    assert w_in.shape[0] == DEPTH == 1
    l = 0
    wt = w_in.reshape(d, w_in.shape[-1]).T

    o_kr = Q_LORA + KV_LORA
    o_hq = o_kr + ROPE
    o_hf, o_hi, o_hg = o_hq + HG_WIDTH, o_hq + 2 * HG_WIDTH, o_hq + 3 * HG_WIDTH
    o_ga = o_hq + 4 * HG_WIDTH
    o_gb = o_ga + D_MODEL

    x2 = x.reshape(n, d)
    h16, h_mu, h_rstd = _layer_norm("ln_in", x2, ln_in_g, ln_in_b, BF16, True)

    def residual_specs(y, mu, rstd, g, b, tm, tn):
        return [(y, (tm, tn), _tile_ij), (mu, (tm, LANES), _row_i), (rstd, (tm, LANES), _row_i),
                (g.reshape(1, d), (1, tn), _col_j), (b.reshape(1, d), (1, tn), _col_j)]

    lat_w = o_kr + LANES
    w_lat = jnp.concatenate([wt[:o_hq], jnp.zeros((LANES - ROPE, d), wt.dtype)],
                            axis=0).astype(BF16)
    tm = 1024
    table = ((n, LANES), F32, (tm, LANES), _row_i)
    qn, kvn, kr, rc, rsa, rsb = _matmul(
        "latent_proj", [h16], [(0, w_lat, "nk", 0, lat_w)], _latent_epilogue,
        [((n, Q_LORA), BF16, (tm, Q_LORA), _row_i),
         ((n, KV_LORA), BF16, (tm, KV_LORA), _row_i),
         ((n, LANES), BF16, (tm, LANES), _row_i), table, table, table],
        tm=tm, tn=lat_w,
        extras=[(q_norm_g[l].reshape(1, -1), (1, Q_LORA), lambda i, j: (0, 0)),
                (kv_norm_g[l].reshape(1, -1), (1, KV_LORA), lambda i, j: (0, 0)),
                (positions.reshape(n, 1), (tm, 1), _row_i),
                (_rope_frequencies(), (1, LANES), lambda i, j: (0, 0))])
    rope_specs = [(rc, (tm, LANES), _row_i), (rsa, (tm, LANES), _row_i), (rsb, (tm, LANES), _row_i)]

    scale = (NOPE + ROPE) ** -0.5
    q_w = MLA_HEADS * HEAD_W
    (qf,) = _matmul("q_up_proj", [qn], [(0, _uq_weight(w_uq[l]), "kn", 0, q_w)],
                    functools.partial(_uq_epilogue, scale=scale),
                    [((n, q_w), BF16, (tm, 2048), _tile_ij)],
                    tm=tm, tn=2048, extras=rope_specs)
    kv_w = MLA_HEADS * KV_W
    (kv,) = _matmul("kv_up_proj", [kvn], [(0, w_ukv[l], "kn", 0, kv_w)], _ep_ident,
                    [((n, kv_w), BF16, (2048, 2048), _tile_ij)], tm=2048, tn=2048)
    o_a = _attention(qf, kv, kr, batch, seq, MLA_HEADS)

    tm, tn = 1024, 1024

    def proj(name, col0, ep, dtype, extras=()):
        (r,) = _matmul(name, [h16], [(0, wt, "nk", col0, HG_WIDTH)], ep,
                       [((n, HG_WIDTH), dtype, (tm, tn), _tile_ij)],
                       tm=tm, tn=tn, extras=list(extras))
        return r

    qs = proj("hg_q_proj", o_hq, _ep_silu, BF16)
    fg = proj("hg_f_proj", o_hf, _ep_forget, F32, [(hg_lb, (hg_lb.shape[0], tn), _col_j)])
    iv = proj("hg_i_proj", o_hi, _ep_ident, BF16)
    gs = proj("hg_g_proj", o_hg, _ep_silu, BF16)
    o_b = _hgrn2(qs, fg, iv, gs, hg_norm_g[l], batch, seq, HG_HEADS)

    tm, tn = 512, 512
    (merged,) = _matmul(
        "gated_merge", [h16, o_a, o_b],
        [(0, wt, "nk", o_ga, d), (1, w_branch_a[l], "kn", 0, d),
         (0, wt, "nk", o_gb, d), (2, w_branch_b[l], "kn", 0, d)],
        _ep_merge, [((n, d), BF16, (tm, tn), _tile_ij)], tm=tm, tn=tn)

    tm, tn = 512, 1024
    (y1,) = _matmul("out_proj", [merged], [(0, w_out[l], "kn", 0, d)], _ep_residual,
                    [((n, d), F32, (tm, tn), _tile_ij)], tm=tm, tn=tn,
                    extras=residual_specs(x2, h_mu, h_rstd, ln_in_g, ln_in_b, tm, tn))
    h1_16, h1_mu, h1_rstd = _layer_norm("ln1", y1, ln1_g[l], ln1_b[l], BF16, True)

    tm, tn = 2048, 256
    (act,) = _matmul("swiglu_up", [h1_16], [(0, w_gate[l], "kn", 0, D_FF), (0, w_up[l], "kn", 0, D_FF)],
                     _ep_swiglu, [((n, D_FF), BF16, (tm, tn), _tile_ij)], tm=tm, tn=tn)
    tm, tn = 512, 512
    (y2,) = _matmul("swiglu_down", [act], [(0, w_down[l], "kn", 0, d)], _ep_residual,
                    [((n, d), F32, (tm, tn), _tile_ij)], tm=tm, tn=tn,
                    extras=residual_specs(y1, h1_mu, h1_rstd, ln1_g[l], ln1_b[l], tm, tn))
    (out,) = _layer_norm("ln2", y2, ln2_g[l], ln2_b[l], F32, False)
    return out.reshape(batch, seq, d)
```
